```python
import math
import jax, jax.numpy as jnp
from jax import lax
import numpy as np

D_MODEL = 1024
BATCH = 8
SEQ = 2048
DEPTH = 2

CHUNK = 64
N_MIXERS = 4
GROUP_WIDTH = D_MODEL // N_MIXERS
MIX_WIDTH = N_MIXERS * GROUP_WIDTH

GM_HEADS = 4
GM_HEAD_DIM = GROUP_WIDTH // GM_HEADS
GM_BLOCK = 128
MLA_HEADS = 4
MLA_NOPE_DIM = 64
MLA_ROPE_DIM = 32
MLA_QK_DIM = MLA_NOPE_DIM + MLA_ROPE_DIM
MLA_V_DIM = GROUP_WIDTH // MLA_HEADS
MLA_Q_RANK = 192
MLA_KV_RANK = 128
ROPE_THETA = 10000.0
ATTN_BLOCK = 128
S5_GROUP = 16
S5_GROUPS = GROUP_WIDTH // S5_GROUP
S5_STATE = 64
S5_DT_MIN = 1e-3
S5_DT_MAX = 1e-1
HG_HEADS = 4
HG_DK = GROUP_WIDTH // HG_HEADS
HG_DV = GROUP_WIDTH // HG_HEADS
D_FF = 2816
CONV_W = 3
DN_ALPHA = (2 * DEPTH) ** 0.25
DN_BETA = (8 * DEPTH) ** -0.25
ADA_SCALE = 0.25

IN_SIZES = (GROUP_WIDTH, GROUP_WIDTH, MLA_Q_RANK, MLA_KV_RANK, MLA_ROPE_DIM,
            GROUP_WIDTH, GROUP_WIDTH, GROUP_WIDTH, GROUP_WIDTH, GROUP_WIDTH)
IN_COLS = sum(IN_SIZES)
IN_SPLITS = tuple(sum(IN_SIZES[:i + 1]) for i in range(len(IN_SIZES) - 1))

kernel_name = "hybrid_streaming_encoder_block"


def layer_norm(x, g, b, eps=1e-5):
    xf = x.astype(jnp.float32)
    mu = jnp.mean(xf, -1, keepdims=True)
    var = jnp.mean(jnp.square(xf - mu), -1, keepdims=True)
    return ((xf - mu) * lax.rsqrt(var + eps)).astype(x.dtype) * g + b


def rms_norm(x, g, eps=1e-6):
    xf = x.astype(jnp.float32)
    return (xf * lax.rsqrt(jnp.mean(xf * xf, -1, keepdims=True) + eps)).astype(x.dtype) * g


def spatial_gating(u, v, ln_g, ln_b, w_s, b_s):
    bsz, L, _ = v.shape
    nb = L // GM_BLOCK
    vn = layer_norm(v, ln_g, ln_b).reshape(bsz, nb, GM_BLOCK, GM_HEADS, GM_HEAD_DIM)
    ck = jnp.arange(GM_BLOCK) // CHUNK
    mask = ck[None, :] <= ck[:, None]
    w = jnp.where(mask[None], w_s, 0).astype(v.dtype)
    z = jnp.einsum('hij,bnjhd->bnihd', w, vn) + b_s.T[:, :, None]
    return u * z.reshape(bsz, L, GROUP_WIDTH)


def rope_tables(L):
    inv = 1.0 / (ROPE_THETA ** (jnp.arange(0, MLA_ROPE_DIM, 2, dtype=jnp.float32) / MLA_ROPE_DIM))
    ang = jnp.arange(L, dtype=jnp.float32)[:, None] * inv[None, :]
    return jnp.cos(ang), jnp.sin(ang)


def apply_rope(x, cos, sin):
    x1, x2 = jnp.split(x.astype(jnp.float32), 2, axis=-1)
    cc, ss = cos[:, None, :], sin[:, None, :]
    return jnp.concatenate([x1 * cc - x2 * ss, x1 * ss + x2 * cc], -1).astype(x.dtype)


def chunk_causal_attention(q, k, v):
    bsz, L, H, dq = q.shape
    nb = L // ATTN_BLOCK
    qb = jnp.moveaxis(q.reshape(bsz, nb, ATTN_BLOCK, H, dq), 1, 0)
    kchunk = jnp.arange(L) // CHUNK

    def one_block(args):
        qi, bi = args
        qchunk = (bi * ATTN_BLOCK + jnp.arange(ATTN_BLOCK)) // CHUNK
        s = jnp.einsum('bqhd,bkhd->bhqk', qi, k, preferred_element_type=jnp.float32)
        s = jnp.where(kchunk[None, :] <= qchunk[:, None], s, -jnp.inf)
        p = jax.nn.softmax(s, axis=-1).astype(v.dtype)
        return jnp.einsum('bhqk,bkhd->bqhd', p, v)

    out = lax.map(one_block, (qb, jnp.arange(nb)))
    return jnp.moveaxis(out, 0, 1).reshape(bsz, L, H, v.shape[-1])


def mla(cq, ckv, kr, q_norm, kv_norm, w_uq, w_ukv):
    bsz, L, _ = cq.shape
    q = (rms_norm(cq, q_norm) @ w_uq).reshape(bsz, L, MLA_HEADS, MLA_QK_DIM)
    kv = (rms_norm(ckv, kv_norm) @ w_ukv).reshape(bsz, L, MLA_HEADS, MLA_NOPE_DIM + MLA_V_DIM)
    cos, sin = rope_tables(L)
    q_rope = apply_rope(q[..., MLA_NOPE_DIM:], cos, sin)
    k_rope = apply_rope(kr[:, :, None, :], cos, sin)
    k = jnp.concatenate([kv[..., :MLA_NOPE_DIM],
                         jnp.broadcast_to(k_rope, (bsz, L, MLA_HEADS, MLA_ROPE_DIM))], -1)
    v = kv[..., MLA_NOPE_DIM:]
    qf = jnp.concatenate([q[..., :MLA_NOPE_DIM], q_rope], -1) * (MLA_QK_DIM ** -0.5)
    return chunk_causal_attention(qf, k, v).reshape(bsz, L, GROUP_WIDTH)


def s5(u, a_re, a_im, b_re, b_im, c_re, c_im, d, log_step, w_glu, b_glu):
    f32 = jnp.float32
    bsz, L, _ = u.shape
    uf = u.astype(f32).reshape(bsz, L, S5_GROUPS, S5_GROUP)
    ar, ai = a_re.astype(f32), a_im.astype(f32)
    dt = jnp.exp(log_step.astype(f32))[:, None]
    mag = jnp.exp(dt * ar)
    abar_re, abar_im = mag * jnp.cos(dt * ai), mag * jnp.sin(dt * ai)
    den = ar * ar + ai * ai
    nr, ni = abar_re - 1.0, abar_im
    zoh_re = (nr * ar + ni * ai) / den
    zoh_im = (ni * ar - nr * ai) / den
    br, bi = b_re.astype(f32), b_im.astype(f32)
    bbar_re = zoh_re[..., None] * br - zoh_im[..., None] * bi
    bbar_im = zoh_re[..., None] * bi + zoh_im[..., None] * br
    bu_re = jnp.einsum('blgn,gpn->blgp', uf, bbar_re)
    bu_im = jnp.einsum('blgn,gpn->blgp', uf, bbar_im)
    a_r = jnp.broadcast_to(abar_re, bu_re.shape)
    a_i = jnp.broadcast_to(abar_im, bu_im.shape)

    def combine(e1, e2):
        a1r, a1i, b1r, b1i = e1
        a2r, a2i, b2r, b2i = e2
        return (a2r * a1r - a2i * a1i, a2r * a1i + a2i * a1r,
                a2r * b1r - a2i * b1i + b2r, a2r * b1i + a2i * b1r + b2i)

    _, _, hr, hi = lax.associative_scan(combine, (a_r, a_i, bu_re, bu_im), axis=1)
    y = (jnp.einsum('blgp,gnp->blgn', hr, c_re.astype(f32))
         - jnp.einsum('blgp,gnp->blgn', hi, c_im.astype(f32)))
    y = y + d.astype(f32).reshape(S5_GROUPS, S5_GROUP) * uf
    z = jax.nn.gelu(y.reshape(bsz, L, GROUP_WIDTH).astype(u.dtype))
    return z * jax.nn.sigmoid(z @ w_glu + b_glu)


def hgrn2(q, f_logit, i, g, lb, gn_g):
    f32 = jnp.float32
    bsz, L, _ = q.shape
    nc = L // CHUNK
    f = lb + (1.0 - lb) * jax.nn.sigmoid(f_logit.astype(f32))
    logf = jnp.log(f)
    shp = lambda t, dd: t.reshape(bsz, nc, CHUNK, HG_HEADS, dd)
    qc = shp(q.astype(f32), HG_DK)
    kc = shp(1.0 - f, HG_DK)
    vc = shp(i.astype(f32), HG_DV)
    b = jnp.cumsum(shp(logf, HG_DK), axis=2)
    b_last = b[:, :, -1]
    causal = jnp.tril(jnp.ones((CHUNK, CHUNK), dtype=bool))

    def intra(args):
        qn, kn, vn, bn = args
        rel = jnp.where(causal[None, :, :, None, None], bn[:, :, None] - bn[:, None, :], -jnp.inf)
        att = jnp.einsum('bthk,bshk,btshk->bhts', qn, kn, jnp.exp(rel))
        return jnp.einsum('bhts,bshv->bthv', att, vn)

    mv = lambda t: jnp.moveaxis(t, 1, 0)
    o_intra = jnp.moveaxis(lax.map(intra, (mv(qc), mv(kc), mv(vc), mv(b))), 0, 1)
    u_chunk = jnp.einsum('bnshk,bnshv->bnhkv', kc * jnp.exp(b_last[:, :, None] - b), vc)
    decay_chunk = jnp.exp(b_last)

    def step(S, inp):
        dch, uch = inp
        return dch[..., None] * S + uch, S

    S0 = jnp.zeros((bsz, HG_HEADS, HG_DK, HG_DV), f32)
    _, S_in = lax.scan(step, S0, (mv(decay_chunk), mv(u_chunk)))
    S_in = jnp.moveaxis(S_in, 0, 1)
    o_inter = jnp.einsum('bnthk,bnhkv->bnthv', qc * jnp.exp(b), S_in)
    o = rms_norm(o_intra + o_inter, gn_g.astype(f32).reshape(HG_HEADS, HG_DV))
    o = o.reshape(bsz, L, GROUP_WIDTH).astype(q.dtype)
    return o * jax.nn.sigmoid(g)


def conv_ffn(h, w_up, conv_w, conv_b, w_down):
    a = h @ w_up
    a = lax.conv_general_dilated(a, conv_w[:, None, :].astype(a.dtype), window_strides=(1,),
                                 padding=[(CONV_W - 1, 0)],
                                 dimension_numbers=('NWC', 'WIO', 'NWC'),
                                 feature_group_count=a.shape[-1]) + conv_b
    val, gate = jnp.split(a, 2, axis=-1)
    return (val * jax.nn.silu(gate)) @ w_down


def setup_inputs(seed: int = 0) -> dict:
    key = jax.random.key(seed)
    ks = iter(jax.random.split(key, 48))
    L = DEPTH
    D = D_MODEL
    GW = GROUP_WIDTH

    def nrm(shape, scale):
        return jax.random.normal(next(ks), shape, jnp.float32) * scale

    def gain(shape):
        return 1.0 + nrm(shape, 0.02)

    x = nrm((BATCH, SEQ, D), 1.0)
    c = nrm((BATCH, D), 1.0)
    ln_in_g = gain((D,))
    ln_in_b = nrm((D,), 0.02)
    hg_lb_logits = nrm((L, GW), 1.0)
    w_ada = nrm((L, D, 6 * D), ADA_SCALE * D ** -0.5)
    b_ada = nrm((L, 6 * D), 0.02)
    w_in = nrm((L, D, IN_COLS), D ** -0.5)
    gm_ln_g = gain((L, GW))
    gm_ln_b = nrm((L, GW), 0.02)
    gm_ws = nrm((L, GM_HEADS, GM_BLOCK, GM_BLOCK), GM_BLOCK ** -0.5)
    gm_bs = 1.0 + nrm((L, GM_HEADS, GM_BLOCK), 0.02)
    mla_q_norm = gain((L, MLA_Q_RANK))
    mla_kv_norm = gain((L, MLA_KV_RANK))
    mla_w_uq = nrm((L, MLA_Q_RANK, MLA_HEADS * MLA_QK_DIM), MLA_Q_RANK ** -0.5)
    mla_w_ukv = nrm((L, MLA_KV_RANK, MLA_HEADS * (MLA_NOPE_DIM + MLA_V_DIM)), MLA_KV_RANK ** -0.5)
    s5_a_re = -0.5 + nrm((L, S5_GROUPS, S5_STATE), 0.01)
    s5_a_im = math.pi * jnp.arange(S5_STATE, dtype=jnp.float32) + nrm((L, S5_GROUPS, S5_STATE), 0.01)
    s5_b_re = nrm((L, S5_GROUPS, S5_STATE, S5_GROUP), (2 * S5_GROUP) ** -0.5)
    s5_b_im = nrm((L, S5_GROUPS, S5_STATE, S5_GROUP), (2 * S5_GROUP) ** -0.5)
    s5_c_re = nrm((L, S5_GROUPS, S5_GROUP, S5_STATE), (2 * S5_STATE) ** -0.5)
    s5_c_im = nrm((L, S5_GROUPS, S5_GROUP, S5_STATE), (2 * S5_STATE) ** -0.5)
    s5_d = nrm((L, GW), 1.0)
    s5_log_step = jax.random.uniform(next(ks), (L, S5_GROUPS), jnp.float32,
                                     minval=math.log(S5_DT_MIN), maxval=math.log(S5_DT_MAX))
    s5_w_glu = nrm((L, GW, GW), GW ** -0.5)
    s5_b_glu = nrm((L, GW), 0.02)
    hg_norm_g = gain((L, GW))
    merge_g = gain((L, MIX_WIDTH))
    w_out = nrm((L, MIX_WIDTH, D), MIX_WIDTH ** -0.5 * DN_BETA)
    ln1_g = gain((L, D))
    ln1_b = nrm((L, D), 0.02)
    w_up = nrm((L, D, 2 * D_FF), D ** -0.5)
    conv_w = nrm((L, CONV_W, 2 * D_FF), CONV_W ** -0.5)
    conv_b = nrm((L, 2 * D_FF), 0.02)
    w_down = nrm((L, D_FF, D), D_FF ** -0.5 * DN_BETA)
    ln2_g = gain((L, D))
    ln2_b = nrm((L, D), 0.02)
    return {"x": x, "c": c, "ln_in_g": ln_in_g, "ln_in_b": ln_in_b, "hg_lb_logits": hg_lb_logits,
            "w_ada": w_ada, "b_ada": b_ada, "w_in": w_in, "gm_ln_g": gm_ln_g, "gm_ln_b": gm_ln_b,
            "gm_ws": gm_ws, "gm_bs": gm_bs, "mla_q_norm": mla_q_norm, "mla_kv_norm": mla_kv_norm,
            "mla_w_uq": mla_w_uq, "mla_w_ukv": mla_w_ukv, "s5_a_re": s5_a_re, "s5_a_im": s5_a_im,
            "s5_b_re": s5_b_re, "s5_b_im": s5_b_im, "s5_c_re": s5_c_re, "s5_c_im": s5_c_im,
            "s5_d": s5_d, "s5_log_step": s5_log_step, "s5_w_glu": s5_w_glu, "s5_b_glu": s5_b_glu,
            "hg_norm_g": hg_norm_g, "merge_g": merge_g, "w_out": w_out, "ln1_g": ln1_g,
            "ln1_b": ln1_b, "w_up": w_up, "conv_w": conv_w, "conv_b": conv_b, "w_down": w_down,
            "ln2_g": ln2_g, "ln2_b": ln2_b}


def reference(x, c, ln_in_g, ln_in_b, hg_lb_logits, w_ada, b_ada, w_in, gm_ln_g, gm_ln_b,
              gm_ws, gm_bs, mla_q_norm, mla_kv_norm, mla_w_uq, mla_w_ukv, s5_a_re, s5_a_im,
              s5_b_re, s5_b_im, s5_c_re, s5_c_im, s5_d, s5_log_step, s5_w_glu, s5_b_glu,
              hg_norm_g, merge_g, w_out, ln1_g, ln1_b, w_up, conv_w, conv_b, w_down,
              ln2_g, ln2_b):
    bsz, L, _ = x.shape
    x = layer_norm(x, ln_in_g, ln_in_b)
    c_act = jax.nn.silu(c)
    lb_cum = jnp.cumsum(jax.nn.softmax(hg_lb_logits.astype(jnp.float32), axis=0), axis=0)
    for l in range(DEPTH):
        mod = c_act @ w_ada[l] + b_ada[l]
        sh1, sc1, g1, sh2, sc2, g2 = [m[:, None, :] for m in jnp.split(mod, 6, axis=-1)]
        h = x * (1.0 + sc1) + sh1
        z = h @ w_in[l]
        a_u, a_v, b_cq, b_ckv, b_kr, c_u, d_q, d_f, d_i, d_g = jnp.split(z, IN_SPLITS, axis=-1)
        y_a = spatial_gating(jax.nn.gelu(a_u), jax.nn.gelu(a_v), gm_ln_g[l], gm_ln_b[l],
                             gm_ws[l], gm_bs[l])
        y_b = mla(b_cq, b_ckv, b_kr, mla_q_norm[l], mla_kv_norm[l], mla_w_uq[l], mla_w_ukv[l])
        y_c = s5(c_u, s5_a_re[l], s5_a_im[l], s5_b_re[l], s5_b_im[l], s5_c_re[l], s5_c_im[l],
                 s5_d[l], s5_log_step[l], s5_w_glu[l], s5_b_glu[l])
        y_d = hgrn2(d_q, d_f, d_i, d_g, lb_cum[l] - lb_cum[0], hg_norm_g[l])
        ym = jnp.concatenate([y_a, y_b, y_c, y_d], axis=-1).reshape(bsz, L, N_MIXERS, GROUP_WIDTH)
        ym = rms_norm(ym, merge_g[l].reshape(N_MIXERS, GROUP_WIDTH)).reshape(bsz, L, MIX_WIDTH)
        x = layer_norm(DN_ALPHA * x + (1.0 + g1) * (ym @ w_out[l]), ln1_g[l], ln1_b[l])
        h = x * (1.0 + sc2) + sh2
        x = layer_norm(DN_ALPHA * x + (1.0 + g2) * conv_ffn(h, w_up[l], conv_w[l], conv_b[l], w_down[l]),
                       ln2_g[l], ln2_b[l])
    return x
```

```python
import functools
import math

import jax
import jax.numpy as jnp
from jax import lax
from jax.experimental import pallas as pl
from jax.experimental.pallas import tpu as pltpu

F32 = jnp.float32
BF16 = jnp.bfloat16

D_MODEL = 1024
DEPTH = 2
CHUNK = 64
GROUP_WIDTH = 256
N_MIXERS = 4
GM_HEADS = 4
GM_BLOCK = 128
MLA_HEADS = 4
MLA_NOPE_DIM = 64
MLA_ROPE_DIM = 32
MLA_QK_DIM = MLA_NOPE_DIM + MLA_ROPE_DIM
MLA_V_DIM = 64
MLA_Q_RANK = 192
MLA_KV_RANK = 128
ROPE_THETA = 10000.0
S5_GROUP = 16
S5_GROUPS = 16
S5_STATE = 64
HG_HEADS = 4
HG_DK = 64
D_FF = 2816
CONV_W = 3
DN_ALPHA = (2 * DEPTH) ** 0.25
LN_EPS = 1e-5
RMS_EPS = 1e-6

HEAD_PAD = 128
QK_PAD = MLA_HEADS * HEAD_PAD
S5_NSTATE = S5_GROUPS * S5_STATE
HG_SUB = 16

COL_AU, COL_AV, COL_CU, COL_DQ = 0, 256, 512, 768
COL_CQ, COL_CKV, COL_KR = 1792, 2048, 2176
IN_EXT = 2304

VMEM_LIMIT = 56 * 1024 * 1024


def _sigmoid(x):
    return 1.0 / (1.0 + jnp.exp(-x))


def _gelu(x):
    return 0.5 * x * (1.0 + jnp.tanh(math.sqrt(2.0 / math.pi) * (x + 0.044715 * (x * x * x))))


def _split2(x):
    hi = x.astype(BF16)
    lo = (x - hi.astype(F32)).astype(BF16)
    return hi, lo


def _split3(x):
    hi = x.astype(BF16)
    r = x - hi.astype(F32)
    mid = r.astype(BF16)
    lo = (r - mid.astype(F32)).astype(BF16)
    return hi, mid, lo


def _dot(a, b):
    return jnp.dot(a, b, preferred_element_type=F32)


def _dot_nt(a, b):
    return lax.dot_general(a, b, (((1,), (1,)), ((), ())), preferred_element_type=F32)


def _dot_tn(a, b):
    return lax.dot_general(a, b, (((0,), (0,)), ((), ())), preferred_element_type=F32)


def _layer_norm(x, g, b):
    mu = jnp.mean(x, axis=-1, keepdims=True)
    xc = x - mu
    var = jnp.mean(xc * xc, axis=-1, keepdims=True)
    return xc * lax.rsqrt(var + LN_EPS) * g + b


def _cparams(sem, vmem=VMEM_LIMIT):
    return pltpu.CompilerParams(dimension_semantics=sem, vmem_limit_bytes=vmem)


def _ada_kernel(c_ref, w_ref, b_ref, o_ref):
    c = c_ref[...]
    chi, clo = _split2(c * _sigmoid(c))
    whi, wlo = _split2(w_ref[0])
    acc = _dot(chi, whi) + _dot(clo, whi) + _dot(chi, wlo)
    o_ref[0] = acc + b_ref[0]


def _ada(c, w_ada, b_ada):
    nl, d, n = w_ada.shape
    tn = 1536
    bsz = c.shape[0]
    return pl.pallas_call(
        _ada_kernel,
        out_shape=jax.ShapeDtypeStruct((nl, bsz, n), F32),
        grid=(nl, n // tn),
        in_specs=[pl.BlockSpec((bsz, d), lambda l, j: (0, 0)),
                  pl.BlockSpec((1, d, tn), lambda l, j: (l, 0, j)),
                  pl.BlockSpec((1, 1, tn), lambda l, j: (l, 0, j))],
        out_specs=pl.BlockSpec((1, bsz, tn), lambda l, j: (l, 0, j)),
        compiler_params=_cparams(("arbitrary", "arbitrary")),
        name="ada_mod",
    )(c, w_ada, b_ada.reshape(nl, 1, n))


def _ln_in_kernel(x_ref, g_ref, b_ref, o_ref):
    o_ref[...] = _layer_norm(x_ref[...], g_ref[...], b_ref[...])


def _ln_in(x2, g, b):
    m, d = x2.shape
    tm = 1024
    return pl.pallas_call(
        _ln_in_kernel,
        out_shape=jax.ShapeDtypeStruct((m, d), F32),
        grid=(m // tm,),
        in_specs=[pl.BlockSpec((tm, d), lambda i: (i, 0)),
                  pl.BlockSpec((1, d), lambda i: (0, 0)),
                  pl.BlockSpec((1, d), lambda i: (0, 0))],
        out_specs=pl.BlockSpec((tm, d), lambda i: (i, 0)),
        compiler_params=_cparams(("arbitrary",)),
        name="ln_in",
    )(x2, g.reshape(1, d), b.reshape(1, d))


def _inproj_kernel(x_ref, mod_ref, w_ref, gmg_ref, gmb_ref, gmw_ref, gmbias_ref,
                   qn_ref, kvn_ref, wq_ref, wkv_ref, tq_c_ref, tq_s_ref, tk_ref,
                   ya_ref, cu_ref, d_ref, q_ref, k_ref, v_ref, *, tm):
    x = x_ref[0]
    sh1 = mod_ref[0, :, 0:D_MODEL]
    sc1 = mod_ref[0, :, D_MODEL:2 * D_MODEL]
    h = (x * (1.0 + sc1) + sh1).astype(BF16)
    z = _dot(h, w_ref[...])

    cu_ref[0] = z[:, COL_CU:COL_CU + GROUP_WIDTH]
    d_ref[0] = z[:, COL_DQ:COL_DQ + 4 * GROUP_WIDTH]

    lane_head = lax.broadcasted_iota(jnp.int32, (GM_BLOCK, GROUP_WIDTH), 1) // (GROUP_WIDTH // GM_HEADS)
    for r in range(tm // GM_BLOCK):
        rows = slice(r * GM_BLOCK, (r + 1) * GM_BLOCK)
        u = _gelu(z[rows, COL_AU:COL_AU + GROUP_WIDTH])
        v = _gelu(z[rows, COL_AV:COL_AV + GROUP_WIDTH])
        vn = _layer_norm(v, gmg_ref[...], gmb_ref[...]).astype(BF16)
        zz = gmbias_ref[...]
        for hh in range(GM_HEADS):
            zz = zz + jnp.where(lane_head == hh, _dot(gmw_ref[hh], vn), 0.0)
        ya_ref[0, rows, :] = u * zz

    cq = z[:, COL_CQ:COL_CQ + 256]
    cqn = cq * lax.rsqrt(jnp.sum(cq * cq, axis=-1, keepdims=True) * (1.0 / MLA_Q_RANK) + RMS_EPS)
    cqn = (cqn * qn_ref[...]).astype(BF16)
    qq = _dot(cqn, wq_ref[...])
    tq_c = jnp.concatenate([tq_c_ref[...]] * MLA_HEADS, axis=1)
    tq_s = jnp.concatenate([tq_s_ref[...]] * MLA_HEADS, axis=1)
    q_ref[0] = (qq[:, :QK_PAD] * tq_c + qq[:, QK_PAD:] * tq_s).astype(BF16)

    ckv = z[:, COL_CKV:COL_CKV + MLA_KV_RANK]
    ckvn = ckv * lax.rsqrt(jnp.mean(ckv * ckv, axis=-1, keepdims=True) + RMS_EPS)
    ckvn = (ckvn * kvn_ref[...]).astype(BF16)
    krt = (z[:, COL_KR:COL_KR + 128] * tk_ref[...]).astype(BF16)
    kv = _dot(jnp.concatenate([ckvn, krt], axis=1), wkv_ref[...])
    k_ref[0] = kv[:, :QK_PAD].astype(BF16)
    for hh in range(MLA_HEADS):
        v_ref[0, hh] = kv[:, QK_PAD + hh * MLA_V_DIM:QK_PAD + (hh + 1) * MLA_V_DIM].astype(BF16)


def _inproj(x, mod3, w_ext, gm_g, gm_b, gm_w, gm_bias, qn, kvn, wq_ext, wkv_ext, tq_c, tq_s, tk):
    bsz, L, d = x.shape
    tm = 512
    const = lambda shape: pl.BlockSpec(shape, lambda b, t: (0,) * len(shape))
    out_shape = (
        jax.ShapeDtypeStruct((bsz, L, GROUP_WIDTH), F32),
        jax.ShapeDtypeStruct((bsz, L, GROUP_WIDTH), F32),
        jax.ShapeDtypeStruct((bsz, L, 4 * GROUP_WIDTH), F32),
        jax.ShapeDtypeStruct((bsz, L, QK_PAD), BF16),
        jax.ShapeDtypeStruct((bsz, L, QK_PAD), BF16),
        jax.ShapeDtypeStruct((bsz, MLA_HEADS, L, MLA_V_DIM), BF16),
    )
    row = lambda w: pl.BlockSpec((1, tm, w), lambda b, t: (b, t, 0))
    return pl.pallas_call(
        functools.partial(_inproj_kernel, tm=tm),
        out_shape=out_shape,
        grid=(bsz, L // tm),
        in_specs=[row(d),
                  pl.BlockSpec((1, 1, 6 * d), lambda b, t: (b, 0, 0)),
                  const((d, IN_EXT)),
                  const((1, GROUP_WIDTH)), const((1, GROUP_WIDTH)),
                  const((GM_HEADS, GM_BLOCK, GM_BLOCK)), const((GM_BLOCK, GROUP_WIDTH)),
                  const((1, 256)), const((1, MLA_KV_RANK)),
                  const((256, 2 * QK_PAD)), const((256, QK_PAD + 256)),
                  pl.BlockSpec((tm, HEAD_PAD), lambda b, t: (t, 0)),
                  pl.BlockSpec((tm, HEAD_PAD), lambda b, t: (t, 0)),
                  pl.BlockSpec((tm, 128), lambda b, t: (t, 0))],
        out_specs=(row(GROUP_WIDTH), row(GROUP_WIDTH), row(4 * GROUP_WIDTH), row(QK_PAD), row(QK_PAD),
                   pl.BlockSpec((1, MLA_HEADS, tm, MLA_V_DIM), lambda b, t: (b, 0, t, 0))),
        compiler_params=_cparams(("arbitrary", "arbitrary")),
        name="inproj",
    )(x, mod3, w_ext, gm_g, gm_b, gm_w, gm_bias, qn, kvn, wq_ext, wkv_ext, tq_c, tq_s, tk)


def _attn_kernel(q_ref, k_ref, v_ref, o_ref, m_ref, l_ref, acc_ref, *, tq, tk):
    qi = pl.program_id(1)
    m_ref[...] = jnp.full(m_ref.shape, -1e30, F32)
    l_ref[...] = jnp.zeros(l_ref.shape, F32)
    acc_ref[...] = jnp.zeros(acc_ref.shape, F32)
    q_chunk = (qi * tq + lax.broadcasted_iota(jnp.int32, (tq, tk), 0)) // CHUNK
    k_chunk0 = lax.broadcasted_iota(jnp.int32, (tq, tk), 1) // CHUNK

    def body(j, carry):
        start = pl.multiple_of(j * tk, tk)
        allowed = (k_chunk0 + j * (tk // CHUNK)) <= q_chunk
        for hh in range(MLA_HEADS):
            qh = q_ref[0, :, hh * HEAD_PAD:(hh + 1) * HEAD_PAD]
            kh = k_ref[0, pl.ds(start, tk), hh * HEAD_PAD:(hh + 1) * HEAD_PAD]
            s = jnp.where(allowed, _dot_nt(qh, kh), -1e30)
            m_old = m_ref[hh]
            m_new = jnp.maximum(m_old, jnp.max(s, axis=-1, keepdims=True))
            alpha = jnp.exp(m_old - m_new)
            p = jnp.exp(s - m_new)
            l_ref[hh] = alpha * l_ref[hh] + jnp.sum(p, axis=-1, keepdims=True)
            vh = v_ref[0, hh, pl.ds(start, tk), :]
            acc_ref[hh] = alpha * acc_ref[hh] + _dot(p.astype(BF16), vh)
            m_ref[hh] = m_new
        return carry

    lax.fori_loop(0, qi + 1, body, 0)
    o_ref[0] = jnp.concatenate([acc_ref[hh] / l_ref[hh] for hh in range(MLA_HEADS)], axis=1)


def _attention(q, k, v):
    bsz, L, _ = q.shape
    tq = tk = 256
    return pl.pallas_call(
        functools.partial(_attn_kernel, tq=tq, tk=tk),
        out_shape=jax.ShapeDtypeStruct((bsz, L, GROUP_WIDTH), F32),
        grid=(bsz, L // tq),
        in_specs=[pl.BlockSpec((1, tq, QK_PAD), lambda b, i: (b, i, 0)),
                  pl.BlockSpec((1, L, QK_PAD), lambda b, i: (b, 0, 0)),
                  pl.BlockSpec((1, MLA_HEADS, L, MLA_V_DIM), lambda b, i: (b, 0, 0, 0))],
        out_specs=pl.BlockSpec((1, tq, GROUP_WIDTH), lambda b, i: (b, i, 0)),
        scratch_shapes=[pltpu.VMEM((MLA_HEADS, tq, 1), F32),
                        pltpu.VMEM((MLA_HEADS, tq, 1), F32),
                        pltpu.VMEM((MLA_HEADS, tq, MLA_V_DIM), F32)],
        compiler_params=_cparams(("arbitrary", "arbitrary")),
        name="mla_attention",
    )(q, k, v)


def _s5_kernel(u_ref, bmat_ref, are_ref, aim_ref, cmat_ref, dvec_ref, wglu_ref, bglu_ref,
               o_ref, h_ref, bu_ref, hs_ref, *, tc, nb):
    @pl.when(pl.program_id(0) == 0)
    def _():
        h_ref[...] = jnp.zeros(h_ref.shape, F32)

    rows = tc * nb
    u = u_ref[...].reshape(rows, GROUP_WIDTH)
    bu_ref[...] = _dot(u.astype(BF16), bmat_ref[...])
    a_re = jnp.broadcast_to(are_ref[...], (nb, S5_NSTATE))
    a_im = jnp.broadcast_to(aim_ref[...], (nb, S5_NSTATE))

    def step(t, carry):
        h_re, h_im = carry
        r = pl.ds(pl.multiple_of(t * nb, nb), nb)
        n_re = a_re * h_re - a_im * h_im + bu_ref[r, 0:S5_NSTATE]
        n_im = a_re * h_im + a_im * h_re + bu_ref[r, S5_NSTATE:2 * S5_NSTATE]
        hs_ref[r, 0:S5_NSTATE] = n_re
        hs_ref[r, S5_NSTATE:2 * S5_NSTATE] = n_im
        return n_re, n_im

    h_re, h_im = lax.fori_loop(0, tc, step, (h_ref[:, 0:S5_NSTATE], h_ref[:, S5_NSTATE:2 * S5_NSTATE]))
    h_ref[:, 0:S5_NSTATE] = h_re
    h_ref[:, S5_NSTATE:2 * S5_NSTATE] = h_im

    y = _dot(hs_ref[...].astype(BF16), cmat_ref[...]) + dvec_ref[...] * u
    z = _gelu(y)
    gate = _sigmoid(_dot(z.astype(BF16), wglu_ref[...]) + bglu_ref[...])
    o_ref[...] = (z * gate).reshape(tc, nb, GROUP_WIDTH)


def _s5(u_tb, bmat, a_re, a_im, cmat, dvec, wglu, bglu):
    L, nb, gw = u_tb.shape
    tc = 128
    const = lambda shape: pl.BlockSpec(shape, lambda t: (0,) * len(shape))
    return pl.pallas_call(
        functools.partial(_s5_kernel, tc=tc, nb=nb),
        out_shape=jax.ShapeDtypeStruct((L, nb, gw), F32),
        grid=(L // tc,),
        in_specs=[pl.BlockSpec((tc, nb, gw), lambda t: (t, 0, 0)),
                  const((gw, 2 * S5_NSTATE)), const((1, S5_NSTATE)), const((1, S5_NSTATE)),
                  const((2 * S5_NSTATE, gw)), const((1, gw)), const((gw, gw)), const((1, gw))],
        out_specs=pl.BlockSpec((tc, nb, gw), lambda t: (t, 0, 0)),
        scratch_shapes=[pltpu.VMEM((nb, 2 * S5_NSTATE), F32),
                        pltpu.VMEM((tc * nb, 2 * S5_NSTATE), F32),
                        pltpu.VMEM((tc * nb, 2 * S5_NSTATE), F32)],
        compiler_params=_cparams(("arbitrary",)),
        name="s5_scan",
    )(u_tb, bmat, a_re, a_im, cmat, dvec, wglu, bglu)


def _hgrn_chunk(q, fl, iv, lb, tri, ones_bd, st):
    f = lb + (1.0 - lb) * _sigmoid(fl)
    logf = jnp.log(f)
    kk = 1.0 - f
    hi, mid, lo = _split3(logf)
    b = _dot(tri, hi) + _dot(tri, mid) + _dot(tri, lo)
    lane_head = lax.broadcasted_iota(jnp.int32, (HG_SUB, GROUP_WIDTH), 1) // HG_DK
    row_id = lax.broadcasted_iota(jnp.int32, (HG_SUB, GROUP_WIDTH), 0)
    row_all = lax.broadcasted_iota(jnp.int32, (CHUNK, GROUP_WIDTH), 0)
    v_bf = iv.astype(BF16)
    o_parts = []
    for si in range(CHUNK // HG_SUB):
        r0 = si * HG_SUB
        q_i, k_i, b_i, v_i = (a[r0:r0 + HG_SUB] for a in (q, kk, b, iv))
        terms = []
        for j in range(HG_SUB):
            e = jnp.exp(jnp.minimum(b_i - b_i[j:j + 1], 0.0))
            terms.append(jnp.where(row_id >= j, q_i * k_i[j:j + 1] * e, 0.0).astype(BF16))
        p = _dot(jnp.concatenate(terms, axis=0), ones_bd)
        o_i = p[0:HG_SUB] * v_i[0:1]
        for j in range(1, HG_SUB):
            o_i = o_i + p[j * HG_SUB:(j + 1) * HG_SUB] * v_i[j:j + 1]
        if si > 0:
            b_ref_row = b[r0 - 1:r0]
            q_t = q_i * jnp.exp(b_i - b_ref_row)
            k_t = jnp.where(row_all < r0, kk * jnp.exp(jnp.minimum(b_ref_row - b, 0.0)), 0.0).astype(BF16)
            q_exp = jnp.concatenate([jnp.where(lane_head == hh, q_t, 0.0) for hh in range(HG_HEADS)],
                                    axis=0).astype(BF16)
            att = _dot_nt(q_exp, k_t)
            o2 = _dot(att.astype(BF16), v_bf)
            for hh in range(HG_HEADS):
                o_i = o_i + jnp.where(lane_head == hh, o2[hh * HG_SUB:(hh + 1) * HG_SUB], 0.0)
        o_parts.append(o_i)
    o_intra = jnp.concatenate(o_parts, axis=0)
    o_inter = _dot_nt((q * jnp.exp(b)).astype(BF16), st.astype(BF16))
    b_last = b[CHUNK - 1:CHUNK]
    k_hat = (kk * jnp.exp(b_last - b)).astype(BF16)
    upd = _dot_tn(v_bf, k_hat)
    bd = (lax.broadcasted_iota(jnp.int32, (GROUP_WIDTH, GROUP_WIDTH), 0) // HG_DK
          == lax.broadcasted_iota(jnp.int32, (GROUP_WIDTH, GROUP_WIDTH), 1) // HG_DK)
    st_new = st * jnp.exp(b_last) + jnp.where(bd, upd, 0.0)
    return o_intra + o_inter, st_new


def _hgrn_kernel(d_ref, lb_ref, gn_ref, tri_ref, ones_ref, o_ref, st_ref, *, th):
    @pl.when(pl.program_id(1) == 0)
    def _():
        st_ref[...] = jnp.zeros(st_ref.shape, F32)

    lb = lb_ref[...]
    tri = tri_ref[...]
    ones_bd = ones_ref[...]
    for c in range(th // CHUNK):
        rows = slice(c * CHUNK, (c + 1) * CHUNK)
        q = d_ref[0, rows, 0:GROUP_WIDTH]
        fl = d_ref[0, rows, GROUP_WIDTH:2 * GROUP_WIDTH]
        iv = d_ref[0, rows, 2 * GROUP_WIDTH:3 * GROUP_WIDTH]
        g = d_ref[0, rows, 3 * GROUP_WIDTH:4 * GROUP_WIDTH]
        o, st_new = _hgrn_chunk(q, fl, iv, lb, tri, ones_bd, st_ref[...])
        st_ref[...] = st_new
        sq_hi, sq_lo = _split2(o * o)
        ms = (_dot(sq_hi, ones_bd) + _dot(sq_lo, ones_bd)) * (1.0 / HG_DK)
        o = o * lax.rsqrt(ms + RMS_EPS) * gn_ref[...]
        o_ref[0, rows, :] = o * _sigmoid(g)


def _hgrn(d, lb, gn, tri, ones_bd):
    bsz, L, _ = d.shape
    th = 256
    const = lambda shape: pl.BlockSpec(shape, lambda b, t: (0,) * len(shape))
    return pl.pallas_call(
        functools.partial(_hgrn_kernel, th=th),
        out_shape=jax.ShapeDtypeStruct((bsz, L, GROUP_WIDTH), F32),
        grid=(bsz, L // th),
        in_specs=[pl.BlockSpec((1, th, 4 * GROUP_WIDTH), lambda b, t: (b, t, 0)),
                  const((1, GROUP_WIDTH)), const((1, GROUP_WIDTH)),
                  const((CHUNK, CHUNK)), const((GROUP_WIDTH, GROUP_WIDTH))],
        out_specs=pl.BlockSpec((1, th, GROUP_WIDTH), lambda b, t: (b, t, 0)),
        scratch_shapes=[pltpu.VMEM((GROUP_WIDTH, GROUP_WIDTH), F32)],
        compiler_params=_cparams(("arbitrary", "arbitrary")),
        name="hgrn2",
    )(d, lb, gn, tri, ones_bd)


def _merge_kernel(ya_ref, yb_ref, yc_ref, yd_ref, x_ref, mod_ref, mg_ref, wo_ref, g_ref, b_ref, o_ref):
    acc = None
    for gi, y_ref in enumerate((ya_ref, yb_ref, yc_ref, yd_ref)):
        y = y_ref[0]
        yn = y * lax.rsqrt(jnp.mean(y * y, axis=-1, keepdims=True) + RMS_EPS)
        yn = (yn * mg_ref[:, gi * GROUP_WIDTH:(gi + 1) * GROUP_WIDTH]).astype(BF16)
        part = _dot(yn, wo_ref[gi * GROUP_WIDTH:(gi + 1) * GROUP_WIDTH, :])
        acc = part if acc is None else acc + part
    g1 = mod_ref[0, :, 2 * D_MODEL:3 * D_MODEL]
    o_ref[0] = _layer_norm(DN_ALPHA * x_ref[0] + (1.0 + g1) * acc, g_ref[...], b_ref[...])


def _merge(ya, yb, yc, yd, x, mod3, mg, wo, g, b):
    bsz, L, d = x.shape
    tm = 512
    const = lambda shape: pl.BlockSpec(shape, lambda bb, t: (0,) * len(shape))
    row = lambda w: pl.BlockSpec((1, tm, w), lambda bb, t: (bb, t, 0))
    return pl.pallas_call(
        _merge_kernel,
        out_shape=jax.ShapeDtypeStruct((bsz, L, d), F32),
        grid=(bsz, L // tm),
        in_specs=[row(GROUP_WIDTH)] * 4 + [row(d), pl.BlockSpec((1, 1, 6 * d), lambda bb, t: (bb, 0, 0)),
                                          const((1, d)), const((d, d)), const((1, d)), const((1, d))],
        out_specs=row(d),
        compiler_params=_cparams(("arbitrary", "arbitrary")),
        name="merge_out",
    )(ya, yb, yc, yd, x, mod3, mg, wo, g, b)


FF_TILE = 256
FF_HALO = 16


def _ffn_kernel(x_ref, xh_ref, mod_ref, wup_ref, cw_ref, cb_ref, wdn_ref, g_ref, b_ref, o_ref,
                hx_ref, a_ref, acc_ref, *, tm):
    ti = pl.program_id(1)
    sh2 = mod_ref[0, :, 3 * D_MODEL:4 * D_MODEL]
    sc2 = mod_ref[0, :, 4 * D_MODEL:5 * D_MODEL]
    g2 = mod_ref[0, :, 5 * D_MODEL:6 * D_MODEL]
    x = x_ref[0]
    halo = xh_ref[0] * (1.0 + sc2) + sh2
    hx_ref[0:FF_HALO, :] = jnp.where(ti > 0, halo, 0.0).astype(BF16)
    hx_ref[FF_HALO:FF_HALO + tm, :] = (x * (1.0 + sc2) + sh2).astype(BF16)
    acc_ref[...] = jnp.zeros(acc_ref.shape, F32)

    def body(j, carry):
        a_ref[...] = _dot(hx_ref[...], wup_ref[j])
        cw = cw_ref[j]
        a = (a_ref[pl.ds(FF_HALO - 2, tm), :] * cw[0:1] + a_ref[pl.ds(FF_HALO - 1, tm), :] * cw[1:2]
             + a_ref[pl.ds(FF_HALO, tm), :] * cw[2:3] + cb_ref[j])
        val = a[:, :FF_TILE]
        gate = a[:, FF_TILE:]
        act = (val * (gate * _sigmoid(gate))).astype(BF16)
        acc_ref[...] += _dot(act, wdn_ref[j])
        return carry

    lax.fori_loop(0, D_FF // FF_TILE, body, 0)
    o_ref[0] = _layer_norm(DN_ALPHA * x + (1.0 + g2) * acc_ref[...], g_ref[...], b_ref[...])


def _ffn(x, mod3, wup_r, cw_r, cb_r, wdn_r, g, b):
    bsz, L, d = x.shape
    tm = 512
    nj = D_FF // FF_TILE
    const = lambda shape: pl.BlockSpec(shape, lambda bb, t: (0,) * len(shape))
    hb = tm // FF_HALO
    return pl.pallas_call(
        functools.partial(_ffn_kernel, tm=tm),
        out_shape=jax.ShapeDtypeStruct((bsz, L, d), F32),
        grid=(bsz, L // tm),
        in_specs=[pl.BlockSpec((1, tm, d), lambda bb, t: (bb, t, 0)),
                  pl.BlockSpec((1, FF_HALO, d), lambda bb, t: (bb, jnp.maximum(t * hb - 1, 0), 0)),
                  pl.BlockSpec((1, 1, 6 * d), lambda bb, t: (bb, 0, 0)),
                  const((nj, d, 2 * FF_TILE)), const((nj, CONV_W, 2 * FF_TILE)), const((nj, 1, 2 * FF_TILE)),
                  const((nj, FF_TILE, d)), const((1, d)), const((1, d))],
        out_specs=pl.BlockSpec((1, tm, d), lambda bb, t: (bb, t, 0)),
        scratch_shapes=[pltpu.VMEM((tm + FF_HALO, d), BF16),
                        pltpu.VMEM((tm + FF_HALO, 2 * FF_TILE), F32),
                        pltpu.VMEM((tm, d), F32)],
        compiler_params=_cparams(("arbitrary", "arbitrary")),
        name="conv_ffn",
    )(x, x, mod3, wup_r, cw_r, cb_r, wdn_r, g, b)


def _rope_tables(L):
    inv = 1.0 / (ROPE_THETA ** (jnp.arange(0, MLA_ROPE_DIM, 2, dtype=F32) / MLA_ROPE_DIM))
    ang = jnp.arange(L, dtype=F32)[:, None] * inv[None, :]
    cos, sin = jnp.cos(ang), jnp.sin(ang)
    cc = jnp.concatenate([cos, cos], axis=1)
    ss = jnp.concatenate([-sin, sin], axis=1)
    scale = MLA_QK_DIM ** -0.5
    zeros32 = jnp.zeros((L, HEAD_PAD - MLA_QK_DIM), F32)
    tq_c = jnp.concatenate([jnp.ones((L, MLA_NOPE_DIM), F32), cc, zeros32], axis=1) * scale
    tq_s = jnp.concatenate([jnp.zeros((L, MLA_NOPE_DIM), F32), ss, zeros32], axis=1) * scale
    tk = jnp.concatenate([cc, ss, jnp.zeros((L, 64), F32)], axis=1)
    return tq_c, tq_s, tk


def _swap_halves(w):
    half = MLA_ROPE_DIM // 2
    return jnp.concatenate([w[..., half:], w[..., :half]], axis=-1)


def _prep_inproj_weights(w_in_l, w_uq_l, w_ukv_l, q_norm_l, kv_norm_l):
    d = w_in_l.shape[0]
    a_u, a_v, cq, ckv, kr, c_u, d_q, d_f, d_i, d_g = jnp.split(
        w_in_l, (256, 512, 704, 832, 864, 1120, 1376, 1632, 1888), axis=1)
    z64 = jnp.zeros((d, 64), F32)
    w_ext = jnp.concatenate([a_u, a_v, c_u, d_q, d_f, d_i, d_g, cq, z64, ckv, kr, _swap_halves(kr), z64],
                            axis=1).astype(BF16)
    wq = w_uq_l.reshape(MLA_Q_RANK, MLA_HEADS, MLA_QK_DIM)
    zq32 = jnp.zeros((MLA_Q_RANK, MLA_HEADS, HEAD_PAD - MLA_QK_DIM), F32)
    zq64 = jnp.zeros((MLA_Q_RANK, MLA_HEADS, MLA_NOPE_DIM), F32)
    wq_a = jnp.concatenate([wq, zq32], axis=2).reshape(MLA_Q_RANK, QK_PAD)
    wq_b = jnp.concatenate([zq64, _swap_halves(wq[:, :, MLA_NOPE_DIM:]), zq32], axis=2).reshape(MLA_Q_RANK, QK_PAD)
    wq_ext = jnp.concatenate([wq_a, wq_b], axis=1)
    wq_ext = jnp.concatenate([wq_ext, jnp.zeros((256 - MLA_Q_RANK, 2 * QK_PAD), F32)], axis=0).astype(BF16)
    qn = jnp.concatenate([q_norm_l, jnp.zeros((256 - MLA_Q_RANK,), F32)]).reshape(1, 256)
    wkv = w_ukv_l.reshape(MLA_KV_RANK, MLA_HEADS, MLA_NOPE_DIM + MLA_V_DIM)
    zk64 = jnp.zeros((MLA_KV_RANK, MLA_HEADS, HEAD_PAD - MLA_NOPE_DIM), F32)
    wk = jnp.concatenate([wkv[:, :, :MLA_NOPE_DIM], zk64], axis=2).reshape(MLA_KV_RANK, QK_PAD)
    wv = wkv[:, :, MLA_NOPE_DIM:].reshape(MLA_KV_RANK, MLA_HEADS * MLA_V_DIM)
    eye = jnp.eye(MLA_ROPE_DIM, dtype=F32)
    place_h = jnp.concatenate([jnp.zeros((MLA_ROPE_DIM, MLA_NOPE_DIM), F32), eye,
                               jnp.zeros((MLA_ROPE_DIM, HEAD_PAD - MLA_QK_DIM), F32)], axis=1)
    place = jnp.concatenate([place_h] * MLA_HEADS, axis=1)
    place = jnp.concatenate([place, jnp.zeros((MLA_ROPE_DIM, MLA_HEADS * MLA_V_DIM), F32)], axis=1)
    top = jnp.concatenate([wk, wv], axis=1)
    wkv_ext = jnp.concatenate([top, place, place, jnp.zeros((64, QK_PAD + 256), F32)], axis=0).astype(BF16)
    kvn = kv_norm_l.reshape(1, MLA_KV_RANK)
    return w_ext, wq_ext, wkv_ext, qn, kvn


def _prep_gm(gm_ws_l, gm_bs_l):
    ck = jnp.arange(GM_BLOCK) // CHUNK
    mask = ck[None, :] <= ck[:, None]
    w = jnp.where(mask[None], gm_ws_l, 0).astype(BF16)
    bias = jnp.repeat(gm_bs_l.T, GROUP_WIDTH // GM_HEADS, axis=1)
    return w, bias


def _prep_s5(a_re, a_im, b_re, b_im, c_re, c_im, log_step):
    dt = jnp.exp(log_step)[:, None]
    mag = jnp.exp(dt * a_re)
    abar_re, abar_im = mag * jnp.cos(dt * a_im), mag * jnp.sin(dt * a_im)
    den = a_re * a_re + a_im * a_im
    nr, ni = abar_re - 1.0, abar_im
    zoh_re = (nr * a_re + ni * a_im) / den
    zoh_im = (ni * a_re - nr * a_im) / den
    bbar_re = zoh_re[..., None] * b_re - zoh_im[..., None] * b_im
    bbar_im = zoh_re[..., None] * b_im + zoh_im[..., None] * b_re
    eye_g = jnp.eye(S5_GROUPS, dtype=F32)
    bm_re = jnp.einsum('gpn,gh->gnhp', bbar_re, eye_g).reshape(GROUP_WIDTH, S5_NSTATE)
    bm_im = jnp.einsum('gpn,gh->gnhp', bbar_im, eye_g).reshape(GROUP_WIDTH, S5_NSTATE)
    bmat = jnp.concatenate([bm_re, bm_im], axis=1).astype(BF16)
    cm_re = jnp.einsum('gnp,gh->gphn', c_re, eye_g).reshape(S5_NSTATE, GROUP_WIDTH)
    cm_im = jnp.einsum('gnp,gh->gphn', c_im, eye_g).reshape(S5_NSTATE, GROUP_WIDTH)
    cmat = jnp.concatenate([cm_re, -cm_im], axis=0).astype(BF16)
    return bmat, abar_re.reshape(1, S5_NSTATE), abar_im.reshape(1, S5_NSTATE), cmat


def kernel(x, c, ln_in_g, ln_in_b, hg_lb_logits, w_ada, b_ada, w_in, gm_ln_g, gm_ln_b, gm_ws, gm_bs, mla_q_norm, mla_kv_norm, mla_w_uq, mla_w_ukv, s5_a_re, s5_a_im, s5_b_re, s5_b_im, s5_c_re, s5_c_im, s5_d, s5_log_step, s5_w_glu, s5_b_glu, hg_norm_g, merge_g, w_out, ln1_g, ln1_b, w_up, conv_w, conv_b, w_down, ln2_g, ln2_b):
    bsz, L, d = x.shape
    nl = w_in.shape[0]
    mod = _ada(c, w_ada, b_ada)
    xs = _ln_in(x.reshape(bsz * L, d), ln_in_g, ln_in_b).reshape(bsz, L, d)
    lb_cum = jnp.cumsum(jax.nn.softmax(hg_lb_logits.astype(F32), axis=0), axis=0)
    tq_c, tq_s, tk = _rope_tables(L)
    tri = jnp.tril(jnp.ones((CHUNK, CHUNK), F32)).astype(BF16)
    hid = jnp.arange(GROUP_WIDTH) // HG_DK
    ones_bd = (hid[:, None] == hid[None, :]).astype(BF16)
    nj = D_FF // FF_TILE
    row = lambda v: v.reshape(1, -1)

    for l in range(nl):
        mod3 = mod[l].reshape(bsz, 1, 6 * d)
        w_ext, wq_ext, wkv_ext, qn, kvn = _prep_inproj_weights(w_in[l], mla_w_uq[l], mla_w_ukv[l],
                                                              mla_q_norm[l], mla_kv_norm[l])
        gm_w, gm_bias = _prep_gm(gm_ws[l], gm_bs[l])
        y_a, c_u, dd, q, k, v = _inproj(xs, mod3, w_ext, row(gm_ln_g[l]), row(gm_ln_b[l]), gm_w, gm_bias,
                                        qn, kvn, wq_ext, wkv_ext, tq_c, tq_s, tk)
        y_b = _attention(q, k, v)
        bmat, a_re, a_im, cmat = _prep_s5(s5_a_re[l], s5_a_im[l], s5_b_re[l], s5_b_im[l], s5_c_re[l],
                                          s5_c_im[l], s5_log_step[l])
        y_c = _s5(jnp.swapaxes(c_u, 0, 1), bmat, a_re, a_im, cmat, row(s5_d[l]), s5_w_glu[l].astype(BF16),
                  row(s5_b_glu[l]))
        y_c = jnp.swapaxes(y_c, 0, 1)
        y_d = _hgrn(dd, row(lb_cum[l] - lb_cum[0]), row(hg_norm_g[l]), tri, ones_bd)
        xs = _merge(y_a, y_b, y_c, y_d, xs, mod3, row(merge_g[l]), w_out[l].astype(BF16), row(ln1_g[l]),
                    row(ln1_b[l]))
        wup_r = jnp.concatenate([w_up[l][:, :D_FF].reshape(d, nj, FF_TILE),
                                 w_up[l][:, D_FF:].reshape(d, nj, FF_TILE)], axis=2)
        wup_r = jnp.transpose(wup_r, (1, 0, 2)).astype(BF16)
        cw_r = jnp.concatenate([conv_w[l][:, :D_FF].reshape(CONV_W, nj, FF_TILE),
                                conv_w[l][:, D_FF:].reshape(CONV_W, nj, FF_TILE)], axis=2)
        cw_r = jnp.transpose(cw_r, (1, 0, 2))
        cb_r = jnp.concatenate([conv_b[l][:D_FF].reshape(nj, 1, FF_TILE),
                                conv_b[l][D_FF:].reshape(nj, 1, FF_TILE)], axis=2)
        wdn_r = w_down[l].reshape(nj, FF_TILE, d).astype(BF16)
        xs = _ffn(xs, mod3, wup_r, cw_r, cb_r, wdn_r, row(ln2_g[l]), row(ln2_b[l]))
    return xs
```

```python
import functools
import math

import jax
import jax.numpy as jnp
from jax import lax
from jax.experimental import pallas as pl
from jax.experimental.pallas import tpu as pltpu

F32 = jnp.float32
BF16 = jnp.bfloat16

D_MODEL = 1024
DEPTH = 2
CHUNK = 64
GROUP_WIDTH = 256
N_MIXERS = 4
GM_HEADS = 4
GM_BLOCK = 128
MLA_HEADS = 4
MLA_NOPE_DIM = 64
MLA_ROPE_DIM = 32
MLA_QK_DIM = MLA_NOPE_DIM + MLA_ROPE_DIM
MLA_V_DIM = 64
MLA_Q_RANK = 192
MLA_KV_RANK = 128
ROPE_THETA = 10000.0
S5_GROUP = 16
S5_GROUPS = 16
S5_STATE = 64
HG_HEADS = 4
HG_DK = 64
D_FF = 2816
CONV_W = 3
DN_ALPHA = (2 * DEPTH) ** 0.25
LN_EPS = 1e-5
RMS_EPS = 1e-6

HEAD_PAD = 128
QK_PAD = MLA_HEADS * HEAD_PAD
S5_NSTATE = S5_GROUPS * S5_STATE
HG_SUB = 16

COL_AU, COL_AV, COL_CU, COL_DQ = 0, 256, 512, 768
COL_CQ, COL_CKV, COL_KR = 1792, 2048, 2176
IN_EXT = 2304

VMEM_LIMIT = 56 * 1024 * 1024


def _sigmoid(x):
    return 1.0 / (1.0 + jnp.exp(-x))


def _gelu(x):
    return 0.5 * x * (1.0 + jnp.tanh(math.sqrt(2.0 / math.pi) * (x + 0.044715 * (x * x * x))))


def _split2(x):
    hi = x.astype(BF16)
    lo = (x - hi.astype(F32)).astype(BF16)
    return hi, lo


def _split3(x):
    hi = x.astype(BF16)
    r = x - hi.astype(F32)
    mid = r.astype(BF16)
    lo = (r - mid.astype(F32)).astype(BF16)
    return hi, mid, lo


def _dot(a, b):
    return jnp.dot(a, b, preferred_element_type=F32)


def _dot_nt(a, b):
    return lax.dot_general(a, b, (((1,), (1,)), ((), ())), preferred_element_type=F32)


def _dot_tn(a, b):
    return lax.dot_general(a, b, (((0,), (0,)), ((), ())), preferred_element_type=F32)


def _layer_norm(x, g, b):
    mu = jnp.mean(x, axis=-1, keepdims=True)
    xc = x - mu
    var = jnp.mean(xc * xc, axis=-1, keepdims=True)
    return xc * lax.rsqrt(var + LN_EPS) * g + b


def _cparams(sem, vmem=VMEM_LIMIT):
    return pltpu.CompilerParams(dimension_semantics=sem, vmem_limit_bytes=vmem)


def _ada_kernel(c_ref, w_ref, b_ref, o_ref):
    c = c_ref[...]
    chi, clo = _split2(c * _sigmoid(c))
    whi, wlo = _split2(w_ref[0])
    acc = _dot(chi, whi) + _dot(clo, whi) + _dot(chi, wlo)
    o_ref[0] = acc + b_ref[0]


def _ada(c, w_ada, b_ada):
    nl, d, n = w_ada.shape
    tn = 1536
    bsz = c.shape[0]
    return pl.pallas_call(
        _ada_kernel,
        out_shape=jax.ShapeDtypeStruct((nl, bsz, n), F32),
        grid=(nl, n // tn),
        in_specs=[pl.BlockSpec((bsz, d), lambda l, j: (0, 0)),
                  pl.BlockSpec((1, d, tn), lambda l, j: (l, 0, j)),
                  pl.BlockSpec((1, 1, tn), lambda l, j: (l, 0, j))],
        out_specs=pl.BlockSpec((1, bsz, tn), lambda l, j: (l, 0, j)),
        compiler_params=_cparams(("arbitrary", "arbitrary")),
        name="ada_mod",
    )(c, w_ada, b_ada.reshape(nl, 1, n))


def _ln_in_kernel(x_ref, g_ref, b_ref, o_ref):
    o_ref[...] = _layer_norm(x_ref[...], g_ref[...], b_ref[...])


def _ln_in(x2, g, b):
    m, d = x2.shape
    tm = 1024
    return pl.pallas_call(
        _ln_in_kernel,
        out_shape=jax.ShapeDtypeStruct((m, d), F32),
        grid=(m // tm,),
        in_specs=[pl.BlockSpec((tm, d), lambda i: (i, 0)),
                  pl.BlockSpec((1, d), lambda i: (0, 0)),
                  pl.BlockSpec((1, d), lambda i: (0, 0))],
        out_specs=pl.BlockSpec((tm, d), lambda i: (i, 0)),
        compiler_params=_cparams(("arbitrary",)),
        name="ln_in",
    )(x2, g.reshape(1, d), b.reshape(1, d))


def _inproj_kernel(x_ref, mod_ref, w_ref, gmg_ref, gmb_ref, gmw_ref, gmbias_ref,
                   qn_ref, kvn_ref, wq_ref, wkv_ref, tq_c_ref, tq_s_ref, tk_ref,
                   ya_ref, cu_ref, d_ref, q_ref, k_ref, v_ref, *, tm):
    x = x_ref[0]
    sh1 = mod_ref[0, :, 0:D_MODEL]
    sc1 = mod_ref[0, :, D_MODEL:2 * D_MODEL]
    h = (x * (1.0 + sc1) + sh1).astype(BF16)
    z = _dot(h, w_ref[...])

    cu_ref[0] = z[:, COL_CU:COL_CU + GROUP_WIDTH]
    d_ref[0] = z[:, COL_DQ:COL_DQ + 4 * GROUP_WIDTH]

    lane_head = lax.broadcasted_iota(jnp.int32, (GM_BLOCK, GROUP_WIDTH), 1) // (GROUP_WIDTH // GM_HEADS)
    for r in range(tm // GM_BLOCK):
        rows = slice(r * GM_BLOCK, (r + 1) * GM_BLOCK)
        u = _gelu(z[rows, COL_AU:COL_AU + GROUP_WIDTH])
        v = _gelu(z[rows, COL_AV:COL_AV + GROUP_WIDTH])
        vn = _layer_norm(v, gmg_ref[...], gmb_ref[...]).astype(BF16)
        zz = gmbias_ref[...]
        for hh in range(GM_HEADS):
            zz = zz + jnp.where(lane_head == hh, _dot(gmw_ref[hh], vn), 0.0)
        ya_ref[0, rows, :] = u * zz

    cq = z[:, COL_CQ:COL_CQ + 256]
    cqn = cq * lax.rsqrt(jnp.sum(cq * cq, axis=-1, keepdims=True) * (1.0 / MLA_Q_RANK) + RMS_EPS)
    cqn = (cqn * qn_ref[...]).astype(BF16)
    qq = _dot(cqn, wq_ref[...])
    tq_c = jnp.concatenate([tq_c_ref[...]] * MLA_HEADS, axis=1)
    tq_s = jnp.concatenate([tq_s_ref[...]] * MLA_HEADS, axis=1)
    q_ref[0] = (qq[:, :QK_PAD] * tq_c + qq[:, QK_PAD:] * tq_s).astype(BF16)

    ckv = z[:, COL_CKV:COL_CKV + MLA_KV_RANK]
    ckvn = ckv * lax.rsqrt(jnp.mean(ckv * ckv, axis=-1, keepdims=True) + RMS_EPS)
    ckvn = (ckvn * kvn_ref[...]).astype(BF16)
    krt = (z[:, COL_KR:COL_KR + 128] * tk_ref[...]).astype(BF16)
    kv = _dot(jnp.concatenate([ckvn, krt], axis=1), wkv_ref[...])
    k_ref[0] = kv[:, :QK_PAD].astype(BF16)
    for hh in range(MLA_HEADS):
        v_ref[0, hh] = kv[:, QK_PAD + hh * MLA_V_DIM:QK_PAD + (hh + 1) * MLA_V_DIM].astype(BF16)


def _inproj(x, mod3, w_ext, gm_g, gm_b, gm_w, gm_bias, qn, kvn, wq_ext, wkv_ext, tq_c, tq_s, tk):
    bsz, L, d = x.shape
    tm = 512
    const = lambda shape: pl.BlockSpec(shape, lambda b, t: (0,) * len(shape))
    out_shape = (
        jax.ShapeDtypeStruct((bsz, L, GROUP_WIDTH), F32),
        jax.ShapeDtypeStruct((bsz, L, GROUP_WIDTH), F32),
        jax.ShapeDtypeStruct((bsz, L, 4 * GROUP_WIDTH), F32),
        jax.ShapeDtypeStruct((bsz, L, QK_PAD), BF16),
        jax.ShapeDtypeStruct((bsz, L, QK_PAD), BF16),
        jax.ShapeDtypeStruct((bsz, MLA_HEADS, L, MLA_V_DIM), BF16),
    )
    row = lambda w: pl.BlockSpec((1, tm, w), lambda b, t: (b, t, 0))
    return pl.pallas_call(
        functools.partial(_inproj_kernel, tm=tm),
        out_shape=out_shape,
        grid=(bsz, L // tm),
        in_specs=[row(d),
                  pl.BlockSpec((1, 1, 6 * d), lambda b, t: (b, 0, 0)),
                  const((d, IN_EXT)),
                  const((1, GROUP_WIDTH)), const((1, GROUP_WIDTH)),
                  const((GM_HEADS, GM_BLOCK, GM_BLOCK)), const((GM_BLOCK, GROUP_WIDTH)),
                  const((1, 256)), const((1, MLA_KV_RANK)),
                  const((256, 2 * QK_PAD)), const((256, QK_PAD + 256)),
                  pl.BlockSpec((tm, HEAD_PAD), lambda b, t: (t, 0)),
                  pl.BlockSpec((tm, HEAD_PAD), lambda b, t: (t, 0)),
                  pl.BlockSpec((tm, 128), lambda b, t: (t, 0))],
        out_specs=(row(GROUP_WIDTH), row(GROUP_WIDTH), row(4 * GROUP_WIDTH), row(QK_PAD), row(QK_PAD),
                   pl.BlockSpec((1, MLA_HEADS, tm, MLA_V_DIM), lambda b, t: (b, 0, t, 0))),
        compiler_params=_cparams(("arbitrary", "arbitrary")),
        name="inproj",
    )(x, mod3, w_ext, gm_g, gm_b, gm_w, gm_bias, qn, kvn, wq_ext, wkv_ext, tq_c, tq_s, tk)


ATT_TQ = 256
ATT_HEADS_PER_STEP = 2


def _attn_kernel(q_ref, k_ref, v_ref, o_ref):
    L = q_ref.shape[1]
    tq = ATT_TQ
    q_chunk = lax.broadcasted_iota(jnp.int32, (tq, tq), 0) // CHUNK
    k_chunk = lax.broadcasted_iota(jnp.int32, (tq, tq), 1) // CHUNK
    allowed = k_chunk <= q_chunk
    for qi in range(L // tq):
        lk = (qi + 1) * tq
        outs = []
        for hh in range(ATT_HEADS_PER_STEP):
            lanes = slice(hh * HEAD_PAD, (hh + 1) * HEAD_PAD)
            qh = q_ref[0, qi * tq:(qi + 1) * tq, lanes]
            s = _dot_nt(qh, k_ref[0, 0:lk, lanes])
            s_diag = jnp.where(allowed, s[:, lk - tq:], -1e30)
            s = s_diag if qi == 0 else jnp.concatenate([s[:, :lk - tq], s_diag], axis=1)
            m = jnp.max(s, axis=-1, keepdims=True)
            p = jnp.exp(s - m)
            l = jnp.sum(p, axis=-1, keepdims=True)
            outs.append(_dot(p.astype(BF16), v_ref[0, hh, 0:lk, :]) / l)
        o_ref[0, qi * tq:(qi + 1) * tq, :] = jnp.concatenate(outs, axis=1)


def _attention(q, k, v):
    bsz, L, _ = q.shape
    hp = ATT_HEADS_PER_STEP
    return pl.pallas_call(
        _attn_kernel,
        out_shape=jax.ShapeDtypeStruct((bsz, L, GROUP_WIDTH), F32),
        grid=(bsz, MLA_HEADS // hp),
        in_specs=[pl.BlockSpec((1, L, hp * HEAD_PAD), lambda b, g: (b, 0, g)),
                  pl.BlockSpec((1, L, hp * HEAD_PAD), lambda b, g: (b, 0, g)),
                  pl.BlockSpec((1, hp, L, MLA_V_DIM), lambda b, g: (b, g, 0, 0))],
        out_specs=pl.BlockSpec((1, L, hp * MLA_V_DIM), lambda b, g: (b, 0, g)),
        compiler_params=_cparams(("arbitrary", "arbitrary")),
        name="mla_attention",
    )(q, k, v)


def _s5_kernel(u_ref, bmat_ref, are_ref, aim_ref, cmat_ref, dvec_ref, wglu_ref, bglu_ref,
               o_ref, h_ref, bu_ref, hs_ref, *, tc, nb):
    @pl.when(pl.program_id(0) == 0)
    def _():
        h_ref[...] = jnp.zeros(h_ref.shape, F32)

    rows = tc * nb
    u = u_ref[...].reshape(rows, GROUP_WIDTH)
    bu_ref[...] = _dot(u.astype(BF16), bmat_ref[...])
    a_re = jnp.broadcast_to(are_ref[...], (nb, S5_NSTATE))
    a_im = jnp.broadcast_to(aim_ref[...], (nb, S5_NSTATE))

    def step(t, carry):
        h_re, h_im = carry
        r = pl.ds(pl.multiple_of(t * nb, nb), nb)
        n_re = a_re * h_re - a_im * h_im + bu_ref[r, 0:S5_NSTATE]
        n_im = a_re * h_im + a_im * h_re + bu_ref[r, S5_NSTATE:2 * S5_NSTATE]
        hs_ref[r, 0:S5_NSTATE] = n_re
        hs_ref[r, S5_NSTATE:2 * S5_NSTATE] = n_im
        return n_re, n_im

    h_re, h_im = lax.fori_loop(0, tc, step, (h_ref[:, 0:S5_NSTATE], h_ref[:, S5_NSTATE:2 * S5_NSTATE]))
    h_ref[:, 0:S5_NSTATE] = h_re
    h_ref[:, S5_NSTATE:2 * S5_NSTATE] = h_im

    y = _dot(hs_ref[...].astype(BF16), cmat_ref[...]) + dvec_ref[...] * u
    z = _gelu(y)
    gate = _sigmoid(_dot(z.astype(BF16), wglu_ref[...]) + bglu_ref[...])
    o_ref[...] = (z * gate).reshape(tc, nb, GROUP_WIDTH)


def _s5(u_tb, bmat, a_re, a_im, cmat, dvec, wglu, bglu):
    L, nb, gw = u_tb.shape
    tc = 128
    const = lambda shape: pl.BlockSpec(shape, lambda t: (0,) * len(shape))
    return pl.pallas_call(
        functools.partial(_s5_kernel, tc=tc, nb=nb),
        out_shape=jax.ShapeDtypeStruct((L, nb, gw), F32),
        grid=(L // tc,),
        in_specs=[pl.BlockSpec((tc, nb, gw), lambda t: (t, 0, 0)),
                  const((gw, 2 * S5_NSTATE)), const((1, S5_NSTATE)), const((1, S5_NSTATE)),
                  const((2 * S5_NSTATE, gw)), const((1, gw)), const((gw, gw)), const((1, gw))],
        out_specs=pl.BlockSpec((tc, nb, gw), lambda t: (t, 0, 0)),
        scratch_shapes=[pltpu.VMEM((nb, 2 * S5_NSTATE), F32),
                        pltpu.VMEM((tc * nb, 2 * S5_NSTATE), F32),
                        pltpu.VMEM((tc * nb, 2 * S5_NSTATE), F32)],
        compiler_params=_cparams(("arbitrary",)),
        name="s5_scan",
    )(u_tb, bmat, a_re, a_im, cmat, dvec, wglu, bglu)


def _hgrn_chunk(q, fl, iv, lb, tri, ones_bd, st):
    f = lb + (1.0 - lb) * _sigmoid(fl)
    logf = jnp.log(f)
    kk = 1.0 - f
    hi, mid, lo = _split3(logf)
    b = _dot(tri, hi) + _dot(tri, mid) + _dot(tri, lo)
    lane_head = lax.broadcasted_iota(jnp.int32, (HG_SUB, GROUP_WIDTH), 1) // HG_DK
    row_id = lax.broadcasted_iota(jnp.int32, (HG_SUB, GROUP_WIDTH), 0)
    row_all = lax.broadcasted_iota(jnp.int32, (CHUNK, GROUP_WIDTH), 0)
    v_bf = iv.astype(BF16)
    o_parts = []
    for si in range(CHUNK // HG_SUB):
        r0 = si * HG_SUB
        q_i, k_i, b_i, v_i = (a[r0:r0 + HG_SUB] for a in (q, kk, b, iv))
        terms = []
        for j in range(HG_SUB):
            e = jnp.exp(jnp.minimum(b_i - b_i[j:j + 1], 0.0))
            terms.append(jnp.where(row_id >= j, q_i * k_i[j:j + 1] * e, 0.0).astype(BF16))
        p = _dot(jnp.concatenate(terms, axis=0), ones_bd)
        o_i = p[0:HG_SUB] * v_i[0:1]
        for j in range(1, HG_SUB):
            o_i = o_i + p[j * HG_SUB:(j + 1) * HG_SUB] * v_i[j:j + 1]
        if si > 0:
            b_ref_row = b[r0 - 1:r0]
            q_t = q_i * jnp.exp(b_i - b_ref_row)
            k_t = jnp.where(row_all < r0, kk * jnp.exp(jnp.minimum(b_ref_row - b, 0.0)), 0.0).astype(BF16)
            q_exp = jnp.concatenate([jnp.where(lane_head == hh, q_t, 0.0) for hh in range(HG_HEADS)],
                                    axis=0).astype(BF16)
            att = _dot_nt(q_exp, k_t)
            o2 = _dot(att.astype(BF16), v_bf)
            for hh in range(HG_HEADS):
                o_i = o_i + jnp.where(lane_head == hh, o2[hh * HG_SUB:(hh + 1) * HG_SUB], 0.0)
        o_parts.append(o_i)
    o_intra = jnp.concatenate(o_parts, axis=0)
    o_inter = _dot_nt((q * jnp.exp(b)).astype(BF16), st.astype(BF16))
    b_last = b[CHUNK - 1:CHUNK]
    k_hat = (kk * jnp.exp(b_last - b)).astype(BF16)
    upd = _dot_tn(v_bf, k_hat)
    bd = (lax.broadcasted_iota(jnp.int32, (GROUP_WIDTH, GROUP_WIDTH), 0) // HG_DK
          == lax.broadcasted_iota(jnp.int32, (GROUP_WIDTH, GROUP_WIDTH), 1) // HG_DK)
    st_new = st * jnp.exp(b_last) + jnp.where(bd, upd, 0.0)
    return o_intra + o_inter, st_new


def _hgrn_kernel(d_ref, lb_ref, gn_ref, tri_ref, ones_ref, o_ref, st_ref, *, th):
    @pl.when(pl.program_id(1) == 0)
    def _():
        st_ref[...] = jnp.zeros(st_ref.shape, F32)

    lb = lb_ref[...]
    tri = tri_ref[...]
    ones_bd = ones_ref[...]
    for c in range(th // CHUNK):
        rows = slice(c * CHUNK, (c + 1) * CHUNK)
        q = d_ref[0, rows, 0:GROUP_WIDTH]
        fl = d_ref[0, rows, GROUP_WIDTH:2 * GROUP_WIDTH]
        iv = d_ref[0, rows, 2 * GROUP_WIDTH:3 * GROUP_WIDTH]
        g = d_ref[0, rows, 3 * GROUP_WIDTH:4 * GROUP_WIDTH]
        o, st_new = _hgrn_chunk(q, fl, iv, lb, tri, ones_bd, st_ref[...])
        st_ref[...] = st_new
        sq_hi, sq_lo = _split2(o * o)
        ms = (_dot(sq_hi, ones_bd) + _dot(sq_lo, ones_bd)) * (1.0 / HG_DK)
        o = o * lax.rsqrt(ms + RMS_EPS) * gn_ref[...]
        o_ref[0, rows, :] = o * _sigmoid(g)


def _hgrn(d, lb, gn, tri, ones_bd):
    bsz, L, _ = d.shape
    th = 256
    const = lambda shape: pl.BlockSpec(shape, lambda b, t: (0,) * len(shape))
    return pl.pallas_call(
        functools.partial(_hgrn_kernel, th=th),
        out_shape=jax.ShapeDtypeStruct((bsz, L, GROUP_WIDTH), F32),
        grid=(bsz, L // th),
        in_specs=[pl.BlockSpec((1, th, 4 * GROUP_WIDTH), lambda b, t: (b, t, 0)),
                  const((1, GROUP_WIDTH)), const((1, GROUP_WIDTH)),
                  const((CHUNK, CHUNK)), const((GROUP_WIDTH, GROUP_WIDTH))],
        out_specs=pl.BlockSpec((1, th, GROUP_WIDTH), lambda b, t: (b, t, 0)),
        scratch_shapes=[pltpu.VMEM((GROUP_WIDTH, GROUP_WIDTH), F32)],
        compiler_params=_cparams(("arbitrary", "arbitrary")),
        name="hgrn2",
    )(d, lb, gn, tri, ones_bd)


def _merge_kernel(ya_ref, yb_ref, yc_ref, yd_ref, x_ref, mod_ref, mg_ref, wo_ref, g_ref, b_ref, o_ref):
    acc = None
    for gi, y_ref in enumerate((ya_ref, yb_ref, yc_ref, yd_ref)):
        y = y_ref[0]
        yn = y * lax.rsqrt(jnp.mean(y * y, axis=-1, keepdims=True) + RMS_EPS)
        yn = (yn * mg_ref[:, gi * GROUP_WIDTH:(gi + 1) * GROUP_WIDTH]).astype(BF16)
        part = _dot(yn, wo_ref[gi * GROUP_WIDTH:(gi + 1) * GROUP_WIDTH, :])
        acc = part if acc is None else acc + part
    g1 = mod_ref[0, :, 2 * D_MODEL:3 * D_MODEL]
    o_ref[0] = _layer_norm(DN_ALPHA * x_ref[0] + (1.0 + g1) * acc, g_ref[...], b_ref[...])


def _merge(ya, yb, yc, yd, x, mod3, mg, wo, g, b):
    bsz, L, d = x.shape
    tm = 512
    const = lambda shape: pl.BlockSpec(shape, lambda bb, t: (0,) * len(shape))
    row = lambda w: pl.BlockSpec((1, tm, w), lambda bb, t: (bb, t, 0))
    return pl.pallas_call(
        _merge_kernel,
        out_shape=jax.ShapeDtypeStruct((bsz, L, d), F32),
        grid=(bsz, L // tm),
        in_specs=[row(GROUP_WIDTH)] * 4 + [row(d), pl.BlockSpec((1, 1, 6 * d), lambda bb, t: (bb, 0, 0)),
                                          const((1, d)), const((d, d)), const((1, d)), const((1, d))],
        out_specs=row(d),
        compiler_params=_cparams(("arbitrary", "arbitrary")),
        name="merge_out",
    )(ya, yb, yc, yd, x, mod3, mg, wo, g, b)


FF_TILE = 256
FF_HALO = 16


FF_DOWN_GROUP = 4


def _ffn_kernel(x_ref, xh_ref, mod_ref, wup_ref, cw_ref, cb_ref, wdn_ref, g_ref, b_ref, o_ref,
                hx_ref, a_ref, act_ref, *, tm):
    ti = pl.program_id(1)
    sh2 = mod_ref[0, :, 3 * D_MODEL:4 * D_MODEL]
    sc2 = mod_ref[0, :, 4 * D_MODEL:5 * D_MODEL]
    g2 = mod_ref[0, :, 5 * D_MODEL:6 * D_MODEL]
    x = x_ref[0]
    halo = xh_ref[0] * (1.0 + sc2) + sh2
    hx_ref[0:FF_HALO, :] = jnp.where(ti > 0, halo, 0.0).astype(BF16)
    hx_ref[FF_HALO:FF_HALO + tm, :] = (x * (1.0 + sc2) + sh2).astype(BF16)
    nj = D_FF // FF_TILE

    def up(j):
        hx = hx_ref[...]
        a_ref[j % 2, 0] = _dot(hx, wup_ref[:, j * FF_TILE:(j + 1) * FF_TILE])
        a_ref[j % 2, 1] = _dot(hx, wup_ref[:, D_FF + j * FF_TILE:D_FF + (j + 1) * FF_TILE])

    def conv(j, half):
        cols = slice(half * D_FF + j * FF_TILE, half * D_FF + (j + 1) * FF_TILE)
        a = a_ref.at[j % 2, half]
        return (a[pl.ds(FF_HALO - 2, tm), :] * cw_ref[0:1, cols] + a[pl.ds(FF_HALO - 1, tm), :] * cw_ref[1:2, cols]
                + a[pl.ds(FF_HALO, tm), :] * cw_ref[2:3, cols] + cb_ref[:, cols])

    up(0)
    acc = None
    for j in range(nj):
        if j + 1 < nj:
            up(j + 1)
        gate = conv(j, 1)
        act_ref[:, j * FF_TILE:(j + 1) * FF_TILE] = (conv(j, 0) * (gate * _sigmoid(gate))).astype(BF16)
        if (j + 1) % FF_DOWN_GROUP == 0 or j + 1 == nj:
            k0 = (j // FF_DOWN_GROUP) * FF_DOWN_GROUP * FF_TILE
            k1 = (j + 1) * FF_TILE
            part = _dot(act_ref[:, k0:k1], wdn_ref[k0:k1, :])
            acc = part if acc is None else acc + part
    o_ref[0] = _layer_norm(DN_ALPHA * x + (1.0 + g2) * acc, g_ref[...], b_ref[...])


def _ffn(x, mod3, wup, cw, cb, wdn, g, b):
    bsz, L, d = x.shape
    tm = 512
    const = lambda shape: pl.BlockSpec(shape, lambda bb, t: (0,) * len(shape), pipeline_mode=pl.Buffered(1))
    hb = tm // FF_HALO
    return pl.pallas_call(
        functools.partial(_ffn_kernel, tm=tm),
        out_shape=jax.ShapeDtypeStruct((bsz, L, d), F32),
        grid=(bsz, L // tm),
        in_specs=[pl.BlockSpec((1, tm, d), lambda bb, t: (bb, t, 0)),
                  pl.BlockSpec((1, FF_HALO, d), lambda bb, t: (bb, jnp.maximum(t * hb - 1, 0), 0)),
                  pl.BlockSpec((1, 1, 6 * d), lambda bb, t: (bb, 0, 0)),
                  const((d, 2 * D_FF)), const((CONV_W, 2 * D_FF)), const((1, 2 * D_FF)),
                  const((D_FF, d)), const((1, d)), const((1, d))],
        out_specs=pl.BlockSpec((1, tm, d), lambda bb, t: (bb, t, 0)),
        scratch_shapes=[pltpu.VMEM((tm + FF_HALO, d), BF16),
                        pltpu.VMEM((2, 2, tm + FF_HALO, FF_TILE), F32),
                        pltpu.VMEM((tm, D_FF), BF16)],
        compiler_params=_cparams(("arbitrary", "arbitrary")),
        name="conv_ffn",
    )(x, x, mod3, wup, cw, cb, wdn, g, b)


def _rope_tables(L):
    inv = 1.0 / (ROPE_THETA ** (jnp.arange(0, MLA_ROPE_DIM, 2, dtype=F32) / MLA_ROPE_DIM))
    ang = jnp.arange(L, dtype=F32)[:, None] * inv[None, :]
    cos, sin = jnp.cos(ang), jnp.sin(ang)
    cc = jnp.concatenate([cos, cos], axis=1)
    ss = jnp.concatenate([-sin, sin], axis=1)
    scale = MLA_QK_DIM ** -0.5
    zeros32 = jnp.zeros((L, HEAD_PAD - MLA_QK_DIM), F32)
    tq_c = jnp.concatenate([jnp.ones((L, MLA_NOPE_DIM), F32), cc, zeros32], axis=1) * scale
    tq_s = jnp.concatenate([jnp.zeros((L, MLA_NOPE_DIM), F32), ss, zeros32], axis=1) * scale
    tk = jnp.concatenate([cc, ss, jnp.zeros((L, 64), F32)], axis=1)
    return tq_c, tq_s, tk


def _swap_halves(w):
    half = MLA_ROPE_DIM // 2
    return jnp.concatenate([w[..., half:], w[..., :half]], axis=-1)


def _prep_inproj_weights(w_in_l, w_uq_l, w_ukv_l, q_norm_l, kv_norm_l):
    d = w_in_l.shape[0]
    a_u, a_v, cq, ckv, kr, c_u, d_q, d_f, d_i, d_g = jnp.split(
        w_in_l, (256, 512, 704, 832, 864, 1120, 1376, 1632, 1888), axis=1)
    z64 = jnp.zeros((d, 64), F32)
    w_ext = jnp.concatenate([a_u, a_v, c_u, d_q, d_f, d_i, d_g, cq, z64, ckv, kr, _swap_halves(kr), z64],
                            axis=1).astype(BF16)
    wq = w_uq_l.reshape(MLA_Q_RANK, MLA_HEADS, MLA_QK_DIM)
    zq32 = jnp.zeros((MLA_Q_RANK, MLA_HEADS, HEAD_PAD - MLA_QK_DIM), F32)
    zq64 = jnp.zeros((MLA_Q_RANK, MLA_HEADS, MLA_NOPE_DIM), F32)
    wq_a = jnp.concatenate([wq, zq32], axis=2).reshape(MLA_Q_RANK, QK_PAD)
    wq_b = jnp.concatenate([zq64, _swap_halves(wq[:, :, MLA_NOPE_DIM:]), zq32], axis=2).reshape(MLA_Q_RANK, QK_PAD)
    wq_ext = jnp.concatenate([wq_a, wq_b], axis=1)
    wq_ext = jnp.concatenate([wq_ext, jnp.zeros((256 - MLA_Q_RANK, 2 * QK_PAD), F32)], axis=0).astype(BF16)
    qn = jnp.concatenate([q_norm_l, jnp.zeros((256 - MLA_Q_RANK,), F32)]).reshape(1, 256)
    wkv = w_ukv_l.reshape(MLA_KV_RANK, MLA_HEADS, MLA_NOPE_DIM + MLA_V_DIM)
    zk64 = jnp.zeros((MLA_KV_RANK, MLA_HEADS, HEAD_PAD - MLA_NOPE_DIM), F32)
    wk = jnp.concatenate([wkv[:, :, :MLA_NOPE_DIM], zk64], axis=2).reshape(MLA_KV_RANK, QK_PAD)
    wv = wkv[:, :, MLA_NOPE_DIM:].reshape(MLA_KV_RANK, MLA_HEADS * MLA_V_DIM)
    eye = jnp.eye(MLA_ROPE_DIM, dtype=F32)
    place_h = jnp.concatenate([jnp.zeros((MLA_ROPE_DIM, MLA_NOPE_DIM), F32), eye,
                               jnp.zeros((MLA_ROPE_DIM, HEAD_PAD - MLA_QK_DIM), F32)], axis=1)
    place = jnp.concatenate([place_h] * MLA_HEADS, axis=1)
    place = jnp.concatenate([place, jnp.zeros((MLA_ROPE_DIM, MLA_HEADS * MLA_V_DIM), F32)], axis=1)
    top = jnp.concatenate([wk, wv], axis=1)
    wkv_ext = jnp.concatenate([top, place, place, jnp.zeros((64, QK_PAD + 256), F32)], axis=0).astype(BF16)
    kvn = kv_norm_l.reshape(1, MLA_KV_RANK)
    return w_ext, wq_ext, wkv_ext, qn, kvn


def _prep_gm(gm_ws_l, gm_bs_l):
    ck = jnp.arange(GM_BLOCK) // CHUNK
    mask = ck[None, :] <= ck[:, None]
    w = jnp.where(mask[None], gm_ws_l, 0).astype(BF16)
    bias = jnp.repeat(gm_bs_l.T, GROUP_WIDTH // GM_HEADS, axis=1)
    return w, bias


def _prep_s5(a_re, a_im, b_re, b_im, c_re, c_im, log_step):
    dt = jnp.exp(log_step)[:, None]
    mag = jnp.exp(dt * a_re)
    abar_re, abar_im = mag * jnp.cos(dt * a_im), mag * jnp.sin(dt * a_im)
    den = a_re * a_re + a_im * a_im
    nr, ni = abar_re - 1.0, abar_im
    zoh_re = (nr * a_re + ni * a_im) / den
    zoh_im = (ni * a_re - nr * a_im) / den
    bbar_re = zoh_re[..., None] * b_re - zoh_im[..., None] * b_im
    bbar_im = zoh_re[..., None] * b_im + zoh_im[..., None] * b_re
    eye_g = jnp.eye(S5_GROUPS, dtype=F32)
    bm_re = jnp.einsum('gpn,gh->gnhp', bbar_re, eye_g).reshape(GROUP_WIDTH, S5_NSTATE)
    bm_im = jnp.einsum('gpn,gh->gnhp', bbar_im, eye_g).reshape(GROUP_WIDTH, S5_NSTATE)
    bmat = jnp.concatenate([bm_re, bm_im], axis=1).astype(BF16)
    cm_re = jnp.einsum('gnp,gh->gphn', c_re, eye_g).reshape(S5_NSTATE, GROUP_WIDTH)
    cm_im = jnp.einsum('gnp,gh->gphn', c_im, eye_g).reshape(S5_NSTATE, GROUP_WIDTH)
    cmat = jnp.concatenate([cm_re, -cm_im], axis=0).astype(BF16)
    return bmat, abar_re.reshape(1, S5_NSTATE), abar_im.reshape(1, S5_NSTATE), cmat


def kernel(x, c, ln_in_g, ln_in_b, hg_lb_logits, w_ada, b_ada, w_in, gm_ln_g, gm_ln_b, gm_ws, gm_bs, mla_q_norm, mla_kv_norm, mla_w_uq, mla_w_ukv, s5_a_re, s5_a_im, s5_b_re, s5_b_im, s5_c_re, s5_c_im, s5_d, s5_log_step, s5_w_glu, s5_b_glu, hg_norm_g, merge_g, w_out, ln1_g, ln1_b, w_up, conv_w, conv_b, w_down, ln2_g, ln2_b):
    bsz, L, d = x.shape
    nl = w_in.shape[0]
    mod = _ada(c, w_ada, b_ada)
    xs = _ln_in(x.reshape(bsz * L, d), ln_in_g, ln_in_b).reshape(bsz, L, d)
    lb_cum = jnp.cumsum(jax.nn.softmax(hg_lb_logits.astype(F32), axis=0), axis=0)
    tq_c, tq_s, tk = _rope_tables(L)
    tri = jnp.tril(jnp.ones((CHUNK, CHUNK), F32)).astype(BF16)
    hid = jnp.arange(GROUP_WIDTH) // HG_DK
    ones_bd = (hid[:, None] == hid[None, :]).astype(BF16)
    row = lambda v: v.reshape(1, -1)

    for l in range(nl):
        mod3 = mod[l].reshape(bsz, 1, 6 * d)
        w_ext, wq_ext, wkv_ext, qn, kvn = _prep_inproj_weights(w_in[l], mla_w_uq[l], mla_w_ukv[l],
                                                              mla_q_norm[l], mla_kv_norm[l])
        gm_w, gm_bias = _prep_gm(gm_ws[l], gm_bs[l])
        y_a, c_u, dd, q, k, v = _inproj(xs, mod3, w_ext, row(gm_ln_g[l]), row(gm_ln_b[l]), gm_w, gm_bias,
                                        qn, kvn, wq_ext, wkv_ext, tq_c, tq_s, tk)
        y_b = _attention(q, k, v)
        bmat, a_re, a_im, cmat = _prep_s5(s5_a_re[l], s5_a_im[l], s5_b_re[l], s5_b_im[l], s5_c_re[l],
                                          s5_c_im[l], s5_log_step[l])
        y_c = _s5(jnp.swapaxes(c_u, 0, 1), bmat, a_re, a_im, cmat, row(s5_d[l]), s5_w_glu[l].astype(BF16),
                  row(s5_b_glu[l]))
        y_c = jnp.swapaxes(y_c, 0, 1)
        y_d = _hgrn(dd, row(lb_cum[l] - lb_cum[0]), row(hg_norm_g[l]), tri, ones_bd)
        xs = _merge(y_a, y_b, y_c, y_d, xs, mod3, row(merge_g[l]), w_out[l].astype(BF16), row(ln1_g[l]),
                    row(ln1_b[l]))
        xs = _ffn(xs, mod3, w_up[l].astype(BF16), conv_w[l], row(conv_b[l]), w_down[l].astype(BF16),
                  row(ln2_g[l]), row(ln2_b[l]))
    return xs
```

```python
import functools
import math

import jax
import jax.numpy as jnp
from jax import lax
from jax.experimental import pallas as pl
from jax.experimental.pallas import tpu as pltpu

F32 = jnp.float32
BF16 = jnp.bfloat16

D_MODEL = 1024
DEPTH = 2
CHUNK = 64
GROUP_WIDTH = 256
N_MIXERS = 4
GM_HEADS = 4
GM_BLOCK = 128
MLA_HEADS = 4
MLA_NOPE_DIM = 64
MLA_ROPE_DIM = 32
MLA_QK_DIM = MLA_NOPE_DIM + MLA_ROPE_DIM
MLA_V_DIM = 64
MLA_Q_RANK = 192
MLA_KV_RANK = 128
ROPE_THETA = 10000.0
S5_GROUP = 16
S5_GROUPS = 16
S5_STATE = 64
HG_HEADS = 4
HG_DK = 64
D_FF = 2816
CONV_W = 3
DN_ALPHA = (2 * DEPTH) ** 0.25
LN_EPS = 1e-5
RMS_EPS = 1e-6

LANES = 128
HEAD_PAD = 128
QK_PAD = MLA_HEADS * HEAD_PAD
S5_NSTATE = S5_GROUPS * S5_STATE
HG_SUB = 8

COL_AU, COL_AV, COL_CU, COL_DQ = 0, 256, 512, 768
COL_CQ, COL_CKV, COL_KR = 1792, 2048, 2176
IN_EXT = 2304

VMEM_LIMIT = 56 * 1024 * 1024


def _sigmoid(x):
    return 1.0 / (1.0 + jnp.exp(-x))


def _gelu(x):
    return 0.5 * x * (1.0 + jnp.tanh(math.sqrt(2.0 / math.pi) * (x + 0.044715 * (x * x * x))))


def _split2(x):
    hi = x.astype(BF16)
    lo = (x - hi.astype(F32)).astype(BF16)
    return hi, lo


def _split3(x):
    hi = x.astype(BF16)
    r = x - hi.astype(F32)
    mid = r.astype(BF16)
    lo = (r - mid.astype(F32)).astype(BF16)
    return hi, mid, lo


def _dot(a, b):
    return jnp.dot(a, b, preferred_element_type=F32)


def _dot_nt(a, b):
    return lax.dot_general(a, b, (((1,), (1,)), ((), ())), preferred_element_type=F32)


def _dot_tn(a, b):
    return lax.dot_general(a, b, (((0,), (0,)), ((), ())), preferred_element_type=F32)


def _layer_norm(x, g, b):
    mu = jnp.mean(x, axis=-1, keepdims=True)
    xc = x - mu
    var = jnp.mean(xc * xc, axis=-1, keepdims=True)
    return xc * lax.rsqrt(var + LN_EPS) * g + b


def _cparams(sem, vmem=VMEM_LIMIT):
    return pltpu.CompilerParams(dimension_semantics=sem, vmem_limit_bytes=vmem)


def _ada_kernel(c_ref, w_ref, b_ref, o_ref):
    c = c_ref[...]
    chi, clo = _split2(c * _sigmoid(c))
    whi, wlo = _split2(w_ref[0])
    acc = _dot(chi, whi) + _dot(clo, whi) + _dot(chi, wlo)
    o_ref[0] = acc + b_ref[0]


def _ada(c, w_ada, b_ada):
    nl, d, n = w_ada.shape
    tn = 1536
    bsz = c.shape[0]
    return pl.pallas_call(
        _ada_kernel,
        out_shape=jax.ShapeDtypeStruct((nl, bsz, n), F32),
        grid=(nl, n // tn),
        in_specs=[pl.BlockSpec((bsz, d), lambda l, j: (0, 0)),
                  pl.BlockSpec((1, d, tn), lambda l, j: (l, 0, j)),
                  pl.BlockSpec((1, 1, tn), lambda l, j: (l, 0, j))],
        out_specs=pl.BlockSpec((1, bsz, tn), lambda l, j: (l, 0, j)),
        compiler_params=_cparams(("arbitrary", "arbitrary")),
        name="ada_mod",
    )(c, w_ada, b_ada.reshape(nl, 1, n))


def _ln_in_kernel(x_ref, g_ref, b_ref, o_ref):
    o_ref[...] = _layer_norm(x_ref[...], g_ref[...], b_ref[...])


def _ln_in(x2, g, b):
    m, d = x2.shape
    tm = 1024
    return pl.pallas_call(
        _ln_in_kernel,
        out_shape=jax.ShapeDtypeStruct((m, d), F32),
        grid=(m // tm,),
        in_specs=[pl.BlockSpec((tm, d), lambda i: (i, 0)),
                  pl.BlockSpec((1, d), lambda i: (0, 0)),
                  pl.BlockSpec((1, d), lambda i: (0, 0))],
        out_specs=pl.BlockSpec((tm, d), lambda i: (i, 0)),
        compiler_params=_cparams(("arbitrary",)),
        name="ln_in",
    )(x2, g.reshape(1, d), b.reshape(1, d))


def _inproj_kernel(x_ref, mod_ref, w_ref, gmg_ref, gmb_ref, gmw_ref, gmbias_ref,
                   qn_ref, kvn_ref, wq_ref, wkv_ref, tq_c_ref, tq_s_ref, tk_ref,
                   ya_ref, cu_ref, d_ref, q_ref, k_ref, v_ref, *, tm):
    x = x_ref[0]
    sh1 = mod_ref[0, :, 0:D_MODEL]
    sc1 = mod_ref[0, :, D_MODEL:2 * D_MODEL]
    h = (x * (1.0 + sc1) + sh1).astype(BF16)
    z = _dot(h, w_ref[...])

    cu_ref[...] = z[:, COL_CU:COL_CU + GROUP_WIDTH]
    d_ref[0] = z[:, COL_DQ:COL_DQ + 4 * GROUP_WIDTH]

    lane_head = lax.broadcasted_iota(jnp.int32, (GM_BLOCK, GROUP_WIDTH), 1) // (GROUP_WIDTH // GM_HEADS)
    for r in range(tm // GM_BLOCK):
        rows = slice(r * GM_BLOCK, (r + 1) * GM_BLOCK)
        u = _gelu(z[rows, COL_AU:COL_AU + GROUP_WIDTH])
        v = _gelu(z[rows, COL_AV:COL_AV + GROUP_WIDTH])
        vn = _layer_norm(v, gmg_ref[...], gmb_ref[...]).astype(BF16)
        zz = gmbias_ref[...]
        for hh in range(GM_HEADS):
            zz = zz + jnp.where(lane_head == hh, _dot(gmw_ref[hh], vn), 0.0)
        ya_ref[0, rows, :] = u * zz

    cq = z[:, COL_CQ:COL_CQ + 256]
    cqn = cq * lax.rsqrt(jnp.sum(cq * cq, axis=-1, keepdims=True) * (1.0 / MLA_Q_RANK) + RMS_EPS)
    cqn = (cqn * qn_ref[...]).astype(BF16)
    qq = _dot(cqn, wq_ref[...])
    tq_c = jnp.concatenate([tq_c_ref[...]] * MLA_HEADS, axis=1)
    tq_s = jnp.concatenate([tq_s_ref[...]] * MLA_HEADS, axis=1)
    q_ref[0] = (qq[:, :QK_PAD] * tq_c + qq[:, QK_PAD:] * tq_s).astype(BF16)

    ckv = z[:, COL_CKV:COL_CKV + MLA_KV_RANK]
    ckvn = ckv * lax.rsqrt(jnp.mean(ckv * ckv, axis=-1, keepdims=True) + RMS_EPS)
    ckvn = (ckvn * kvn_ref[...]).astype(BF16)
    krt = (z[:, COL_KR:COL_KR + 128] * tk_ref[...]).astype(BF16)
    kv = _dot(jnp.concatenate([ckvn, krt], axis=1), wkv_ref[...])
    k_ref[0] = kv[:, :QK_PAD].astype(BF16)
    for hh in range(MLA_HEADS):
        v_ref[0, hh] = kv[:, QK_PAD + hh * MLA_V_DIM:QK_PAD + (hh + 1) * MLA_V_DIM].astype(BF16)


def _inproj(x, mod3, w_ext, gm_g, gm_b, gm_w, gm_bias, qn, kvn, wq_ext, wkv_ext, tq_c, tq_s, tk):
    bsz, L, d = x.shape
    tm = 512
    const = lambda shape: pl.BlockSpec(shape, lambda b, t: (0,) * len(shape))
    out_shape = (
        jax.ShapeDtypeStruct((bsz, L, GROUP_WIDTH), F32),
        jax.ShapeDtypeStruct((L, bsz * GROUP_WIDTH), F32),
        jax.ShapeDtypeStruct((bsz, L, 4 * GROUP_WIDTH), F32),
        jax.ShapeDtypeStruct((bsz, L, QK_PAD), BF16),
        jax.ShapeDtypeStruct((bsz, L, QK_PAD), BF16),
        jax.ShapeDtypeStruct((bsz, MLA_HEADS, L, MLA_V_DIM), BF16),
    )
    row = lambda w: pl.BlockSpec((1, tm, w), lambda b, t: (b, t, 0))
    return pl.pallas_call(
        functools.partial(_inproj_kernel, tm=tm),
        out_shape=out_shape,
        grid=(bsz, L // tm),
        in_specs=[row(d),
                  pl.BlockSpec((1, 1, 6 * d), lambda b, t: (b, 0, 0)),
                  const((d, IN_EXT)),
                  const((1, GROUP_WIDTH)), const((1, GROUP_WIDTH)),
                  const((GM_HEADS, GM_BLOCK, GM_BLOCK)), const((GM_BLOCK, GROUP_WIDTH)),
                  const((1, 256)), const((1, MLA_KV_RANK)),
                  const((256, 2 * QK_PAD)), const((256, QK_PAD + 256)),
                  pl.BlockSpec((tm, HEAD_PAD), lambda b, t: (t, 0)),
                  pl.BlockSpec((tm, HEAD_PAD), lambda b, t: (t, 0)),
                  pl.BlockSpec((tm, 128), lambda b, t: (t, 0))],
        out_specs=(row(GROUP_WIDTH), pl.BlockSpec((tm, GROUP_WIDTH), lambda b, t: (t, b)),
                   row(4 * GROUP_WIDTH), row(QK_PAD), row(QK_PAD),
                   pl.BlockSpec((1, MLA_HEADS, tm, MLA_V_DIM), lambda b, t: (b, 0, t, 0))),
        compiler_params=_cparams(("arbitrary", "arbitrary")),
        name="inproj",
    )(x, mod3, w_ext, gm_g, gm_b, gm_w, gm_bias, qn, kvn, wq_ext, wkv_ext, tq_c, tq_s, tk)


ATT_TQ = 256
ATT_HEADS_PER_STEP = 2


def _attn_kernel(q_ref, k_ref, v_ref, o_ref):
    L = q_ref.shape[1]
    tq = ATT_TQ
    q_chunk = lax.broadcasted_iota(jnp.int32, (tq, tq), 0) // CHUNK
    k_chunk = lax.broadcasted_iota(jnp.int32, (tq, tq), 1) // CHUNK
    allowed = k_chunk <= q_chunk
    for qi in range(L // tq):
        lk = (qi + 1) * tq
        outs = []
        for hh in range(ATT_HEADS_PER_STEP):
            lanes = slice(hh * HEAD_PAD, (hh + 1) * HEAD_PAD)
            qh = q_ref[0, qi * tq:(qi + 1) * tq, lanes]
            s = _dot_nt(qh, k_ref[0, 0:lk, lanes])
            s_diag = jnp.where(allowed, s[:, lk - tq:], -1e30)
            s = s_diag if qi == 0 else jnp.concatenate([s[:, :lk - tq], s_diag], axis=1)
            m = jnp.max(s, axis=-1, keepdims=True)
            p = jnp.exp(s - m)
            l = jnp.sum(p, axis=-1, keepdims=True)
            outs.append(_dot(p.astype(BF16), v_ref[0, hh, 0:lk, :]) / l)
        o_ref[0, qi * tq:(qi + 1) * tq, :] = jnp.concatenate(outs, axis=1)


def _attention(q, k, v):
    bsz, L, _ = q.shape
    hp = ATT_HEADS_PER_STEP
    return pl.pallas_call(
        _attn_kernel,
        out_shape=jax.ShapeDtypeStruct((bsz, L, GROUP_WIDTH), F32),
        grid=(bsz, MLA_HEADS // hp),
        in_specs=[pl.BlockSpec((1, L, hp * HEAD_PAD), lambda b, g: (b, 0, g)),
                  pl.BlockSpec((1, L, hp * HEAD_PAD), lambda b, g: (b, 0, g)),
                  pl.BlockSpec((1, hp, L, MLA_V_DIM), lambda b, g: (b, g, 0, 0))],
        out_specs=pl.BlockSpec((1, L, hp * MLA_V_DIM), lambda b, g: (b, 0, g)),
        compiler_params=_cparams(("arbitrary", "arbitrary")),
        name="mla_attention",
    )(q, k, v)


def _s5_kernel(u_ref, bmat_ref, are_ref, aim_ref, cmat_ref, dvec_ref, wglu_ref, bglu_ref,
               o_ref, h_ref, us_ref, bu_ref, hs_ref, os_ref, *, tc, nb):
    @pl.when(pl.program_id(0) == 0)
    def _():
        h_ref[...] = jnp.zeros(h_ref.shape, F32)

    nslab = GROUP_WIDTH // LANES
    for bb in range(nb):
        for s in range(nslab):
            lanes = slice(bb * GROUP_WIDTH + s * LANES, bb * GROUP_WIDTH + (s + 1) * LANES)
            us_ref[s, pl.ds(bb, tc, stride=nb), :] = u_ref[:, lanes]
    u = jnp.concatenate([us_ref[s] for s in range(nslab)], axis=1)
    bu_ref[...] = _dot(u.astype(BF16), bmat_ref[...])
    a_re = jnp.broadcast_to(are_ref[...], (nb, S5_NSTATE))
    a_im = jnp.broadcast_to(aim_ref[...], (nb, S5_NSTATE))
    re = slice(0, S5_NSTATE)
    im = slice(S5_NSTATE, 2 * S5_NSTATE)

    def step(i, carry):
        h_re, h_im = carry
        r0 = pl.ds(pl.multiple_of(i * 2 * nb, 2 * nb), nb)
        r1 = pl.ds(pl.multiple_of(i * 2 * nb, 2 * nb) + nb, nb)
        m_re = a_re * h_re - a_im * h_im + bu_ref[r0, re]
        m_im = a_re * h_im + a_im * h_re + bu_ref[r0, im]
        n_re = a_re * m_re - a_im * m_im + bu_ref[r1, re]
        n_im = a_re * m_im + a_im * m_re + bu_ref[r1, im]
        rr = pl.ds(pl.multiple_of(i * 2 * nb, 2 * nb), 2 * nb)
        hs_ref[rr, re] = jnp.concatenate([m_re, n_re], axis=0).astype(BF16)
        hs_ref[rr, im] = jnp.concatenate([m_im, n_im], axis=0).astype(BF16)
        return n_re, n_im

    h_re, h_im = lax.fori_loop(0, tc // 2, step, (h_ref[:, re], h_ref[:, im]))
    h_ref[:, re] = h_re
    h_ref[:, im] = h_im

    y = _dot(hs_ref[...], cmat_ref[...]) + dvec_ref[...] * u
    z = _gelu(y)
    gate = _sigmoid(_dot(z.astype(BF16), wglu_ref[...]) + bglu_ref[...])
    res = z * gate
    for s in range(nslab):
        os_ref[s] = res[:, s * LANES:(s + 1) * LANES]
    for bb in range(nb):
        for s in range(nslab):
            lanes = slice(bb * GROUP_WIDTH + s * LANES, bb * GROUP_WIDTH + (s + 1) * LANES)
            o_ref[:, lanes] = os_ref[s, pl.ds(bb, tc, stride=nb), :]


def _s5(u_t, nb, bmat, a_re, a_im, cmat, dvec, wglu, bglu):
    L = u_t.shape[0]
    gw = GROUP_WIDTH
    tc = 128
    const = lambda shape: pl.BlockSpec(shape, lambda t: (0,) * len(shape))
    return pl.pallas_call(
        functools.partial(_s5_kernel, tc=tc, nb=nb),
        out_shape=jax.ShapeDtypeStruct((L, nb * gw), F32),
        grid=(L // tc,),
        in_specs=[pl.BlockSpec((tc, nb * gw), lambda t: (t, 0)),
                  const((gw, 2 * S5_NSTATE)), const((1, S5_NSTATE)), const((1, S5_NSTATE)),
                  const((2 * S5_NSTATE, gw)), const((1, gw)), const((gw, gw)), const((1, gw))],
        out_specs=pl.BlockSpec((tc, nb * gw), lambda t: (t, 0)),
        scratch_shapes=[pltpu.VMEM((nb, 2 * S5_NSTATE), F32),
                        pltpu.VMEM((gw // LANES, tc * nb, LANES), F32),
                        pltpu.VMEM((tc * nb, 2 * S5_NSTATE), F32),
                        pltpu.VMEM((tc * nb, 2 * S5_NSTATE), BF16),
                        pltpu.VMEM((gw // LANES, tc * nb, LANES), F32)],
        compiler_params=_cparams(("arbitrary",)),
        name="s5_scan",
    )(u_t, bmat, a_re, a_im, cmat, dvec, wglu, bglu)


LOG2E = 1.4426950408889634
HG_BATCH = 2


def _hgrn_chunk(q, fl, iv, lb, tri, ones_bd, st):
    nsub = CHUNK // HG_SUB
    f = lb + (1.0 - lb) * _sigmoid(fl)
    logf = jnp.log(f)
    kk = 1.0 - f
    hi, mid, lo = _split3(logf)
    b2 = (_dot(tri, hi) + _dot(tri, mid) + _dot(tri, lo)) * LOG2E
    lane_head = lax.broadcasted_iota(jnp.int32, (HG_SUB, GROUP_WIDTH), 1) // HG_DK
    row_id = lax.broadcasted_iota(jnp.int32, (HG_SUB, GROUP_WIDTH), 0)
    v_bf = iv.astype(BF16)

    terms = []
    for si in range(nsub):
        r0 = si * HG_SUB
        q_i, k_i, b_i = (a[r0:r0 + HG_SUB] for a in (q, kk, b2))
        for j in range(HG_SUB):
            e = jnp.exp2(jnp.where(row_id >= j, b_i - b_i[j:j + 1], -1e30))
            terms.append(q_i * k_i[j:j + 1] * e)
    p = _dot(jnp.concatenate(terms, axis=0).astype(BF16), ones_bd)
    o_parts = []
    for si in range(nsub):
        r0 = si * HG_SUB
        base = si * HG_SUB * HG_SUB
        o_i = p[base:base + HG_SUB] * iv[r0:r0 + 1]
        for j in range(1, HG_SUB):
            o_i = o_i + p[base + j * HG_SUB:base + (j + 1) * HG_SUB] * iv[r0 + j:r0 + j + 1]
        o_parts.append(o_i)

    atts = []
    for si in range(1, nsub):
        r0 = si * HG_SUB
        b_edge = b2[r0 - 1:r0]
        q_t = q[r0:r0 + HG_SUB] * jnp.exp2(b2[r0:r0 + HG_SUB] - b_edge)
        k_t = jnp.concatenate([kk[0:r0] * jnp.exp2(b_edge - b2[0:r0]),
                               jnp.zeros((CHUNK - r0, GROUP_WIDTH), F32)], axis=0).astype(BF16)
        q_exp = jnp.concatenate([jnp.where(lane_head == hh, q_t, 0.0) for hh in range(HG_HEADS)],
                                axis=0).astype(BF16)
        atts.append(_dot_nt(q_exp, k_t))
    o2 = _dot(jnp.concatenate(atts, axis=0).astype(BF16), v_bf)
    for si in range(1, nsub):
        base = (si - 1) * HG_HEADS * HG_SUB
        for hh in range(HG_HEADS):
            o_parts[si] = o_parts[si] + jnp.where(lane_head == hh,
                                                  o2[base + hh * HG_SUB:base + (hh + 1) * HG_SUB], 0.0)
    o_intra = jnp.concatenate(o_parts, axis=0)

    o_inter = _dot_nt((q * jnp.exp2(b2)).astype(BF16), st.astype(BF16))
    b_last = b2[CHUNK - 1:CHUNK]
    k_hat = (kk * jnp.exp2(b_last - b2)).astype(BF16)
    upd = _dot_tn(v_bf, k_hat)
    bd = (lax.broadcasted_iota(jnp.int32, (GROUP_WIDTH, GROUP_WIDTH), 0) // HG_DK
          == lax.broadcasted_iota(jnp.int32, (GROUP_WIDTH, GROUP_WIDTH), 1) // HG_DK)
    st_new = st * jnp.exp2(b_last) + jnp.where(bd, upd, 0.0)
    return o_intra + o_inter, st_new


def _hgrn_kernel(d_ref, lb_ref, gn_ref, tri_ref, ones_ref, o_ref, st_ref, *, th):
    @pl.when(pl.program_id(1) == 0)
    def _():
        st_ref[...] = jnp.zeros(st_ref.shape, F32)

    lb = lb_ref[...]
    tri = tri_ref[...]
    ones_bd = ones_ref[...]
    for c in range(th // CHUNK):
        rows = slice(c * CHUNK, (c + 1) * CHUNK)
        for bb in range(d_ref.shape[0]):
            q = d_ref[bb, rows, 0:GROUP_WIDTH]
            fl = d_ref[bb, rows, GROUP_WIDTH:2 * GROUP_WIDTH]
            iv = d_ref[bb, rows, 2 * GROUP_WIDTH:3 * GROUP_WIDTH]
            g = d_ref[bb, rows, 3 * GROUP_WIDTH:4 * GROUP_WIDTH]
            o, st_new = _hgrn_chunk(q, fl, iv, lb, tri, ones_bd, st_ref[bb])
            st_ref[bb] = st_new
            sq_hi, sq_lo = _split2(o * o)
            ms = (_dot(sq_hi, ones_bd) + _dot(sq_lo, ones_bd)) * (1.0 / HG_DK)
            o = o * lax.rsqrt(ms + RMS_EPS) * gn_ref[...]
            o_ref[bb, rows, :] = o * _sigmoid(g)


def _hgrn(d, lb, gn, tri, ones_bd):
    bsz, L, _ = d.shape
    th = 256
    nbb = HG_BATCH if bsz % HG_BATCH == 0 else 1
    const = lambda shape: pl.BlockSpec(shape, lambda b, t: (0,) * len(shape))
    return pl.pallas_call(
        functools.partial(_hgrn_kernel, th=th),
        out_shape=jax.ShapeDtypeStruct((bsz, L, GROUP_WIDTH), F32),
        grid=(bsz // nbb, L // th),
        in_specs=[pl.BlockSpec((nbb, th, 4 * GROUP_WIDTH), lambda b, t: (b, t, 0)),
                  const((1, GROUP_WIDTH)), const((1, GROUP_WIDTH)),
                  const((CHUNK, CHUNK)), const((GROUP_WIDTH, GROUP_WIDTH))],
        out_specs=pl.BlockSpec((nbb, th, GROUP_WIDTH), lambda b, t: (b, t, 0)),
        scratch_shapes=[pltpu.VMEM((nbb, GROUP_WIDTH, GROUP_WIDTH), F32)],
        compiler_params=_cparams(("arbitrary", "arbitrary")),
        name="hgrn2",
    )(d, lb, gn, tri, ones_bd)


def _merge_kernel(ya_ref, yb_ref, yc_ref, yd_ref, x_ref, mod_ref, mg_ref, wo_ref, g_ref, b_ref, o_ref):
    acc = None
    for gi, y_ref in enumerate((ya_ref, yb_ref, yc_ref, yd_ref)):
        y = y_ref[...] if gi == 2 else y_ref[0]
        yn = y * lax.rsqrt(jnp.mean(y * y, axis=-1, keepdims=True) + RMS_EPS)
        yn = (yn * mg_ref[:, gi * GROUP_WIDTH:(gi + 1) * GROUP_WIDTH]).astype(BF16)
        part = _dot(yn, wo_ref[gi * GROUP_WIDTH:(gi + 1) * GROUP_WIDTH, :])
        acc = part if acc is None else acc + part
    g1 = mod_ref[0, :, 2 * D_MODEL:3 * D_MODEL]
    o_ref[0] = _layer_norm(DN_ALPHA * x_ref[0] + (1.0 + g1) * acc, g_ref[...], b_ref[...])


def _merge(ya, yb, yc, yd, x, mod3, mg, wo, g, b):
    bsz, L, d = x.shape
    tm = 512
    const = lambda shape: pl.BlockSpec(shape, lambda bb, t: (0,) * len(shape))
    row = lambda w: pl.BlockSpec((1, tm, w), lambda bb, t: (bb, t, 0))
    return pl.pallas_call(
        _merge_kernel,
        out_shape=jax.ShapeDtypeStruct((bsz, L, d), F32),
        grid=(bsz, L // tm),
        in_specs=[row(GROUP_WIDTH), row(GROUP_WIDTH), pl.BlockSpec((tm, GROUP_WIDTH), lambda bb, t: (t, bb)),
                  row(GROUP_WIDTH), row(d), pl.BlockSpec((1, 1, 6 * d), lambda bb, t: (bb, 0, 0)),
                  const((1, d)), const((d, d)), const((1, d)), const((1, d))],
        out_specs=row(d),
        compiler_params=_cparams(("arbitrary", "arbitrary")),
        name="merge_out",
    )(ya, yb, yc, yd, x, mod3, mg, wo, g, b)


FF_TILE = 256
FF_HALO = 16


FF_DOWN_GROUP = 4


FF_SEGS = 8


def _ffn_kernel(x_ref, xh_ref, mod_ref, wup_ref, cw_ref, cb_ref, wdn_ref, g_ref, b_ref, o_ref,
                hp_ref, hx_ref, a_ref, act_ref, ap_ref, *, tm):
    ti = pl.program_id(1)
    seg = tm // FF_SEGS
    nslab = D_MODEL // LANES
    sh2 = mod_ref[0, :, 3 * D_MODEL:4 * D_MODEL]
    sc2 = mod_ref[0, :, 4 * D_MODEL:5 * D_MODEL]
    g2 = mod_ref[0, :, 5 * D_MODEL:6 * D_MODEL]
    x = x_ref[0]
    h2 = x * (1.0 + sc2) + sh2
    halo = jnp.where(ti > 0, xh_ref[0] * (1.0 + sc2) + sh2, 0.0)
    for s in range(nslab):
        lanes = slice(s * LANES, (s + 1) * LANES)
        hp_ref[s, 0:FF_HALO, :] = halo[:, lanes]
        for p in range(FF_SEGS):
            hp_ref[s, pl.ds(FF_HALO + p, seg, stride=FF_SEGS), :] = h2[p * seg:(p + 1) * seg, lanes]
    hx_ref[...] = jnp.concatenate([hp_ref[s] for s in range(nslab)], axis=1).astype(BF16)
    nj = D_FF // FF_TILE
    sub0 = lax.broadcasted_iota(jnp.int32, (FF_SEGS, FF_TILE), 0) == 0

    def up(j):
        hx = hx_ref[...]
        for half in range(2):
            a = a_ref.at[j % 2, half]
            cols = slice(half * D_FF + j * FF_TILE, half * D_FF + (j + 1) * FF_TILE)
            a[...] = _dot(hx, wup_ref[:, cols])
            prev1 = jnp.where(sub0, a[FF_HALO - 1:FF_HALO, :], pltpu.roll(a[FF_HALO + tm - 8:FF_HALO + tm, :], 1, 0))
            prev2 = jnp.where(sub0, a[FF_HALO - 2:FF_HALO - 1, :],
                              pltpu.roll(a[FF_HALO + tm - 16:FF_HALO + tm - 8, :], 1, 0))
            a[8:16, :] = prev1
            a[0:8, :] = prev2

    def conv(j, half):
        cols = slice(half * D_FF + j * FF_TILE, half * D_FF + (j + 1) * FF_TILE)
        a = a_ref.at[j % 2, half]
        return (a[0:tm, :] * cw_ref[0:1, cols] + a[8:tm + 8, :] * cw_ref[1:2, cols]
                + a[16:tm + 16, :] * cw_ref[2:3, cols] + cb_ref[:, cols])

    up(0)
    acc = None
    for j in range(nj):
        if j + 1 < nj:
            up(j + 1)
        gate = conv(j, 1)
        act_ref[:, j * FF_TILE:(j + 1) * FF_TILE] = (conv(j, 0) * (gate * _sigmoid(gate))).astype(BF16)
        if (j + 1) % FF_DOWN_GROUP == 0 or j + 1 == nj:
            k0 = (j // FF_DOWN_GROUP) * FF_DOWN_GROUP * FF_TILE
            k1 = (j + 1) * FF_TILE
            part = _dot(act_ref[:, k0:k1], wdn_ref[k0:k1, :])
            acc = part if acc is None else acc + part
    for s in range(nslab):
        ap_ref[s] = acc[:, s * LANES:(s + 1) * LANES]
    acc_nat = jnp.concatenate(
        [jnp.concatenate([ap_ref[s, pl.ds(p, seg, stride=FF_SEGS), :] for p in range(FF_SEGS)], axis=0)
         for s in range(nslab)], axis=1)
    o_ref[0] = _layer_norm(DN_ALPHA * x + (1.0 + g2) * acc_nat, g_ref[...], b_ref[...])


def _ffn(x, mod3, wup, cw, cb, wdn, g, b):
    bsz, L, d = x.shape
    tm = 512
    const = lambda shape: pl.BlockSpec(shape, lambda bb, t: (0,) * len(shape), pipeline_mode=pl.Buffered(1))
    hb = tm // FF_HALO
    return pl.pallas_call(
        functools.partial(_ffn_kernel, tm=tm),
        out_shape=jax.ShapeDtypeStruct((bsz, L, d), F32),
        grid=(bsz, L // tm),
        in_specs=[pl.BlockSpec((1, tm, d), lambda bb, t: (bb, t, 0)),
                  pl.BlockSpec((1, FF_HALO, d), lambda bb, t: (bb, jnp.maximum(t * hb - 1, 0), 0)),
                  pl.BlockSpec((1, 1, 6 * d), lambda bb, t: (bb, 0, 0)),
                  const((d, 2 * D_FF)), const((CONV_W, 2 * D_FF)), const((1, 2 * D_FF)),
                  const((D_FF, d)), const((1, d)), const((1, d))],
        out_specs=pl.BlockSpec((1, tm, d), lambda bb, t: (bb, t, 0)),
        scratch_shapes=[pltpu.VMEM((d // LANES, tm + FF_HALO, LANES), F32),
                        pltpu.VMEM((tm + FF_HALO, d), BF16),
                        pltpu.VMEM((2, 2, tm + FF_HALO, FF_TILE), F32),
                        pltpu.VMEM((tm, D_FF), BF16),
                        pltpu.VMEM((d // LANES, tm, LANES), F32)],
        compiler_params=_cparams(("arbitrary", "arbitrary")),
        name="conv_ffn",
    )(x, x, mod3, wup, cw, cb, wdn, g, b)


def _rope_tables(L):
    inv = 1.0 / (ROPE_THETA ** (jnp.arange(0, MLA_ROPE_DIM, 2, dtype=F32) / MLA_ROPE_DIM))
    ang = jnp.arange(L, dtype=F32)[:, None] * inv[None, :]
    cos, sin = jnp.cos(ang), jnp.sin(ang)
    cc = jnp.concatenate([cos, cos], axis=1)
    ss = jnp.concatenate([-sin, sin], axis=1)
    scale = MLA_QK_DIM ** -0.5
    zeros32 = jnp.zeros((L, HEAD_PAD - MLA_QK_DIM), F32)
    tq_c = jnp.concatenate([jnp.ones((L, MLA_NOPE_DIM), F32), cc, zeros32], axis=1) * scale
    tq_s = jnp.concatenate([jnp.zeros((L, MLA_NOPE_DIM), F32), ss, zeros32], axis=1) * scale
    tk = jnp.concatenate([cc, ss, jnp.zeros((L, 64), F32)], axis=1)
    return tq_c, tq_s, tk


def _swap_halves(w):
    half = MLA_ROPE_DIM // 2
    return jnp.concatenate([w[..., half:], w[..., :half]], axis=-1)


def _prep_inproj_weights(w_in_l, w_uq_l, w_ukv_l, q_norm_l, kv_norm_l):
    d = w_in_l.shape[0]
    a_u, a_v, cq, ckv, kr, c_u, d_q, d_f, d_i, d_g = jnp.split(
        w_in_l, (256, 512, 704, 832, 864, 1120, 1376, 1632, 1888), axis=1)
    z64 = jnp.zeros((d, 64), F32)
    w_ext = jnp.concatenate([a_u, a_v, c_u, d_q, d_f, d_i, d_g, cq, z64, ckv, kr, _swap_halves(kr), z64],
                            axis=1).astype(BF16)
    wq = w_uq_l.reshape(MLA_Q_RANK, MLA_HEADS, MLA_QK_DIM)
    zq32 = jnp.zeros((MLA_Q_RANK, MLA_HEADS, HEAD_PAD - MLA_QK_DIM), F32)
    zq64 = jnp.zeros((MLA_Q_RANK, MLA_HEADS, MLA_NOPE_DIM), F32)
    wq_a = jnp.concatenate([wq, zq32], axis=2).reshape(MLA_Q_RANK, QK_PAD)
    wq_b = jnp.concatenate([zq64, _swap_halves(wq[:, :, MLA_NOPE_DIM:]), zq32], axis=2).reshape(MLA_Q_RANK, QK_PAD)
    wq_ext = jnp.concatenate([wq_a, wq_b], axis=1)
    wq_ext = jnp.concatenate([wq_ext, jnp.zeros((256 - MLA_Q_RANK, 2 * QK_PAD), F32)], axis=0).astype(BF16)
    qn = jnp.concatenate([q_norm_l, jnp.zeros((256 - MLA_Q_RANK,), F32)]).reshape(1, 256)
    wkv = w_ukv_l.reshape(MLA_KV_RANK, MLA_HEADS, MLA_NOPE_DIM + MLA_V_DIM)
    zk64 = jnp.zeros((MLA_KV_RANK, MLA_HEADS, HEAD_PAD - MLA_NOPE_DIM), F32)
    wk = jnp.concatenate([wkv[:, :, :MLA_NOPE_DIM], zk64], axis=2).reshape(MLA_KV_RANK, QK_PAD)
    wv = wkv[:, :, MLA_NOPE_DIM:].reshape(MLA_KV_RANK, MLA_HEADS * MLA_V_DIM)
    eye = jnp.eye(MLA_ROPE_DIM, dtype=F32)
    place_h = jnp.concatenate([jnp.zeros((MLA_ROPE_DIM, MLA_NOPE_DIM), F32), eye,
                               jnp.zeros((MLA_ROPE_DIM, HEAD_PAD - MLA_QK_DIM), F32)], axis=1)
    place = jnp.concatenate([place_h] * MLA_HEADS, axis=1)
    place = jnp.concatenate([place, jnp.zeros((MLA_ROPE_DIM, MLA_HEADS * MLA_V_DIM), F32)], axis=1)
    top = jnp.concatenate([wk, wv], axis=1)
    wkv_ext = jnp.concatenate([top, place, place, jnp.zeros((64, QK_PAD + 256), F32)], axis=0).astype(BF16)
    kvn = kv_norm_l.reshape(1, MLA_KV_RANK)
    return w_ext, wq_ext, wkv_ext, qn, kvn


def _prep_gm(gm_ws_l, gm_bs_l):
    ck = jnp.arange(GM_BLOCK) // CHUNK
    mask = ck[None, :] <= ck[:, None]
    w = jnp.where(mask[None], gm_ws_l, 0).astype(BF16)
    bias = jnp.repeat(gm_bs_l.T, GROUP_WIDTH // GM_HEADS, axis=1)
    return w, bias


def _prep_s5(a_re, a_im, b_re, b_im, c_re, c_im, log_step):
    dt = jnp.exp(log_step)[:, None]
    mag = jnp.exp(dt * a_re)
    abar_re, abar_im = mag * jnp.cos(dt * a_im), mag * jnp.sin(dt * a_im)
    den = a_re * a_re + a_im * a_im
    nr, ni = abar_re - 1.0, abar_im
    zoh_re = (nr * a_re + ni * a_im) / den
    zoh_im = (ni * a_re - nr * a_im) / den
    bbar_re = zoh_re[..., None] * b_re - zoh_im[..., None] * b_im
    bbar_im = zoh_re[..., None] * b_im + zoh_im[..., None] * b_re
    eye_g = jnp.eye(S5_GROUPS, dtype=F32)
    bm_re = jnp.einsum('gpn,gh->gnhp', bbar_re, eye_g).reshape(GROUP_WIDTH, S5_NSTATE)
    bm_im = jnp.einsum('gpn,gh->gnhp', bbar_im, eye_g).reshape(GROUP_WIDTH, S5_NSTATE)
    bmat = jnp.concatenate([bm_re, bm_im], axis=1).astype(BF16)
    cm_re = jnp.einsum('gnp,gh->gphn', c_re, eye_g).reshape(S5_NSTATE, GROUP_WIDTH)
    cm_im = jnp.einsum('gnp,gh->gphn', c_im, eye_g).reshape(S5_NSTATE, GROUP_WIDTH)
    cmat = jnp.concatenate([cm_re, -cm_im], axis=0).astype(BF16)
    return bmat, abar_re.reshape(1, S5_NSTATE), abar_im.reshape(1, S5_NSTATE), cmat


def kernel(x, c, ln_in_g, ln_in_b, hg_lb_logits, w_ada, b_ada, w_in, gm_ln_g, gm_ln_b, gm_ws, gm_bs, mla_q_norm, mla_kv_norm, mla_w_uq, mla_w_ukv, s5_a_re, s5_a_im, s5_b_re, s5_b_im, s5_c_re, s5_c_im, s5_d, s5_log_step, s5_w_glu, s5_b_glu, hg_norm_g, merge_g, w_out, ln1_g, ln1_b, w_up, conv_w, conv_b, w_down, ln2_g, ln2_b):
    bsz, L, d = x.shape
    nl = w_in.shape[0]
    mod = _ada(c, w_ada, b_ada)
    xs = _ln_in(x.reshape(bsz * L, d), ln_in_g, ln_in_b).reshape(bsz, L, d)
    lb_cum = jnp.cumsum(jax.nn.softmax(hg_lb_logits.astype(F32), axis=0), axis=0)
    tq_c, tq_s, tk = _rope_tables(L)
    tri = jnp.tril(jnp.ones((CHUNK, CHUNK), F32)).astype(BF16)
    hid = jnp.arange(GROUP_WIDTH) // HG_DK
    ones_bd = (hid[:, None] == hid[None, :]).astype(BF16)
    row = lambda v: v.reshape(1, -1)
    w_ext, wq_ext, wkv_ext, qn, kvn = jax.vmap(_prep_inproj_weights)(w_in, mla_w_uq, mla_w_ukv, mla_q_norm,
                                                                     mla_kv_norm)
    gm_w, gm_bias = jax.vmap(_prep_gm)(gm_ws, gm_bs)
    bmat, a_re, a_im, cmat = jax.vmap(_prep_s5)(s5_a_re, s5_a_im, s5_b_re, s5_b_im, s5_c_re, s5_c_im, s5_log_step)
    wglu_bf, wout_bf, wup_bf, wdn_bf = (w.astype(BF16) for w in (s5_w_glu, w_out, w_up, w_down))

    for l in range(nl):
        mod3 = mod[l].reshape(bsz, 1, 6 * d)
        y_a, c_u, dd, q, k, v = _inproj(xs, mod3, w_ext[l], row(gm_ln_g[l]), row(gm_ln_b[l]), gm_w[l], gm_bias[l],
                                        qn[l], kvn[l], wq_ext[l], wkv_ext[l], tq_c, tq_s, tk)
        y_b = _attention(q, k, v)
        y_c = _s5(c_u, bsz, bmat[l], a_re[l], a_im[l], cmat[l], row(s5_d[l]), wglu_bf[l], row(s5_b_glu[l]))
        y_d = _hgrn(dd, row(lb_cum[l] - lb_cum[0]), row(hg_norm_g[l]), tri, ones_bd)
        xs = _merge(y_a, y_b, y_c, y_d, xs, mod3, row(merge_g[l]), wout_bf[l], row(ln1_g[l]), row(ln1_b[l]))
        xs = _ffn(xs, mod3, wup_bf[l], conv_w[l], row(conv_b[l]), wdn_bf[l], row(ln2_g[l]), row(ln2_b[l]))
    return xs
```

```python
import functools
import math

import jax
import jax.numpy as jnp
from jax import lax
from jax.experimental import pallas as pl
from jax.experimental.pallas import tpu as pltpu

F32 = jnp.float32
BF16 = jnp.bfloat16

D_MODEL = 1024
DEPTH = 2
CHUNK = 64
GROUP_WIDTH = 256
N_MIXERS = 4
GM_HEADS = 4
GM_BLOCK = 128
MLA_HEADS = 4
MLA_NOPE_DIM = 64
MLA_ROPE_DIM = 32
MLA_QK_DIM = MLA_NOPE_DIM + MLA_ROPE_DIM
MLA_V_DIM = 64
MLA_Q_RANK = 192
MLA_KV_RANK = 128
ROPE_THETA = 10000.0
S5_GROUP = 16
S5_GROUPS = 16
S5_STATE = 64
HG_HEADS = 4
HG_DK = 64
D_FF = 2816
CONV_W = 3
DN_ALPHA = (2 * DEPTH) ** 0.25
LN_EPS = 1e-5
RMS_EPS = 1e-6

LANES = 128
HEAD_PAD = 128
QK_PAD = MLA_HEADS * HEAD_PAD
S5_NSTATE = S5_GROUPS * S5_STATE
HG_SUB = 8

COL_AU, COL_AV, COL_CU, COL_DQ = 0, 256, 512, 768
COL_CQ, COL_CKV, COL_KR = 1792, 2048, 2176
IN_EXT = 2304

VMEM_LIMIT = 56 * 1024 * 1024


def _sigmoid(x):
    return 1.0 / (1.0 + jnp.exp(-x))


def _gelu(x):
    return 0.5 * x * (1.0 + jnp.tanh(math.sqrt(2.0 / math.pi) * (x + 0.044715 * (x * x * x))))


def _split2(x):
    hi = x.astype(BF16)
    lo = (x - hi.astype(F32)).astype(BF16)
    return hi, lo


def _split3(x):
    hi = x.astype(BF16)
    r = x - hi.astype(F32)
    mid = r.astype(BF16)
    lo = (r - mid.astype(F32)).astype(BF16)
    return hi, mid, lo


def _dot(a, b):
    return jnp.dot(a, b, preferred_element_type=F32)


def _dot_nt(a, b):
    return lax.dot_general(a, b, (((1,), (1,)), ((), ())), preferred_element_type=F32)


def _dot_tn(a, b):
    return lax.dot_general(a, b, (((0,), (0,)), ((), ())), preferred_element_type=F32)


def _layer_norm(x, g, b):
    mu = jnp.mean(x, axis=-1, keepdims=True)
    xc = x - mu
    var = jnp.mean(xc * xc, axis=-1, keepdims=True)
    return xc * lax.rsqrt(var + LN_EPS) * g + b


def _layer_spec(stacked, l, **kwargs):
    shape = stacked.shape[1:]
    return pl.BlockSpec((None,) + shape, lambda *_: (l,) + (0,) * len(shape), **kwargs)


def _mod_spec(mod4, l):
    return pl.BlockSpec((None, 1, 1, mod4.shape[-1]), lambda b, *_: (l, b, 0, 0))


def _cparams(sem, vmem=VMEM_LIMIT):
    return pltpu.CompilerParams(dimension_semantics=sem, vmem_limit_bytes=vmem)


def _ada_kernel(c_ref, w_ref, b_ref, o_ref):
    c = c_ref[...]
    chi, clo = _split2(c * _sigmoid(c))
    whi, wlo = _split2(w_ref[0])
    acc = _dot(chi, whi) + _dot(clo, whi) + _dot(chi, wlo)
    o_ref[0] = acc + b_ref[0]


def _ada(c, w_ada, b_ada):
    nl, d, n = w_ada.shape
    tn = 1536
    bsz = c.shape[0]
    return pl.pallas_call(
        _ada_kernel,
        out_shape=jax.ShapeDtypeStruct((nl, bsz, n), F32),
        grid=(nl, n // tn),
        in_specs=[pl.BlockSpec((bsz, d), lambda l, j: (0, 0)),
                  pl.BlockSpec((1, d, tn), lambda l, j: (l, 0, j)),
                  pl.BlockSpec((1, 1, tn), lambda l, j: (l, 0, j))],
        out_specs=pl.BlockSpec((1, bsz, tn), lambda l, j: (l, 0, j)),
        compiler_params=_cparams(("arbitrary", "arbitrary")),
        name="ada_mod",
    )(c, w_ada, b_ada.reshape(nl, 1, n))


def _inproj_kernel(x_ref, lng_ref, lnb_ref, mod_ref, w_ref, gmg_ref, gmb_ref, gmw_ref, gmbias_ref,
                   qn_ref, kvn_ref, wq_ref, wkv_ref, tq_c_ref, tq_s_ref, tk_ref,
                   ya_ref, cu_ref, d_ref, q_ref, k_ref, v_ref, *, tm, pre_ln):
    x = x_ref[0]
    if pre_ln:
        x = _layer_norm(x, lng_ref[...], lnb_ref[...])
    sh1 = mod_ref[0, :, 0:D_MODEL]
    sc1 = mod_ref[0, :, D_MODEL:2 * D_MODEL]
    h = (x * (1.0 + sc1) + sh1).astype(BF16)
    z = _dot(h, w_ref[...])

    cu_ref[...] = z[:, COL_CU:COL_CU + GROUP_WIDTH]
    d_ref[0] = z[:, COL_DQ:COL_DQ + 4 * GROUP_WIDTH]

    lane_head = lax.broadcasted_iota(jnp.int32, (GM_BLOCK, GROUP_WIDTH), 1) // (GROUP_WIDTH // GM_HEADS)
    for r in range(tm // GM_BLOCK):
        rows = slice(r * GM_BLOCK, (r + 1) * GM_BLOCK)
        u = _gelu(z[rows, COL_AU:COL_AU + GROUP_WIDTH])
        v = _gelu(z[rows, COL_AV:COL_AV + GROUP_WIDTH])
        vn = _layer_norm(v, gmg_ref[...], gmb_ref[...]).astype(BF16)
        zz = gmbias_ref[...]
        for hh in range(GM_HEADS):
            zz = zz + jnp.where(lane_head == hh, _dot(gmw_ref[hh], vn), 0.0)
        ya_ref[0, rows, :] = u * zz

    cq = z[:, COL_CQ:COL_CQ + 256]
    cqn = cq * lax.rsqrt(jnp.sum(cq * cq, axis=-1, keepdims=True) * (1.0 / MLA_Q_RANK) + RMS_EPS)
    cqn = (cqn * qn_ref[...]).astype(BF16)
    qq = _dot(cqn, wq_ref[...])
    tq_c = jnp.concatenate([tq_c_ref[...]] * MLA_HEADS, axis=1)
    tq_s = jnp.concatenate([tq_s_ref[...]] * MLA_HEADS, axis=1)
    q_ref[0] = (qq[:, :QK_PAD] * tq_c + qq[:, QK_PAD:] * tq_s).astype(BF16)

    ckv = z[:, COL_CKV:COL_CKV + MLA_KV_RANK]
    ckvn = ckv * lax.rsqrt(jnp.mean(ckv * ckv, axis=-1, keepdims=True) + RMS_EPS)
    ckvn = (ckvn * kvn_ref[...]).astype(BF16)
    krt = (z[:, COL_KR:COL_KR + 128] * tk_ref[...]).astype(BF16)
    kv = _dot(jnp.concatenate([ckvn, krt], axis=1), wkv_ref[...])
    k_ref[0] = kv[:, :QK_PAD].astype(BF16)
    for hh in range(MLA_HEADS):
        v_ref[0, hh] = kv[:, QK_PAD + hh * MLA_V_DIM:QK_PAD + (hh + 1) * MLA_V_DIM].astype(BF16)


def _inproj(x, l, pre_ln, ln_g, ln_b, mod4, w_ext, gm_g, gm_b, gm_w, gm_bias, qn, kvn, wq_ext, wkv_ext,
            tq_c, tq_s, tk):
    bsz, L, d = x.shape
    tm = 512
    per_layer = lambda a: _layer_spec(a, l)
    const = lambda shape: pl.BlockSpec(shape, lambda b, t: (0,) * len(shape))
    out_shape = (
        jax.ShapeDtypeStruct((bsz, L, GROUP_WIDTH), F32),
        jax.ShapeDtypeStruct((L, bsz * GROUP_WIDTH), F32),
        jax.ShapeDtypeStruct((bsz, L, 4 * GROUP_WIDTH), F32),
        jax.ShapeDtypeStruct((bsz, L, QK_PAD), BF16),
        jax.ShapeDtypeStruct((bsz, L, QK_PAD), BF16),
        jax.ShapeDtypeStruct((bsz, MLA_HEADS, L, MLA_V_DIM), BF16),
    )
    row = lambda w: pl.BlockSpec((1, tm, w), lambda b, t: (b, t, 0))
    return pl.pallas_call(
        functools.partial(_inproj_kernel, tm=tm, pre_ln=pre_ln),
        out_shape=out_shape,
        grid=(bsz, L // tm),
        in_specs=[row(d), const((1, d)), const((1, d)), _mod_spec(mod4, l),
                  per_layer(w_ext), per_layer(gm_g), per_layer(gm_b), per_layer(gm_w), per_layer(gm_bias),
                  per_layer(qn), per_layer(kvn), per_layer(wq_ext), per_layer(wkv_ext),
                  pl.BlockSpec((tm, HEAD_PAD), lambda b, t: (t, 0)),
                  pl.BlockSpec((tm, HEAD_PAD), lambda b, t: (t, 0)),
                  pl.BlockSpec((tm, 128), lambda b, t: (t, 0))],
        out_specs=(row(GROUP_WIDTH), pl.BlockSpec((tm, GROUP_WIDTH), lambda b, t: (t, b)),
                   row(4 * GROUP_WIDTH), row(QK_PAD), row(QK_PAD),
                   pl.BlockSpec((1, MLA_HEADS, tm, MLA_V_DIM), lambda b, t: (b, 0, t, 0))),
        compiler_params=_cparams(("arbitrary", "arbitrary")),
        name="inproj",
    )(x, ln_g, ln_b, mod4, w_ext, gm_g, gm_b, gm_w, gm_bias, qn, kvn, wq_ext, wkv_ext, tq_c, tq_s, tk)


ATT_TQ = 256
ATT_HEADS_PER_STEP = 2


def _attn_kernel(q_ref, k_ref, v_ref, o_ref):
    L = q_ref.shape[1]
    tq = ATT_TQ
    q_chunk = lax.broadcasted_iota(jnp.int32, (tq, tq), 0) // CHUNK
    k_chunk = lax.broadcasted_iota(jnp.int32, (tq, tq), 1) // CHUNK
    allowed = k_chunk <= q_chunk
    for qi in range(L // tq):
        lk = (qi + 1) * tq
        outs = []
        for hh in range(ATT_HEADS_PER_STEP):
            lanes = slice(hh * HEAD_PAD, (hh + 1) * HEAD_PAD)
            qh = q_ref[0, qi * tq:(qi + 1) * tq, lanes]
            s = _dot_nt(qh, k_ref[0, 0:lk, lanes])
            s_diag = jnp.where(allowed, s[:, lk - tq:], -1e30)
            s = s_diag if qi == 0 else jnp.concatenate([s[:, :lk - tq], s_diag], axis=1)
            m = jnp.max(s, axis=-1, keepdims=True)
            p = jnp.exp(s - m)
            l = jnp.sum(p, axis=-1, keepdims=True)
            outs.append(_dot(p.astype(BF16), v_ref[0, hh, 0:lk, :]) / l)
        o_ref[0, qi * tq:(qi + 1) * tq, :] = jnp.concatenate(outs, axis=1)


def _attention(q, k, v):
    bsz, L, _ = q.shape
    hp = ATT_HEADS_PER_STEP
    return pl.pallas_call(
        _attn_kernel,
        out_shape=jax.ShapeDtypeStruct((bsz, L, GROUP_WIDTH), F32),
        grid=(bsz, MLA_HEADS // hp),
        in_specs=[pl.BlockSpec((1, L, hp * HEAD_PAD), lambda b, g: (b, 0, g)),
                  pl.BlockSpec((1, L, hp * HEAD_PAD), lambda b, g: (b, 0, g)),
                  pl.BlockSpec((1, hp, L, MLA_V_DIM), lambda b, g: (b, g, 0, 0))],
        out_specs=pl.BlockSpec((1, L, hp * MLA_V_DIM), lambda b, g: (b, 0, g)),
        compiler_params=_cparams(("arbitrary", "arbitrary")),
        name="mla_attention",
    )(q, k, v)


def _s5_kernel(u_ref, bmat_ref, are_ref, aim_ref, cmat_ref, dvec_ref, wglu_ref, bglu_ref,
               o_ref, h_ref, us_ref, bu_ref, hs_ref, os_ref, *, tc, nb):
    @pl.when(pl.program_id(0) == 0)
    def _():
        h_ref[...] = jnp.zeros(h_ref.shape, F32)

    nslab = GROUP_WIDTH // LANES
    for bb in range(nb):
        for s in range(nslab):
            lanes = slice(bb * GROUP_WIDTH + s * LANES, bb * GROUP_WIDTH + (s + 1) * LANES)
            us_ref[s, pl.ds(bb, tc, stride=nb), :] = u_ref[:, lanes]
    u = jnp.concatenate([us_ref[s] for s in range(nslab)], axis=1)
    bu_ref[...] = _dot(u.astype(BF16), bmat_ref[...])
    a_re = jnp.broadcast_to(are_ref[...], (nb, S5_NSTATE))
    a_im = jnp.broadcast_to(aim_ref[...], (nb, S5_NSTATE))
    re = slice(0, S5_NSTATE)
    im = slice(S5_NSTATE, 2 * S5_NSTATE)

    def step(i, carry):
        h_re, h_im = carry
        r0 = pl.ds(pl.multiple_of(i * 2 * nb, 2 * nb), nb)
        r1 = pl.ds(pl.multiple_of(i * 2 * nb, 2 * nb) + nb, nb)
        m_re = a_re * h_re - a_im * h_im + bu_ref[r0, re]
        m_im = a_re * h_im + a_im * h_re + bu_ref[r0, im]
        n_re = a_re * m_re - a_im * m_im + bu_ref[r1, re]
        n_im = a_re * m_im + a_im * m_re + bu_ref[r1, im]
        rr = pl.ds(pl.multiple_of(i * 2 * nb, 2 * nb), 2 * nb)
        hs_ref[rr, re] = jnp.concatenate([m_re, n_re], axis=0).astype(BF16)
        hs_ref[rr, im] = jnp.concatenate([m_im, n_im], axis=0).astype(BF16)
        return n_re, n_im

    h_re, h_im = lax.fori_loop(0, tc // 2, step, (h_ref[:, re], h_ref[:, im]))
    h_ref[:, re] = h_re
    h_ref[:, im] = h_im

    half = (tc * nb) // 2
    y = jnp.concatenate([_dot(hs_ref[0:half, :], cmat_ref[...]), _dot(hs_ref[half:2 * half, :], cmat_ref[...])],
                        axis=0) + dvec_ref[...] * u
    z = _gelu(y)
    z_bf = z.astype(BF16)
    gate = _sigmoid(jnp.concatenate([_dot(z_bf[0:half], wglu_ref[...]), _dot(z_bf[half:2 * half], wglu_ref[...])],
                                    axis=0) + bglu_ref[...])
    res = z * gate
    for s in range(nslab):
        os_ref[s] = res[:, s * LANES:(s + 1) * LANES]
    for bb in range(nb):
        for s in range(nslab):
            lanes = slice(bb * GROUP_WIDTH + s * LANES, bb * GROUP_WIDTH + (s + 1) * LANES)
            o_ref[:, lanes] = os_ref[s, pl.ds(bb, tc, stride=nb), :]


def _s5(u_t, nb, l, bmat, a_re, a_im, cmat, dvec, wglu, bglu):
    L = u_t.shape[0]
    gw = GROUP_WIDTH
    tc = 128
    per_layer = lambda a: _layer_spec(a, l)
    return pl.pallas_call(
        functools.partial(_s5_kernel, tc=tc, nb=nb),
        out_shape=jax.ShapeDtypeStruct((L, nb * gw), F32),
        grid=(L // tc,),
        in_specs=[pl.BlockSpec((tc, nb * gw), lambda t: (t, 0)),
                  per_layer(bmat), per_layer(a_re), per_layer(a_im), per_layer(cmat), per_layer(dvec),
                  per_layer(wglu), per_layer(bglu)],
        out_specs=pl.BlockSpec((tc, nb * gw), lambda t: (t, 0)),
        scratch_shapes=[pltpu.VMEM((nb, 2 * S5_NSTATE), F32),
                        pltpu.VMEM((gw // LANES, tc * nb, LANES), F32),
                        pltpu.VMEM((tc * nb, 2 * S5_NSTATE), F32),
                        pltpu.VMEM((tc * nb, 2 * S5_NSTATE), BF16),
                        pltpu.VMEM((gw // LANES, tc * nb, LANES), F32)],
        compiler_params=_cparams(("arbitrary",)),
        name="s5_scan",
    )(u_t, bmat, a_re, a_im, cmat, dvec, wglu, bglu)


LOG2E = 1.4426950408889634
HG_BATCH = 2


def _hgrn_chunk(q, fl, iv, lb, tri, ones_bd, st):
    nsub = CHUNK // HG_SUB
    f = lb + (1.0 - lb) * _sigmoid(fl)
    logf = jnp.log(f)
    kk = 1.0 - f
    hi, mid, lo = _split3(logf)
    b2 = (_dot(tri, hi) + _dot(tri, mid) + _dot(tri, lo)) * LOG2E
    lane_head = lax.broadcasted_iota(jnp.int32, (HG_SUB, GROUP_WIDTH), 1) // HG_DK
    row_id = lax.broadcasted_iota(jnp.int32, (HG_SUB, GROUP_WIDTH), 0)
    v_bf = iv.astype(BF16)

    terms = []
    for si in range(nsub):
        r0 = si * HG_SUB
        q_i, k_i, b_i = (a[r0:r0 + HG_SUB] for a in (q, kk, b2))
        for j in range(HG_SUB):
            e = jnp.exp2(jnp.where(row_id >= j, b_i - b_i[j:j + 1], -1e30))
            terms.append(q_i * k_i[j:j + 1] * e)
    p = _dot(jnp.concatenate(terms, axis=0).astype(BF16), ones_bd)
    o_parts = []
    for si in range(nsub):
        r0 = si * HG_SUB
        base = si * HG_SUB * HG_SUB
        o_i = p[base:base + HG_SUB] * iv[r0:r0 + 1]
        for j in range(1, HG_SUB):
            o_i = o_i + p[base + j * HG_SUB:base + (j + 1) * HG_SUB] * iv[r0 + j:r0 + j + 1]
        o_parts.append(o_i)

    atts = []
    for si in range(1, nsub):
        r0 = si * HG_SUB
        b_edge = b2[r0 - 1:r0]
        q_t = q[r0:r0 + HG_SUB] * jnp.exp2(b2[r0:r0 + HG_SUB] - b_edge)
        k_t = jnp.concatenate([kk[0:r0] * jnp.exp2(b_edge - b2[0:r0]),
                               jnp.zeros((CHUNK - r0, GROUP_WIDTH), F32)], axis=0).astype(BF16)
        q_exp = jnp.concatenate([jnp.where(lane_head == hh, q_t, 0.0) for hh in range(HG_HEADS)],
                                axis=0).astype(BF16)
        atts.append(_dot_nt(q_exp, k_t))
    o2 = _dot(jnp.concatenate(atts, axis=0).astype(BF16), v_bf)
    for si in range(1, nsub):
        base = (si - 1) * HG_HEADS * HG_SUB
        for hh in range(HG_HEADS):
            o_parts[si] = o_parts[si] + jnp.where(lane_head == hh,
                                                  o2[base + hh * HG_SUB:base + (hh + 1) * HG_SUB], 0.0)
    o_intra = jnp.concatenate(o_parts, axis=0)

    o_inter = _dot_nt((q * jnp.exp2(b2)).astype(BF16), st.astype(BF16))
    b_last = b2[CHUNK - 1:CHUNK]
    k_hat = (kk * jnp.exp2(b_last - b2)).astype(BF16)
    upd = _dot_tn(v_bf, k_hat)
    bd = (lax.broadcasted_iota(jnp.int32, (GROUP_WIDTH, GROUP_WIDTH), 0) // HG_DK
          == lax.broadcasted_iota(jnp.int32, (GROUP_WIDTH, GROUP_WIDTH), 1) // HG_DK)
    st_new = st * jnp.exp2(b_last) + jnp.where(bd, upd, 0.0)
    return o_intra + o_inter, st_new


def _hgrn_kernel(d_ref, lb_ref, gn_ref, tri_ref, ones_ref, o_ref, st_ref, *, th):
    @pl.when(pl.program_id(1) == 0)
    def _():
        st_ref[...] = jnp.zeros(st_ref.shape, F32)

    lb = lb_ref[...]
    tri = tri_ref[...]
    ones_bd = ones_ref[...]
    for c in range(th // CHUNK):
        rows = slice(c * CHUNK, (c + 1) * CHUNK)
        for bb in range(d_ref.shape[0]):
            q = d_ref[bb, rows, 0:GROUP_WIDTH]
            fl = d_ref[bb, rows, GROUP_WIDTH:2 * GROUP_WIDTH]
            iv = d_ref[bb, rows, 2 * GROUP_WIDTH:3 * GROUP_WIDTH]
            g = d_ref[bb, rows, 3 * GROUP_WIDTH:4 * GROUP_WIDTH]
            o, st_new = _hgrn_chunk(q, fl, iv, lb, tri, ones_bd, st_ref[bb])
            st_ref[bb] = st_new
            sq_hi, sq_lo = _split2(o * o)
            ms = (_dot(sq_hi, ones_bd) + _dot(sq_lo, ones_bd)) * (1.0 / HG_DK)
            o = o * lax.rsqrt(ms + RMS_EPS) * gn_ref[...]
            o_ref[bb, rows, :] = o * _sigmoid(g)


def _hgrn(d, l, lb, gn, tri, ones_bd):
    bsz, L, _ = d.shape
    th = 256
    nbb = HG_BATCH if bsz % HG_BATCH == 0 else 1
    const = lambda shape: pl.BlockSpec(shape, lambda b, t: (0,) * len(shape))
    return pl.pallas_call(
        functools.partial(_hgrn_kernel, th=th),
        out_shape=jax.ShapeDtypeStruct((bsz, L, GROUP_WIDTH), F32),
        grid=(bsz // nbb, L // th),
        in_specs=[pl.BlockSpec((nbb, th, 4 * GROUP_WIDTH), lambda b, t: (b, t, 0)),
                  _layer_spec(lb, l), _layer_spec(gn, l),
                  const((CHUNK, CHUNK)), const((GROUP_WIDTH, GROUP_WIDTH))],
        out_specs=pl.BlockSpec((nbb, th, GROUP_WIDTH), lambda b, t: (b, t, 0)),
        scratch_shapes=[pltpu.VMEM((nbb, GROUP_WIDTH, GROUP_WIDTH), F32)],
        compiler_params=_cparams(("arbitrary", "arbitrary")),
        name="hgrn2",
    )(d, lb, gn, tri, ones_bd)


def _merge_kernel(ya_ref, yb_ref, yc_ref, yd_ref, x_ref, lng_ref, lnb_ref, mod_ref, mg_ref, wo_ref, g_ref, b_ref,
                  o_ref, *, pre_ln):
    acc = None
    for gi, y_ref in enumerate((ya_ref, yb_ref, yc_ref, yd_ref)):
        y = y_ref[...] if gi == 2 else y_ref[0]
        yn = y * lax.rsqrt(jnp.mean(y * y, axis=-1, keepdims=True) + RMS_EPS)
        yn = (yn * mg_ref[:, gi * GROUP_WIDTH:(gi + 1) * GROUP_WIDTH]).astype(BF16)
        part = _dot(yn, wo_ref[gi * GROUP_WIDTH:(gi + 1) * GROUP_WIDTH, :])
        acc = part if acc is None else acc + part
    g1 = mod_ref[0, :, 2 * D_MODEL:3 * D_MODEL]
    x = x_ref[0]
    if pre_ln:
        x = _layer_norm(x, lng_ref[...], lnb_ref[...])
    o_ref[0] = _layer_norm(DN_ALPHA * x + (1.0 + g1) * acc, g_ref[...], b_ref[...])


def _merge(ya, yb, yc, yd, x, l, pre_ln, ln_g, ln_b, mod4, mg, wo, g, b):
    bsz, L, d = x.shape
    tm = 512
    per_layer = lambda a: _layer_spec(a, l)
    const = lambda shape: pl.BlockSpec(shape, lambda bb, t: (0,) * len(shape))
    row = lambda w: pl.BlockSpec((1, tm, w), lambda bb, t: (bb, t, 0))
    return pl.pallas_call(
        functools.partial(_merge_kernel, pre_ln=pre_ln),
        out_shape=jax.ShapeDtypeStruct((bsz, L, d), F32),
        grid=(bsz, L // tm),
        in_specs=[row(GROUP_WIDTH), row(GROUP_WIDTH), pl.BlockSpec((tm, GROUP_WIDTH), lambda bb, t: (t, bb)),
                  row(GROUP_WIDTH), row(d), const((1, d)), const((1, d)), _mod_spec(mod4, l),
                  per_layer(mg), per_layer(wo), per_layer(g), per_layer(b)],
        out_specs=row(d),
        compiler_params=_cparams(("arbitrary", "arbitrary")),
        name="merge_out",
    )(ya, yb, yc, yd, x, ln_g, ln_b, mod4, mg, wo, g, b)


FF_TILE = 256
FF_HALO = 16


FF_DOWN_GROUP = 4


FF_SEGS = 8


def _ffn_kernel(x_ref, xh_ref, mod_ref, wup_ref, cw_ref, cb_ref, wdn_ref, g_ref, b_ref, o_ref,
                hp_ref, hx_ref, a_ref, act_ref, ap_ref, *, tm):
    ti = pl.program_id(1)
    seg = tm // FF_SEGS
    nslab = D_MODEL // LANES
    sh2 = mod_ref[0, :, 3 * D_MODEL:4 * D_MODEL]
    sc2 = mod_ref[0, :, 4 * D_MODEL:5 * D_MODEL]
    g2 = mod_ref[0, :, 5 * D_MODEL:6 * D_MODEL]
    x = x_ref[0]
    h2 = x * (1.0 + sc2) + sh2
    halo = jnp.where(ti > 0, xh_ref[0] * (1.0 + sc2) + sh2, 0.0)
    for s in range(nslab):
        lanes = slice(s * LANES, (s + 1) * LANES)
        hp_ref[s, 0:FF_HALO, :] = halo[:, lanes]
        for p in range(FF_SEGS):
            hp_ref[s, pl.ds(FF_HALO + p, seg, stride=FF_SEGS), :] = h2[p * seg:(p + 1) * seg, lanes]
    hx_ref[...] = jnp.concatenate([hp_ref[s] for s in range(nslab)], axis=1).astype(BF16)
    nj = D_FF // FF_TILE
    sub0 = lax.broadcasted_iota(jnp.int32, (FF_SEGS, FF_TILE), 0) == 0

    def up(j):
        hx = hx_ref[...]
        for half in range(2):
            a = a_ref.at[j % 2, half]
            cols = slice(half * D_FF + j * FF_TILE, half * D_FF + (j + 1) * FF_TILE)
            a[...] = _dot(hx, wup_ref[:, cols])
            prev1 = jnp.where(sub0, a[FF_HALO - 1:FF_HALO, :], pltpu.roll(a[FF_HALO + tm - 8:FF_HALO + tm, :], 1, 0))
            prev2 = jnp.where(sub0, a[FF_HALO - 2:FF_HALO - 1, :],
                              pltpu.roll(a[FF_HALO + tm - 16:FF_HALO + tm - 8, :], 1, 0))
            a[8:16, :] = prev1
            a[0:8, :] = prev2

    def conv(j, half):
        cols = slice(half * D_FF + j * FF_TILE, half * D_FF + (j + 1) * FF_TILE)
        a = a_ref.at[j % 2, half]
        return (a[0:tm, :] * cw_ref[0:1, cols] + a[8:tm + 8, :] * cw_ref[1:2, cols]
                + a[16:tm + 16, :] * cw_ref[2:3, cols] + cb_ref[:, cols])

    up(0)
    acc = None
    for j in range(nj):
        if j + 1 < nj:
            up(j + 1)
        gate = conv(j, 1)
        act_ref[:, j * FF_TILE:(j + 1) * FF_TILE] = (conv(j, 0) * (gate * _sigmoid(gate))).astype(BF16)
        if (j + 1) % FF_DOWN_GROUP == 0 or j + 1 == nj:
            k0 = (j // FF_DOWN_GROUP) * FF_DOWN_GROUP * FF_TILE
            k1 = (j + 1) * FF_TILE
            part = _dot(act_ref[:, k0:k1], wdn_ref[k0:k1, :])
            acc = part if acc is None else acc + part
    for s in range(nslab):
        ap_ref[s] = acc[:, s * LANES:(s + 1) * LANES]
    acc_nat = jnp.concatenate(
        [jnp.concatenate([ap_ref[s, pl.ds(p, seg, stride=FF_SEGS), :] for p in range(FF_SEGS)], axis=0)
         for s in range(nslab)], axis=1)
    o_ref[0] = _layer_norm(DN_ALPHA * x + (1.0 + g2) * acc_nat, g_ref[...], b_ref[...])


def _ffn(x, l, mod4, wup, cw, cb, wdn, g, b):
    bsz, L, d = x.shape
    tm = 512
    per_layer = lambda a: _layer_spec(a, l, pipeline_mode=pl.Buffered(1))
    hb = tm // FF_HALO
    return pl.pallas_call(
        functools.partial(_ffn_kernel, tm=tm),
        out_shape=jax.ShapeDtypeStruct((bsz, L, d), F32),
        grid=(bsz, L // tm),
        in_specs=[pl.BlockSpec((1, tm, d), lambda bb, t: (bb, t, 0)),
                  pl.BlockSpec((1, FF_HALO, d), lambda bb, t: (bb, jnp.maximum(t * hb - 1, 0), 0)),
                  _mod_spec(mod4, l),
                  per_layer(wup), per_layer(cw), per_layer(cb), per_layer(wdn), per_layer(g), per_layer(b)],
        out_specs=pl.BlockSpec((1, tm, d), lambda bb, t: (bb, t, 0)),
        scratch_shapes=[pltpu.VMEM((d // LANES, tm + FF_HALO, LANES), F32),
                        pltpu.VMEM((tm + FF_HALO, d), BF16),
                        pltpu.VMEM((2, 2, tm + FF_HALO, FF_TILE), F32),
                        pltpu.VMEM((tm, D_FF), BF16),
                        pltpu.VMEM((d // LANES, tm, LANES), F32)],
        compiler_params=_cparams(("arbitrary", "arbitrary")),
        name="conv_ffn",
    )(x, x, mod4, wup, cw, cb, wdn, g, b)


def _rope_tables(L):
    inv = 1.0 / (ROPE_THETA ** (jnp.arange(0, MLA_ROPE_DIM, 2, dtype=F32) / MLA_ROPE_DIM))
    ang = jnp.arange(L, dtype=F32)[:, None] * inv[None, :]
    cos, sin = jnp.cos(ang), jnp.sin(ang)
    cc = jnp.concatenate([cos, cos], axis=1)
    ss = jnp.concatenate([-sin, sin], axis=1)
    scale = MLA_QK_DIM ** -0.5
    zeros32 = jnp.zeros((L, HEAD_PAD - MLA_QK_DIM), F32)
    tq_c = jnp.concatenate([jnp.ones((L, MLA_NOPE_DIM), F32), cc, zeros32], axis=1) * scale
    tq_s = jnp.concatenate([jnp.zeros((L, MLA_NOPE_DIM), F32), ss, zeros32], axis=1) * scale
    tk = jnp.concatenate([cc, ss, jnp.zeros((L, 64), F32)], axis=1)
    return tq_c, tq_s, tk


def _swap_halves(w):
    half = MLA_ROPE_DIM // 2
    return jnp.concatenate([w[..., half:], w[..., :half]], axis=-1)


def _prep_inproj_weights(w_in_l, w_uq_l, w_ukv_l, q_norm_l, kv_norm_l):
    d = w_in_l.shape[0]
    a_u, a_v, cq, ckv, kr, c_u, d_q, d_f, d_i, d_g = jnp.split(
        w_in_l, (256, 512, 704, 832, 864, 1120, 1376, 1632, 1888), axis=1)
    z64 = jnp.zeros((d, 64), F32)
    w_ext = jnp.concatenate([a_u, a_v, c_u, d_q, d_f, d_i, d_g, cq, z64, ckv, kr, _swap_halves(kr), z64],
                            axis=1).astype(BF16)
    wq = w_uq_l.reshape(MLA_Q_RANK, MLA_HEADS, MLA_QK_DIM)
    zq32 = jnp.zeros((MLA_Q_RANK, MLA_HEADS, HEAD_PAD - MLA_QK_DIM), F32)
    zq64 = jnp.zeros((MLA_Q_RANK, MLA_HEADS, MLA_NOPE_DIM), F32)
    wq_a = jnp.concatenate([wq, zq32], axis=2).reshape(MLA_Q_RANK, QK_PAD)
    wq_b = jnp.concatenate([zq64, _swap_halves(wq[:, :, MLA_NOPE_DIM:]), zq32], axis=2).reshape(MLA_Q_RANK, QK_PAD)
    wq_ext = jnp.concatenate([wq_a, wq_b], axis=1)
    wq_ext = jnp.concatenate([wq_ext, jnp.zeros((256 - MLA_Q_RANK, 2 * QK_PAD), F32)], axis=0).astype(BF16)
    qn = jnp.concatenate([q_norm_l, jnp.zeros((256 - MLA_Q_RANK,), F32)]).reshape(1, 256)
    wkv = w_ukv_l.reshape(MLA_KV_RANK, MLA_HEADS, MLA_NOPE_DIM + MLA_V_DIM)
    zk64 = jnp.zeros((MLA_KV_RANK, MLA_HEADS, HEAD_PAD - MLA_NOPE_DIM), F32)
    wk = jnp.concatenate([wkv[:, :, :MLA_NOPE_DIM], zk64], axis=2).reshape(MLA_KV_RANK, QK_PAD)
    wv = wkv[:, :, MLA_NOPE_DIM:].reshape(MLA_KV_RANK, MLA_HEADS * MLA_V_DIM)
    eye = jnp.eye(MLA_ROPE_DIM, dtype=F32)
    place_h = jnp.concatenate([jnp.zeros((MLA_ROPE_DIM, MLA_NOPE_DIM), F32), eye,
                               jnp.zeros((MLA_ROPE_DIM, HEAD_PAD - MLA_QK_DIM), F32)], axis=1)
    place = jnp.concatenate([place_h] * MLA_HEADS, axis=1)
    place = jnp.concatenate([place, jnp.zeros((MLA_ROPE_DIM, MLA_HEADS * MLA_V_DIM), F32)], axis=1)
    top = jnp.concatenate([wk, wv], axis=1)
    wkv_ext = jnp.concatenate([top, place, place, jnp.zeros((64, QK_PAD + 256), F32)], axis=0).astype(BF16)
    kvn = kv_norm_l.reshape(1, MLA_KV_RANK)
    return w_ext, wq_ext, wkv_ext, qn, kvn


def _prep_gm(gm_ws_l, gm_bs_l):
    ck = jnp.arange(GM_BLOCK) // CHUNK
    mask = ck[None, :] <= ck[:, None]
    w = jnp.where(mask[None], gm_ws_l, 0).astype(BF16)
    bias = jnp.repeat(gm_bs_l.T, GROUP_WIDTH // GM_HEADS, axis=1)
    return w, bias


def _prep_s5(a_re, a_im, b_re, b_im, c_re, c_im, log_step):
    dt = jnp.exp(log_step)[:, None]
    mag = jnp.exp(dt * a_re)
    abar_re, abar_im = mag * jnp.cos(dt * a_im), mag * jnp.sin(dt * a_im)
    den = a_re * a_re + a_im * a_im
    nr, ni = abar_re - 1.0, abar_im
    zoh_re = (nr * a_re + ni * a_im) / den
    zoh_im = (ni * a_re - nr * a_im) / den
    bbar_re = zoh_re[..., None] * b_re - zoh_im[..., None] * b_im
    bbar_im = zoh_re[..., None] * b_im + zoh_im[..., None] * b_re
    eye_g = jnp.eye(S5_GROUPS, dtype=F32)
    bm_re = jnp.einsum('gpn,gh->gnhp', bbar_re, eye_g).reshape(GROUP_WIDTH, S5_NSTATE)
    bm_im = jnp.einsum('gpn,gh->gnhp', bbar_im, eye_g).reshape(GROUP_WIDTH, S5_NSTATE)
    bmat = jnp.concatenate([bm_re, bm_im], axis=1).astype(BF16)
    cm_re = jnp.einsum('gnp,gh->gphn', c_re, eye_g).reshape(S5_NSTATE, GROUP_WIDTH)
    cm_im = jnp.einsum('gnp,gh->gphn', c_im, eye_g).reshape(S5_NSTATE, GROUP_WIDTH)
    cmat = jnp.concatenate([cm_re, -cm_im], axis=0).astype(BF16)
    return bmat, abar_re.reshape(1, S5_NSTATE), abar_im.reshape(1, S5_NSTATE), cmat


def kernel(x, c, ln_in_g, ln_in_b, hg_lb_logits, w_ada, b_ada, w_in, gm_ln_g, gm_ln_b, gm_ws, gm_bs, mla_q_norm, mla_kv_norm, mla_w_uq, mla_w_ukv, s5_a_re, s5_a_im, s5_b_re, s5_b_im, s5_c_re, s5_c_im, s5_d, s5_log_step, s5_w_glu, s5_b_glu, hg_norm_g, merge_g, w_out, ln1_g, ln1_b, w_up, conv_w, conv_b, w_down, ln2_g, ln2_b):
    bsz, L, d = x.shape
    nl = w_in.shape[0]
    mod = _ada(c, w_ada, b_ada)
    lb_cum = jnp.cumsum(jax.nn.softmax(hg_lb_logits.astype(F32), axis=0), axis=0)
    tq_c, tq_s, tk = _rope_tables(L)
    tri = jnp.tril(jnp.ones((CHUNK, CHUNK), F32)).astype(BF16)
    hid = jnp.arange(GROUP_WIDTH) // HG_DK
    ones_bd = (hid[:, None] == hid[None, :]).astype(BF16)
    rows = lambda v: v.reshape(v.shape[0], 1, -1)
    w_ext, wq_ext, wkv_ext, qn, kvn = jax.vmap(_prep_inproj_weights)(w_in, mla_w_uq, mla_w_ukv, mla_q_norm,
                                                                     mla_kv_norm)
    gm_w, gm_bias = jax.vmap(_prep_gm)(gm_ws, gm_bs)
    bmat, a_re, a_im, cmat = jax.vmap(_prep_s5)(s5_a_re, s5_a_im, s5_b_re, s5_b_im, s5_c_re, s5_c_im, s5_log_step)
    wglu_bf, wout_bf, wup_bf, wdn_bf = (w.astype(BF16) for w in (s5_w_glu, w_out, w_up, w_down))

    mod4 = mod.reshape(nl, bsz, 1, 6 * d)
    lbs = rows(lb_cum - lb_cum[0:1])

    xs = x
    ln_g, ln_b = ln_in_g.reshape(1, d), ln_in_b.reshape(1, d)
    for l in range(nl):
        pre_ln = l == 0
        y_a, c_u, dd, q, k, v = _inproj(xs, l, pre_ln, ln_g, ln_b, mod4, w_ext, rows(gm_ln_g), rows(gm_ln_b), gm_w, gm_bias,
                                        qn, kvn, wq_ext, wkv_ext, tq_c, tq_s, tk)
        y_b = _attention(q, k, v)
        y_c = _s5(c_u, bsz, l, bmat, a_re, a_im, cmat, rows(s5_d), wglu_bf, rows(s5_b_glu))
        y_d = _hgrn(dd, l, lbs, rows(hg_norm_g), tri, ones_bd)
        xs = _merge(y_a, y_b, y_c, y_d, xs, l, pre_ln, ln_g, ln_b, mod4, rows(merge_g), wout_bf, rows(ln1_g), rows(ln1_b))
        xs = _ffn(xs, l, mod4, wup_bf, conv_w, rows(conv_b), wdn_bf, rows(ln2_g), rows(ln2_b))
    return xs
```

```python
import functools
import math

import jax
import jax.numpy as jnp
from jax import lax
from jax.experimental import pallas as pl
from jax.experimental.pallas import tpu as pltpu

F32 = jnp.float32
BF16 = jnp.bfloat16

D_MODEL = 1024
DEPTH = 2
CHUNK = 64
GROUP_WIDTH = 256
N_MIXERS = 4
GM_HEADS = 4
GM_BLOCK = 128
MLA_HEADS = 4
MLA_NOPE_DIM = 64
MLA_ROPE_DIM = 32
MLA_QK_DIM = MLA_NOPE_DIM + MLA_ROPE_DIM
MLA_V_DIM = 64
MLA_Q_RANK = 192
MLA_KV_RANK = 128
ROPE_THETA = 10000.0
S5_GROUP = 16
S5_GROUPS = 16
S5_STATE = 64
HG_HEADS = 4
HG_DK = 64
D_FF = 2816
CONV_W = 3
DN_ALPHA = (2 * DEPTH) ** 0.25
LN_EPS = 1e-5
RMS_EPS = 1e-6
LOG2E = 1.4426950408889634

LANES = 128
HEAD_PAD = 128
QK_PAD = MLA_HEADS * HEAD_PAD
S5_NSTATE = S5_GROUPS * S5_STATE
HG_SUB = 8

COL_AU, COL_AV, COL_CU, COL_DQ = 0, 256, 512, 768
COL_CQ, COL_CKV, COL_KR = 1792, 2048, 2176
IN_EXT = 2304

FF_TILE = 256
FF_HALO = 16
FF_ROWS = 512
FF_SEGS = 8
FF_DOWN_SPLITS = (4, 8, 10, 11)

VMEM_LIMIT = 56 * 1024 * 1024


def _sigmoid(x):
    return 1.0 / (1.0 + jnp.exp(-x))


def _gelu(x):
    return 0.5 * x * (1.0 + jnp.tanh(math.sqrt(2.0 / math.pi) * (x + 0.044715 * (x * x * x))))


def _split2(x):
    hi = x.astype(BF16)
    lo = (x - hi.astype(F32)).astype(BF16)
    return hi, lo


def _split3(x):
    hi = x.astype(BF16)
    r = x - hi.astype(F32)
    mid = r.astype(BF16)
    lo = (r - mid.astype(F32)).astype(BF16)
    return hi, mid, lo


def _dot(a, b):
    return jnp.dot(a, b, preferred_element_type=F32)


def _dot_nt(a, b):
    return lax.dot_general(a, b, (((1,), (1,)), ((), ())), preferred_element_type=F32)


def _dot_tn(a, b):
    return lax.dot_general(a, b, (((0,), (0,)), ((), ())), preferred_element_type=F32)


def _layer_norm(x, g, b):
    mu = jnp.mean(x, axis=-1, keepdims=True)
    xc = x - mu
    var = jnp.mean(xc * xc, axis=-1, keepdims=True)
    return xc * lax.rsqrt(var + LN_EPS) * g + b


def _layer_spec(stacked, l, **kwargs):
    shape = stacked.shape[1:]
    return pl.BlockSpec((None,) + shape, lambda *_: (l,) + (0,) * len(shape), **kwargs)


def _whole_spec(arr):
    return pl.BlockSpec(arr.shape, lambda *_: (0,) * arr.ndim)


def _mod_spec(mod4, l):
    return pl.BlockSpec((None, 1, 1, mod4.shape[-1]), lambda b, *_: (l, b, 0, 0))


def _cparams(sem, vmem=VMEM_LIMIT):
    return pltpu.CompilerParams(dimension_semantics=sem, vmem_limit_bytes=vmem)


def _ada_kernel(c_ref, w_ref, b_ref, o_ref):
    c = c_ref[...]
    chi, clo = _split2(c * _sigmoid(c))
    whi, wlo = _split2(w_ref[0])
    acc = _dot(chi, whi) + _dot(clo, whi) + _dot(chi, wlo)
    o_ref[0] = acc + b_ref[0]


def _ada(c, w_ada, b_ada):
    nl, d, n = w_ada.shape
    tn = 1536
    bsz = c.shape[0]
    return pl.pallas_call(
        _ada_kernel,
        out_shape=jax.ShapeDtypeStruct((nl, bsz, n), F32),
        grid=(nl, n // tn),
        in_specs=[pl.BlockSpec((bsz, d), lambda l, j: (0, 0)),
                  pl.BlockSpec((1, d, tn), lambda l, j: (l, 0, j)),
                  pl.BlockSpec((1, 1, tn), lambda l, j: (l, 0, j))],
        out_specs=pl.BlockSpec((1, bsz, tn), lambda l, j: (l, 0, j)),
        compiler_params=_cparams(("arbitrary", "arbitrary")),
        name="ada_mod",
    )(c, w_ada, b_ada.reshape(nl, 1, n))


def _inproj_kernel(x_ref, lng_ref, lnb_ref, mod_ref, w_ref, gmg_ref, gmb_ref, gmw_ref, gmbias_ref,
                   qn_ref, kvn_ref, wq_ref, wkv_ref, tq_c_ref, tq_s_ref, tk_ref,
                   ya_ref, cu_ref, d_ref, q_ref, k_ref, v_ref, *, tm, pre_ln, l):
    x = x_ref[0]
    if pre_ln:
        x = _layer_norm(x, lng_ref[...], lnb_ref[...])
    sh1 = mod_ref[0, :, 0:D_MODEL]
    sc1 = mod_ref[0, :, D_MODEL:2 * D_MODEL]
    h = (x * (1.0 + sc1) + sh1).astype(BF16)
    z = _dot(h, w_ref[...])

    cu_ref[...] = z[:, COL_CU:COL_CU + GROUP_WIDTH]
    d_ref[0] = z[:, COL_DQ:COL_DQ + 4 * GROUP_WIDTH]

    lane_head = lax.broadcasted_iota(jnp.int32, (GM_BLOCK, GROUP_WIDTH), 1) // (GROUP_WIDTH // GM_HEADS)
    for r in range(tm // GM_BLOCK):
        rows = slice(r * GM_BLOCK, (r + 1) * GM_BLOCK)
        u = _gelu(z[rows, COL_AU:COL_AU + GROUP_WIDTH])
        v = _gelu(z[rows, COL_AV:COL_AV + GROUP_WIDTH])
        vn = _layer_norm(v, gmg_ref[l:l + 1, :], gmb_ref[l:l + 1, :])
        vn_heads = jnp.concatenate([jnp.where(lane_head == hh, vn, 0.0) for hh in range(GM_HEADS)],
                                   axis=0).astype(BF16)
        ya_ref[0, rows, :] = u * (_dot(gmw_ref[...], vn_heads) + gmbias_ref[...])

    cq = z[:, COL_CQ:COL_CQ + 256]
    cqn = cq * lax.rsqrt(jnp.sum(cq * cq, axis=-1, keepdims=True) * (1.0 / MLA_Q_RANK) + RMS_EPS)
    cqn = (cqn * qn_ref[...]).astype(BF16)
    qq = _dot(cqn, wq_ref[...])
    tq_c = jnp.concatenate([tq_c_ref[...]] * MLA_HEADS, axis=1)
    tq_s = jnp.concatenate([tq_s_ref[...]] * MLA_HEADS, axis=1)
    q_ref[0] = (qq[:, :QK_PAD] * tq_c + qq[:, QK_PAD:] * tq_s).astype(BF16)

    ckv = z[:, COL_CKV:COL_CKV + MLA_KV_RANK]
    ckvn = ckv * lax.rsqrt(jnp.mean(ckv * ckv, axis=-1, keepdims=True) + RMS_EPS)
    ckvn = (ckvn * kvn_ref[...]).astype(BF16)
    krt = (z[:, COL_KR:COL_KR + 128] * tk_ref[...]).astype(BF16)
    kv = _dot(jnp.concatenate([ckvn, krt], axis=1), wkv_ref[...])
    k_ref[0] = kv[:, :QK_PAD].astype(BF16)
    one_col = (lax.broadcasted_iota(jnp.int32, (tm, HEAD_PAD - MLA_V_DIM), 1) == 0).astype(F32)
    for hh in range(MLA_HEADS):
        vh = kv[:, QK_PAD + hh * MLA_V_DIM:QK_PAD + (hh + 1) * MLA_V_DIM]
        v_ref[0, hh] = jnp.concatenate([vh, one_col], axis=1).astype(BF16)


def _inproj(x, l, pre_ln, ln_g, ln_b, mod4, w_ext, gm_g, gm_b, gm_w, gm_bias, qn, kvn, wq_ext, wkv_ext,
            tq_c, tq_s, tk):
    bsz, L, d = x.shape
    tm = 512
    per_layer = lambda a: _layer_spec(a, l)
    const = lambda shape: pl.BlockSpec(shape, lambda b, t: (0,) * len(shape))
    out_shape = (
        jax.ShapeDtypeStruct((bsz, L, GROUP_WIDTH), F32),
        jax.ShapeDtypeStruct((L, bsz * GROUP_WIDTH), F32),
        jax.ShapeDtypeStruct((bsz, L, 4 * GROUP_WIDTH), F32),
        jax.ShapeDtypeStruct((bsz, L, QK_PAD), BF16),
        jax.ShapeDtypeStruct((bsz, L, QK_PAD), BF16),
        jax.ShapeDtypeStruct((bsz, MLA_HEADS, L, HEAD_PAD), BF16),
    )
    row = lambda w: pl.BlockSpec((1, tm, w), lambda b, t: (b, t, 0))
    return pl.pallas_call(
        functools.partial(_inproj_kernel, tm=tm, pre_ln=pre_ln, l=l),
        out_shape=out_shape,
        grid=(bsz, L // tm),
        in_specs=[row(d), const((1, d)), const((1, d)), _mod_spec(mod4, l),
                  per_layer(w_ext), _whole_spec(gm_g), _whole_spec(gm_b), per_layer(gm_w), per_layer(gm_bias),
                  per_layer(qn), per_layer(kvn), per_layer(wq_ext), per_layer(wkv_ext),
                  pl.BlockSpec((tm, HEAD_PAD), lambda b, t: (t, 0)),
                  pl.BlockSpec((tm, HEAD_PAD), lambda b, t: (t, 0)),
                  pl.BlockSpec((tm, 128), lambda b, t: (t, 0))],
        out_specs=(row(GROUP_WIDTH), pl.BlockSpec((tm, GROUP_WIDTH), lambda b, t: (t, b)),
                   row(4 * GROUP_WIDTH), row(QK_PAD), row(QK_PAD),
                   pl.BlockSpec((1, MLA_HEADS, tm, HEAD_PAD), lambda b, t: (b, 0, t, 0))),
        compiler_params=_cparams(("arbitrary", "arbitrary")),
        name="inproj",
    )(x, ln_g, ln_b, mod4, w_ext, gm_g, gm_b, gm_w, gm_bias, qn, kvn, wq_ext, wkv_ext, tq_c, tq_s, tk)


ATT_TQ = 256
ATT_HEADS_PER_STEP = 2


def _attn_kernel(q_ref, k_ref, v_ref, o_ref):
    L = q_ref.shape[1]
    tq = ATT_TQ
    q_chunk = lax.broadcasted_iota(jnp.int32, (tq, tq), 0) // CHUNK
    k_chunk = lax.broadcasted_iota(jnp.int32, (tq, tq), 1) // CHUNK
    allowed = k_chunk <= q_chunk
    nq = L // tq
    order = [t for pair in zip(range(nq // 2), range(nq - 1, nq // 2 - 1, -1)) for t in pair] + (
        [nq // 2] if nq % 2 else [])
    for qi in order:
        lk = (qi + 1) * tq
        outs = []
        for hh in range(ATT_HEADS_PER_STEP):
            lanes = slice(hh * HEAD_PAD, (hh + 1) * HEAD_PAD)
            qh = q_ref[0, qi * tq:(qi + 1) * tq, lanes]
            s = _dot_nt(qh, k_ref[0, 0:lk, lanes])
            s_diag = jnp.where(allowed, s[:, lk - tq:], -1e30)
            s = s_diag if qi == 0 else jnp.concatenate([s[:, :lk - tq], s_diag], axis=1)
            m = jnp.max(s, axis=-1, keepdims=True)
            p = jnp.exp2(s - m)
            pv = _dot(p.astype(BF16), v_ref[0, hh, 0:lk, :])
            outs.append(pv[:, 0:MLA_V_DIM] / pv[:, MLA_V_DIM:MLA_V_DIM + 1])
        o_ref[0, qi * tq:(qi + 1) * tq, :] = jnp.concatenate(outs, axis=1)


def _attention(q, k, v):
    bsz, L, _ = q.shape
    hp = ATT_HEADS_PER_STEP
    return pl.pallas_call(
        _attn_kernel,
        out_shape=jax.ShapeDtypeStruct((bsz, L, GROUP_WIDTH), F32),
        grid=(bsz, MLA_HEADS // hp),
        in_specs=[pl.BlockSpec((1, L, hp * HEAD_PAD), lambda b, g: (b, 0, g)),
                  pl.BlockSpec((1, L, hp * HEAD_PAD), lambda b, g: (b, 0, g)),
                  pl.BlockSpec((1, hp, L, HEAD_PAD), lambda b, g: (b, g, 0, 0))],
        out_specs=pl.BlockSpec((1, L, hp * MLA_V_DIM), lambda b, g: (b, 0, g)),
        compiler_params=_cparams(("arbitrary", "arbitrary")),
        name="mla_attention",
    )(q, k, v)


def _s5_kernel(u_ref, bmat_ref, are_ref, aim_ref, cmat_ref, dvec_ref, wglu_ref, bglu_ref,
               o_ref, h_ref, us_ref, bu_ref, hs_ref, os_ref, *, tc, nb, l):
    @pl.when(pl.program_id(0) == 0)
    def _():
        h_ref[...] = jnp.zeros(h_ref.shape, F32)

    nslab = GROUP_WIDTH // LANES
    for bb in range(nb):
        for s in range(nslab):
            lanes = slice(bb * GROUP_WIDTH + s * LANES, bb * GROUP_WIDTH + (s + 1) * LANES)
            us_ref[s, pl.ds(bb, tc, stride=nb), :] = u_ref[:, lanes]
    u = jnp.concatenate([us_ref[s] for s in range(nslab)], axis=1)
    bu_ref[...] = _dot(u.astype(BF16), bmat_ref[...])
    a_re = jnp.broadcast_to(are_ref[...], (nb, S5_NSTATE))
    a_im = jnp.broadcast_to(aim_ref[...], (nb, S5_NSTATE))
    re = slice(0, S5_NSTATE)
    im = slice(S5_NSTATE, 2 * S5_NSTATE)

    def step(i, carry):
        h_re, h_im = carry
        r0 = pl.ds(pl.multiple_of(i * 2 * nb, 2 * nb), nb)
        r1 = pl.ds(pl.multiple_of(i * 2 * nb, 2 * nb) + nb, nb)
        m_re = a_re * h_re - a_im * h_im + bu_ref[r0, re]
        m_im = a_re * h_im + a_im * h_re + bu_ref[r0, im]
        n_re = a_re * m_re - a_im * m_im + bu_ref[r1, re]
        n_im = a_re * m_im + a_im * m_re + bu_ref[r1, im]
        rr = pl.ds(pl.multiple_of(i * 2 * nb, 2 * nb), 2 * nb)
        hs_ref[rr, re] = jnp.concatenate([m_re, n_re], axis=0).astype(BF16)
        hs_ref[rr, im] = jnp.concatenate([m_im, n_im], axis=0).astype(BF16)
        return n_re, n_im

    h_re, h_im = lax.fori_loop(0, tc // 2, step, (h_ref[:, re], h_ref[:, im]))
    h_ref[:, re] = h_re
    h_ref[:, im] = h_im

    half = (tc * nb) // 2
    y = jnp.concatenate([_dot(hs_ref[0:half, :], cmat_ref[...]), _dot(hs_ref[half:2 * half, :], cmat_ref[...])],
                        axis=0) + dvec_ref[l:l + 1, :] * u
    z = _gelu(y)
    z_bf = z.astype(BF16)
    gate = _sigmoid(jnp.concatenate([_dot(z_bf[0:half], wglu_ref[...]), _dot(z_bf[half:2 * half], wglu_ref[...])],
                                    axis=0) + bglu_ref[l:l + 1, :])
    res = z * gate
    for s in range(nslab):
        os_ref[s] = res[:, s * LANES:(s + 1) * LANES]
    for bb in range(nb):
        for s in range(nslab):
            lanes = slice(bb * GROUP_WIDTH + s * LANES, bb * GROUP_WIDTH + (s + 1) * LANES)
            o_ref[:, lanes] = os_ref[s, pl.ds(bb, tc, stride=nb), :]


def _s5(u_t, nb, l, bmat, a_re, a_im, cmat, dvec, wglu, bglu):
    L = u_t.shape[0]
    gw = GROUP_WIDTH
    tc = 128
    per_layer = lambda a: _layer_spec(a, l)
    return pl.pallas_call(
        functools.partial(_s5_kernel, tc=tc, nb=nb, l=l),
        out_shape=jax.ShapeDtypeStruct((L, nb * gw), F32),
        grid=(L // tc,),
        in_specs=[pl.BlockSpec((tc, nb * gw), lambda t: (t, 0)),
                  per_layer(bmat), per_layer(a_re), per_layer(a_im), per_layer(cmat), _whole_spec(dvec),
                  per_layer(wglu), _whole_spec(bglu)],
        out_specs=pl.BlockSpec((tc, nb * gw), lambda t: (t, 0)),
        scratch_shapes=[pltpu.VMEM((nb, 2 * S5_NSTATE), F32),
                        pltpu.VMEM((gw // LANES, tc * nb, LANES), F32),
                        pltpu.VMEM((tc * nb, 2 * S5_NSTATE), F32),
                        pltpu.VMEM((tc * nb, 2 * S5_NSTATE), BF16),
                        pltpu.VMEM((gw // LANES, tc * nb, LANES), F32)],
        compiler_params=_cparams(("arbitrary",)),
        name="s5_scan",
    )(u_t, bmat, a_re, a_im, cmat, dvec, wglu, bglu)


HG_BATCH = 2


def _hgrn_chunk(q, fl, iv, lb, tri, ones_bd, st):
    nsub = CHUNK // HG_SUB
    f = lb + (1.0 - lb) * _sigmoid(fl)
    logf = jnp.log(f)
    kk = 1.0 - f
    hi, mid, lo = _split3(logf)
    b2 = (_dot(tri, hi) + _dot(tri, mid) + _dot(tri, lo)) * LOG2E
    lane_head = lax.broadcasted_iota(jnp.int32, (HG_SUB, GROUP_WIDTH), 1) // HG_DK
    row_id = lax.broadcasted_iota(jnp.int32, (HG_SUB, GROUP_WIDTH), 0)
    v_bf = iv.astype(BF16)

    terms = []
    for si in range(nsub):
        r0 = si * HG_SUB
        q_i, k_i, b_i = (a[r0:r0 + HG_SUB] for a in (q, kk, b2))
        for j in range(HG_SUB):
            e = jnp.exp2(jnp.where(row_id >= j, b_i - b_i[j:j + 1], -1e30))
            terms.append(q_i * k_i[j:j + 1] * e)
    p = _dot(jnp.concatenate(terms, axis=0).astype(BF16), ones_bd)
    o_parts = []
    for si in range(nsub):
        r0 = si * HG_SUB
        base = si * HG_SUB * HG_SUB
        o_i = p[base:base + HG_SUB] * iv[r0:r0 + 1]
        for j in range(1, HG_SUB):
            o_i = o_i + p[base + j * HG_SUB:base + (j + 1) * HG_SUB] * iv[r0 + j:r0 + j + 1]
        o_parts.append(o_i)

    atts = []
    for si in range(1, nsub):
        r0 = si * HG_SUB
        b_edge = b2[r0 - 1:r0]
        q_t = q[r0:r0 + HG_SUB] * jnp.exp2(b2[r0:r0 + HG_SUB] - b_edge)
        k_t = jnp.concatenate([kk[0:r0] * jnp.exp2(b_edge - b2[0:r0]),
                               jnp.zeros((CHUNK - r0, GROUP_WIDTH), F32)], axis=0).astype(BF16)
        q_exp = jnp.concatenate([jnp.where(lane_head == hh, q_t, 0.0) for hh in range(HG_HEADS)],
                                axis=0).astype(BF16)
        atts.append(_dot_nt(q_exp, k_t))
    o2 = _dot(jnp.concatenate(atts, axis=0).astype(BF16), v_bf)
    for si in range(1, nsub):
        base = (si - 1) * HG_HEADS * HG_SUB
        for hh in range(HG_HEADS):
            o_parts[si] = o_parts[si] + jnp.where(lane_head == hh,
                                                  o2[base + hh * HG_SUB:base + (hh + 1) * HG_SUB], 0.0)
    o_intra = jnp.concatenate(o_parts, axis=0)

    o_inter = _dot_nt((q * jnp.exp2(b2)).astype(BF16), st.astype(BF16))
    b_last = b2[CHUNK - 1:CHUNK]
    k_hat = (kk * jnp.exp2(b_last - b2)).astype(BF16)
    upd = _dot_tn(v_bf, k_hat)
    bd = (lax.broadcasted_iota(jnp.int32, (GROUP_WIDTH, GROUP_WIDTH), 0) // HG_DK
          == lax.broadcasted_iota(jnp.int32, (GROUP_WIDTH, GROUP_WIDTH), 1) // HG_DK)
    st_new = st * jnp.exp2(b_last) + jnp.where(bd, upd, 0.0)
    return o_intra + o_inter, st_new


def _hgrn_kernel(d_ref, lb_ref, gn_ref, tri_ref, ones_ref, o_ref, st_ref, *, th, l):
    @pl.when(pl.program_id(1) == 0)
    def _():
        st_ref[...] = jnp.zeros(st_ref.shape, F32)

    lb = lb_ref[l:l + 1, :]
    tri = tri_ref[...]
    ones_bd = ones_ref[...]
    for c in range(th // CHUNK):
        rows = slice(c * CHUNK, (c + 1) * CHUNK)
        for bb in range(d_ref.shape[0]):
            q = d_ref[bb, rows, 0:GROUP_WIDTH]
            fl = d_ref[bb, rows, GROUP_WIDTH:2 * GROUP_WIDTH]
            iv = d_ref[bb, rows, 2 * GROUP_WIDTH:3 * GROUP_WIDTH]
            g = d_ref[bb, rows, 3 * GROUP_WIDTH:4 * GROUP_WIDTH]
            o, st_new = _hgrn_chunk(q, fl, iv, lb, tri, ones_bd, st_ref[bb])
            st_ref[bb] = st_new
            sq_hi, sq_lo = _split2(o * o)
            ms = (_dot(sq_hi, ones_bd) + _dot(sq_lo, ones_bd)) * (1.0 / HG_DK)
            o = o * lax.rsqrt(ms + RMS_EPS) * gn_ref[l:l + 1, :]
            o_ref[bb, rows, :] = o * _sigmoid(g)


def _hgrn(d, l, lb, gn, tri, ones_bd):
    bsz, L, _ = d.shape
    th = 256
    nbb = HG_BATCH if bsz % HG_BATCH == 0 else 1
    const = lambda shape: pl.BlockSpec(shape, lambda b, t: (0,) * len(shape))
    return pl.pallas_call(
        functools.partial(_hgrn_kernel, th=th, l=l),
        out_shape=jax.ShapeDtypeStruct((bsz, L, GROUP_WIDTH), F32),
        grid=(bsz // nbb, L // th),
        in_specs=[pl.BlockSpec((nbb, th, 4 * GROUP_WIDTH), lambda b, t: (b, t, 0)),
                  _whole_spec(lb), _whole_spec(gn),
                  const((CHUNK, CHUNK)), const((GROUP_WIDTH, GROUP_WIDTH))],
        out_specs=pl.BlockSpec((nbb, th, GROUP_WIDTH), lambda b, t: (b, t, 0)),
        scratch_shapes=[pltpu.VMEM((nbb, GROUP_WIDTH, GROUP_WIDTH), F32)],
        compiler_params=_cparams(("arbitrary", "arbitrary")),
        name="hgrn2",
    )(d, lb, gn, tri, ones_bd)


def _merge_kernel(ya_ref, yb_ref, yc_ref, yd_ref, x_ref, lng_ref, lnb_ref, mod_ref, mg_ref, wo_ref, g_ref, b_ref,
                  o_ref, xp_ref, *, pre_ln, l):
    acc = None
    for gi, y_ref in enumerate((ya_ref, yb_ref, yc_ref, yd_ref)):
        y = y_ref[...] if gi == 2 else y_ref[0]
        yn = y * lax.rsqrt(jnp.mean(y * y, axis=-1, keepdims=True) + RMS_EPS)
        yn = (yn * mg_ref[l:l + 1, gi * GROUP_WIDTH:(gi + 1) * GROUP_WIDTH]).astype(BF16)
        part = _dot(yn, wo_ref[gi * GROUP_WIDTH:(gi + 1) * GROUP_WIDTH, :])
        acc = part if acc is None else acc + part
    g1 = mod_ref[0, :, 2 * D_MODEL:3 * D_MODEL]
    x = x_ref[0]
    if pre_ln:
        x = _layer_norm(x, lng_ref[...], lnb_ref[...])
    x1 = _layer_norm(DN_ALPHA * x + (1.0 + g1) * acc, g_ref[l:l + 1, :], b_ref[l:l + 1, :])
    seg = FF_ROWS // FF_SEGS
    nslab = D_MODEL // LANES
    for s in range(nslab):
        for p in range(FF_SEGS):
            xp_ref[s, pl.ds(p, seg, stride=FF_SEGS), :] = x1[p * seg:(p + 1) * seg, s * LANES:(s + 1) * LANES]
    o_ref[0] = jnp.concatenate([xp_ref[s] for s in range(nslab)], axis=1)


def _merge(ya, yb, yc, yd, x, l, pre_ln, ln_g, ln_b, mod4, mg, wo, g, b):
    bsz, L, d = x.shape
    tm = FF_ROWS
    per_layer = lambda a: _layer_spec(a, l)
    const = lambda shape: pl.BlockSpec(shape, lambda bb, t: (0,) * len(shape))
    row = lambda w: pl.BlockSpec((1, tm, w), lambda bb, t: (bb, t, 0))
    return pl.pallas_call(
        functools.partial(_merge_kernel, pre_ln=pre_ln, l=l),
        out_shape=jax.ShapeDtypeStruct((bsz, L, d), F32),
        grid=(bsz, L // tm),
        in_specs=[row(GROUP_WIDTH), row(GROUP_WIDTH), pl.BlockSpec((tm, GROUP_WIDTH), lambda bb, t: (t, bb)),
                  row(GROUP_WIDTH), row(d), const((1, d)), const((1, d)), _mod_spec(mod4, l),
                  _whole_spec(mg), per_layer(wo), _whole_spec(g), _whole_spec(b)],
        out_specs=row(d),
        scratch_shapes=[pltpu.VMEM((d // LANES, tm, LANES), F32)],
        compiler_params=_cparams(("arbitrary", "arbitrary")),
        name="merge_out",
    )(ya, yb, yc, yd, x, ln_g, ln_b, mod4, mg, wo, g, b)


def _ffn_kernel(x_ref, xh_ref, mod_ref, wup_ref, cw_ref, cb_ref, wdn_ref, g_ref, b_ref, o_ref,
                hx_ref, a_ref, act_ref, ap_ref, *, tm, l):
    ti = pl.program_id(1)
    seg = tm // FF_SEGS
    nslab = D_MODEL // LANES
    sh2 = mod_ref[0, :, 3 * D_MODEL:4 * D_MODEL]
    sc2 = mod_ref[0, :, 4 * D_MODEL:5 * D_MODEL]
    g2 = mod_ref[0, :, 5 * D_MODEL:6 * D_MODEL]
    x = x_ref[0]
    halo = jnp.where(ti > 0, xh_ref[0] * (1.0 + sc2) + sh2, 0.0)
    hx_ref[0:FF_HALO, :] = halo.astype(BF16)
    hx_ref[FF_HALO:FF_HALO + tm, :] = (x * (1.0 + sc2) + sh2).astype(BF16)
    nj = D_FF // FF_TILE
    sub0 = lax.broadcasted_iota(jnp.int32, (FF_SEGS, FF_TILE), 0) == 0

    def up(j):
        hx = hx_ref[...]
        for half in range(2):
            a = a_ref.at[j % 2, half]
            cols = slice(half * D_FF + j * FF_TILE, half * D_FF + (j + 1) * FF_TILE)
            a[...] = _dot(hx, wup_ref[:, cols])
            prev1 = jnp.where(sub0, a[FF_HALO - 1:FF_HALO, :], pltpu.roll(a[FF_HALO + tm - 8:FF_HALO + tm, :], 1, 0))
            prev2 = jnp.where(sub0, a[FF_HALO - 9:FF_HALO - 8, :],
                              pltpu.roll(a[FF_HALO + tm - 16:FF_HALO + tm - 8, :], 1, 0))
            a[8:16, :] = prev1
            a[0:8, :] = prev2

    def conv(j, half):
        cols = slice(half * D_FF + j * FF_TILE, half * D_FF + (j + 1) * FF_TILE)
        a = a_ref.at[j % 2, half]
        return (a[0:tm, :] * cw_ref[0:1, cols] + a[8:tm + 8, :] * cw_ref[1:2, cols]
                + a[16:tm + 16, :] * cw_ref[2:3, cols] + cb_ref[l:l + 1, cols])

    up(0)
    acc = None
    for j in range(nj):
        if j + 1 < nj:
            up(j + 1)
        gate = conv(j, 1)
        act_ref[:, j * FF_TILE:(j + 1) * FF_TILE] = (conv(j, 0) * (gate * _sigmoid(gate))).astype(BF16)
        if j + 1 in FF_DOWN_SPLITS:
            k0 = ([0] + [e for e in FF_DOWN_SPLITS if e <= j])[-1] * FF_TILE
            k1 = (j + 1) * FF_TILE
            part = _dot(act_ref[:, k0:k1], wdn_ref[k0:k1, :])
            acc = part if acc is None else acc + part
    out = _layer_norm(DN_ALPHA * x + (1.0 + g2) * acc, g_ref[l:l + 1, :], b_ref[l:l + 1, :])
    for s in range(nslab):
        ap_ref[s] = out[:, s * LANES:(s + 1) * LANES]
    for s in range(nslab):
        for p in range(FF_SEGS):
            o_ref[0, p * seg:(p + 1) * seg, s * LANES:(s + 1) * LANES] = ap_ref[s, pl.ds(p, seg, stride=FF_SEGS), :]


def _ffn(x, l, mod4, wup, cw, cb, wdn, g, b):
    bsz, L, d = x.shape
    tm = FF_ROWS
    per_layer = lambda a: _layer_spec(a, l, pipeline_mode=pl.Buffered(1))
    hb = tm // FF_HALO
    return pl.pallas_call(
        functools.partial(_ffn_kernel, tm=tm, l=l),
        out_shape=jax.ShapeDtypeStruct((bsz, L, d), F32),
        grid=(bsz, L // tm),
        in_specs=[pl.BlockSpec((1, tm, d), lambda bb, t: (bb, t, 0)),
                  pl.BlockSpec((1, FF_HALO, d), lambda bb, t: (bb, jnp.maximum(t * hb - 1, 0), 0)),
                  _mod_spec(mod4, l),
                  per_layer(wup), per_layer(cw), _whole_spec(cb), per_layer(wdn), _whole_spec(g), _whole_spec(b)],
        out_specs=pl.BlockSpec((1, tm, d), lambda bb, t: (bb, t, 0)),
        scratch_shapes=[pltpu.VMEM((tm + FF_HALO, d), BF16),
                        pltpu.VMEM((2, 2, tm + FF_HALO, FF_TILE), F32),
                        pltpu.VMEM((tm, D_FF), BF16),
                        pltpu.VMEM((d // LANES, tm, LANES), F32)],
        compiler_params=_cparams(("arbitrary", "arbitrary")),
        name="conv_ffn",
    )(x, x, mod4, wup, cw, cb, wdn, g, b)


def _rope_tables(L):
    inv = 1.0 / (ROPE_THETA ** (jnp.arange(0, MLA_ROPE_DIM, 2, dtype=F32) / MLA_ROPE_DIM))
    ang = jnp.arange(L, dtype=F32)[:, None] * inv[None, :]
    cos, sin = jnp.cos(ang), jnp.sin(ang)
    cc = jnp.concatenate([cos, cos], axis=1)
    ss = jnp.concatenate([-sin, sin], axis=1)
    scale = MLA_QK_DIM ** -0.5 * LOG2E
    zeros32 = jnp.zeros((L, HEAD_PAD - MLA_QK_DIM), F32)
    tq_c = jnp.concatenate([jnp.ones((L, MLA_NOPE_DIM), F32), cc, zeros32], axis=1) * scale
    tq_s = jnp.concatenate([jnp.zeros((L, MLA_NOPE_DIM), F32), ss, zeros32], axis=1) * scale
    tk = jnp.concatenate([cc, ss, jnp.zeros((L, 64), F32)], axis=1)
    return tq_c, tq_s, tk


def _swap_halves(w):
    half = MLA_ROPE_DIM // 2
    return jnp.concatenate([w[..., half:], w[..., :half]], axis=-1)


def _prep_inproj_weights(w_in_l, w_uq_l, w_ukv_l, q_norm_l, kv_norm_l):
    d = w_in_l.shape[0]
    a_u, a_v, cq, ckv, kr, c_u, d_q, d_f, d_i, d_g = jnp.split(
        w_in_l, (256, 512, 704, 832, 864, 1120, 1376, 1632, 1888), axis=1)
    z64 = jnp.zeros((d, 64), F32)
    w_ext = jnp.concatenate([a_u, a_v, c_u, d_q, d_f, d_i, d_g, cq, z64, ckv, kr, _swap_halves(kr), z64],
                            axis=1).astype(BF16)
    wq = w_uq_l.reshape(MLA_Q_RANK, MLA_HEADS, MLA_QK_DIM)
    zq32 = jnp.zeros((MLA_Q_RANK, MLA_HEADS, HEAD_PAD - MLA_QK_DIM), F32)
    zq64 = jnp.zeros((MLA_Q_RANK, MLA_HEADS, MLA_NOPE_DIM), F32)
    wq_a = jnp.concatenate([wq, zq32], axis=2).reshape(MLA_Q_RANK, QK_PAD)
    wq_b = jnp.concatenate([zq64, _swap_halves(wq[:, :, MLA_NOPE_DIM:]), zq32], axis=2).reshape(MLA_Q_RANK, QK_PAD)
    wq_ext = jnp.concatenate([wq_a, wq_b], axis=1)
    wq_ext = jnp.concatenate([wq_ext, jnp.zeros((256 - MLA_Q_RANK, 2 * QK_PAD), F32)], axis=0).astype(BF16)
    qn = jnp.concatenate([q_norm_l, jnp.zeros((256 - MLA_Q_RANK,), F32)]).reshape(1, 256)
    wkv = w_ukv_l.reshape(MLA_KV_RANK, MLA_HEADS, MLA_NOPE_DIM + MLA_V_DIM)
    zk64 = jnp.zeros((MLA_KV_RANK, MLA_HEADS, HEAD_PAD - MLA_NOPE_DIM), F32)
    wk = jnp.concatenate([wkv[:, :, :MLA_NOPE_DIM], zk64], axis=2).reshape(MLA_KV_RANK, QK_PAD)
    wv = wkv[:, :, MLA_NOPE_DIM:].reshape(MLA_KV_RANK, MLA_HEADS * MLA_V_DIM)
    eye = jnp.eye(MLA_ROPE_DIM, dtype=F32)
    place_h = jnp.concatenate([jnp.zeros((MLA_ROPE_DIM, MLA_NOPE_DIM), F32), eye,
                               jnp.zeros((MLA_ROPE_DIM, HEAD_PAD - MLA_QK_DIM), F32)], axis=1)
    place = jnp.concatenate([place_h] * MLA_HEADS, axis=1)
    place = jnp.concatenate([place, jnp.zeros((MLA_ROPE_DIM, MLA_HEADS * MLA_V_DIM), F32)], axis=1)
    top = jnp.concatenate([wk, wv], axis=1)
    wkv_ext = jnp.concatenate([top, place, place, jnp.zeros((64, QK_PAD + 256), F32)], axis=0).astype(BF16)
    kvn = kv_norm_l.reshape(1, MLA_KV_RANK)
    return w_ext, wq_ext, wkv_ext, qn, kvn


def _prep_gm(gm_ws_l, gm_bs_l):
    ck = jnp.arange(GM_BLOCK) // CHUNK
    mask = ck[None, :] <= ck[:, None]
    w = jnp.where(mask[None], gm_ws_l, 0)
    w = jnp.transpose(w, (1, 0, 2)).reshape(GM_BLOCK, GM_HEADS * GM_BLOCK).astype(BF16)
    bias = jnp.repeat(gm_bs_l.T, GROUP_WIDTH // GM_HEADS, axis=1)
    return w, bias


def _prep_s5(a_re, a_im, b_re, b_im, c_re, c_im, log_step):
    dt = jnp.exp(log_step)[:, None]
    mag = jnp.exp(dt * a_re)
    abar_re, abar_im = mag * jnp.cos(dt * a_im), mag * jnp.sin(dt * a_im)
    den = a_re * a_re + a_im * a_im
    nr, ni = abar_re - 1.0, abar_im
    zoh_re = (nr * a_re + ni * a_im) / den
    zoh_im = (ni * a_re - nr * a_im) / den
    bbar_re = zoh_re[..., None] * b_re - zoh_im[..., None] * b_im
    bbar_im = zoh_re[..., None] * b_im + zoh_im[..., None] * b_re
    eye_g = jnp.eye(S5_GROUPS, dtype=F32)
    bm_re = jnp.einsum('gpn,gh->gnhp', bbar_re, eye_g).reshape(GROUP_WIDTH, S5_NSTATE)
    bm_im = jnp.einsum('gpn,gh->gnhp', bbar_im, eye_g).reshape(GROUP_WIDTH, S5_NSTATE)
    bmat = jnp.concatenate([bm_re, bm_im], axis=1).astype(BF16)
    cm_re = jnp.einsum('gnp,gh->gphn', c_re, eye_g).reshape(S5_NSTATE, GROUP_WIDTH)
    cm_im = jnp.einsum('gnp,gh->gphn', c_im, eye_g).reshape(S5_NSTATE, GROUP_WIDTH)
    cmat = jnp.concatenate([cm_re, -cm_im], axis=0).astype(BF16)
    return bmat, abar_re.reshape(1, S5_NSTATE), abar_im.reshape(1, S5_NSTATE), cmat


def kernel(x, c, ln_in_g, ln_in_b, hg_lb_logits, w_ada, b_ada, w_in, gm_ln_g, gm_ln_b, gm_ws, gm_bs, mla_q_norm, mla_kv_norm, mla_w_uq, mla_w_ukv, s5_a_re, s5_a_im, s5_b_re, s5_b_im, s5_c_re, s5_c_im, s5_d, s5_log_step, s5_w_glu, s5_b_glu, hg_norm_g, merge_g, w_out, ln1_g, ln1_b, w_up, conv_w, conv_b, w_down, ln2_g, ln2_b):
    bsz, L, d = x.shape
    nl = w_in.shape[0]
    mod = _ada(c, w_ada, b_ada)
    lb_cum = jnp.cumsum(jax.nn.softmax(hg_lb_logits.astype(F32), axis=0), axis=0)
    tq_c, tq_s, tk = _rope_tables(L)
    tri = jnp.tril(jnp.ones((CHUNK, CHUNK), F32)).astype(BF16)
    hid = jnp.arange(GROUP_WIDTH) // HG_DK
    ones_bd = (hid[:, None] == hid[None, :]).astype(BF16)
    w_ext, wq_ext, wkv_ext, qn, kvn = jax.vmap(_prep_inproj_weights)(w_in, mla_w_uq, mla_w_ukv, mla_q_norm,
                                                                     mla_kv_norm)
    gm_w, gm_bias = jax.vmap(_prep_gm)(gm_ws, gm_bs)
    bmat, a_re, a_im, cmat = jax.vmap(_prep_s5)(s5_a_re, s5_a_im, s5_b_re, s5_b_im, s5_c_re, s5_c_im, s5_log_step)
    wglu_bf, wout_bf, wup_bf, wdn_bf = (w.astype(BF16) for w in (s5_w_glu, w_out, w_up, w_down))

    mod4 = mod.reshape(nl, bsz, 1, 6 * d)
    lbs = lb_cum - lb_cum[0:1]

    xs = x
    ln_g, ln_b = ln_in_g.reshape(1, d), ln_in_b.reshape(1, d)
    for l in range(nl):
        pre_ln = l == 0
        y_a, c_u, dd, q, k, v = _inproj(xs, l, pre_ln, ln_g, ln_b, mod4, w_ext, gm_ln_g, gm_ln_b, gm_w, gm_bias,
                                        qn, kvn, wq_ext, wkv_ext, tq_c, tq_s, tk)
        y_b = _attention(q, k, v)
        y_c = _s5(c_u, bsz, l, bmat, a_re, a_im, cmat, s5_d, wglu_bf, s5_b_glu)
        y_d = _hgrn(dd, l, lbs, hg_norm_g, tri, ones_bd)
        xs = _merge(y_a, y_b, y_c, y_d, xs, l, pre_ln, ln_g, ln_b, mod4, merge_g, wout_bf, ln1_g, ln1_b)
        xs = _ffn(xs, l, mod4, wup_bf, conv_w, conv_b, wdn_bf, ln2_g, ln2_b)
    return xs
```

```python
import functools
import math

import jax
import jax.numpy as jnp
from jax import lax
from jax.experimental import pallas as pl
from jax.experimental.pallas import tpu as pltpu

F32 = jnp.float32
BF16 = jnp.bfloat16

D_MODEL = 1024
DEPTH = 2
CHUNK = 64
GROUP_WIDTH = 256
N_MIXERS = 4
GM_HEADS = 4
GM_BLOCK = 128
MLA_HEADS = 4
MLA_NOPE_DIM = 64
MLA_ROPE_DIM = 32
MLA_QK_DIM = MLA_NOPE_DIM + MLA_ROPE_DIM
MLA_V_DIM = 64
MLA_Q_RANK = 192
MLA_KV_RANK = 128
ROPE_THETA = 10000.0
S5_GROUP = 16
S5_GROUPS = 16
S5_STATE = 64
HG_HEADS = 4
HG_DK = 64
D_FF = 2816
CONV_W = 3
DN_ALPHA = (2 * DEPTH) ** 0.25
LN_EPS = 1e-5
RMS_EPS = 1e-6
LOG2E = 1.4426950408889634

LANES = 128
HEAD_PAD = 128
QK_PAD = MLA_HEADS * HEAD_PAD
S5_NSTATE = S5_GROUPS * S5_STATE
HG_SUB = 8

COL_AU, COL_AV, COL_CU, COL_DQ = 0, 256, 512, 768
COL_CQ, COL_CKV, COL_KR = 1792, 2048, 2176
IN_EXT = 2304

FF_TILE = 256
FF_HALO = 16
FF_ROWS = 512
FF_SEGS = 8
FF_DOWN_SPLITS = (4, 8, 10, 11)

VMEM_LIMIT = 56 * 1024 * 1024


def _sigmoid(x):
    return 1.0 / (1.0 + jnp.exp(-x))


def _gelu(x):
    return 0.5 * x * (1.0 + jnp.tanh(math.sqrt(2.0 / math.pi) * (x + 0.044715 * (x * x * x))))


def _split2(x):
    hi = x.astype(BF16)
    lo = (x - hi.astype(F32)).astype(BF16)
    return hi, lo


def _dot(a, b):
    return jnp.dot(a, b, preferred_element_type=F32)


def _dot_nt(a, b):
    return lax.dot_general(a, b, (((1,), (1,)), ((), ())), preferred_element_type=F32)


def _dot_tn(a, b):
    return lax.dot_general(a, b, (((0,), (0,)), ((), ())), preferred_element_type=F32)


def _layer_norm(x, g, b):
    mu = jnp.mean(x, axis=-1, keepdims=True)
    xc = x - mu
    var = jnp.mean(xc * xc, axis=-1, keepdims=True)
    return xc * lax.rsqrt(var + LN_EPS) * g + b


def _layer_spec(stacked, l, **kwargs):
    shape = stacked.shape[1:]
    return pl.BlockSpec((None,) + shape, lambda *_: (l,) + (0,) * len(shape), **kwargs)


def _whole_spec(arr):
    return pl.BlockSpec(arr.shape, lambda *_: (0,) * arr.ndim)


def _mod_spec(mod4, l):
    return pl.BlockSpec((None, 1, 1, mod4.shape[-1]), lambda b, *_: (l, b, 0, 0))


def _cparams(sem, vmem=VMEM_LIMIT):
    return pltpu.CompilerParams(dimension_semantics=sem, vmem_limit_bytes=vmem)


def _ada_kernel(c_ref, w_ref, b_ref, o_ref):
    c = c_ref[...]
    chi, clo = _split2(c * _sigmoid(c))
    whi, wlo = _split2(w_ref[0])
    acc = _dot(chi, whi) + _dot(clo, whi) + _dot(chi, wlo)
    o_ref[0] = acc + b_ref[0]


def _ada(c, w_ada, b_ada):
    nl, d, n = w_ada.shape
    tn = 1536
    bsz = c.shape[0]
    return pl.pallas_call(
        _ada_kernel,
        out_shape=jax.ShapeDtypeStruct((nl, bsz, n), F32),
        grid=(nl, n // tn),
        in_specs=[pl.BlockSpec((bsz, d), lambda l, j: (0, 0)),
                  pl.BlockSpec((1, d, tn), lambda l, j: (l, 0, j)),
                  pl.BlockSpec((1, 1, tn), lambda l, j: (l, 0, j))],
        out_specs=pl.BlockSpec((1, bsz, tn), lambda l, j: (l, 0, j)),
        compiler_params=_cparams(("arbitrary", "arbitrary")),
        name="ada_mod",
    )(c, w_ada, b_ada.reshape(nl, 1, n))


def _inproj_kernel(x_ref, lng_ref, lnb_ref, mod_ref, w_ref, gmg_ref, gmb_ref, gmw_ref, gmbias_ref,
                   qn_ref, kvn_ref, wq_ref, wkv_ref, tq_c_ref, tq_s_ref, tk_ref,
                   ya_ref, cu_ref, d_ref, q_ref, k_ref, v_ref, *, tm, pre_ln, l):
    x = x_ref[0]
    if pre_ln:
        x = _layer_norm(x, lng_ref[...], lnb_ref[...])
    sh1 = mod_ref[0, :, 0:D_MODEL]
    sc1 = mod_ref[0, :, D_MODEL:2 * D_MODEL]
    h = (x * (1.0 + sc1) + sh1).astype(BF16)
    z = _dot(h, w_ref[...])

    cu_ref[...] = z[:, COL_CU:COL_CU + GROUP_WIDTH]
    d_ref[0] = z[:, COL_DQ:COL_DQ + 4 * GROUP_WIDTH]

    lane_head = lax.broadcasted_iota(jnp.int32, (GM_BLOCK, GROUP_WIDTH), 1) // (GROUP_WIDTH // GM_HEADS)
    for r in range(tm // GM_BLOCK):
        rows = slice(r * GM_BLOCK, (r + 1) * GM_BLOCK)
        u = _gelu(z[rows, COL_AU:COL_AU + GROUP_WIDTH])
        v = _gelu(z[rows, COL_AV:COL_AV + GROUP_WIDTH])
        vn = _layer_norm(v, gmg_ref[l:l + 1, :], gmb_ref[l:l + 1, :])
        vn_heads = jnp.concatenate([jnp.where(lane_head == hh, vn, 0.0) for hh in range(GM_HEADS)],
                                   axis=0).astype(BF16)
        ya_ref[0, rows, :] = u * (_dot(gmw_ref[...], vn_heads) + gmbias_ref[...])

    cq = z[:, COL_CQ:COL_CQ + 256]
    cqn = cq * lax.rsqrt(jnp.sum(cq * cq, axis=-1, keepdims=True) * (1.0 / MLA_Q_RANK) + RMS_EPS)
    cqn = (cqn * qn_ref[...]).astype(BF16)
    qq = _dot(cqn, wq_ref[...])
    tq_c = jnp.concatenate([tq_c_ref[...]] * MLA_HEADS, axis=1)
    tq_s = jnp.concatenate([tq_s_ref[...]] * MLA_HEADS, axis=1)
    q_ref[0] = (qq[:, :QK_PAD] * tq_c + qq[:, QK_PAD:] * tq_s).astype(BF16)

    ckv = z[:, COL_CKV:COL_CKV + MLA_KV_RANK]
    ckvn = ckv * lax.rsqrt(jnp.mean(ckv * ckv, axis=-1, keepdims=True) + RMS_EPS)
    ckvn = (ckvn * kvn_ref[...]).astype(BF16)
    krt = (z[:, COL_KR:COL_KR + 128] * tk_ref[...]).astype(BF16)
    kv = _dot(jnp.concatenate([ckvn, krt], axis=1), wkv_ref[...])
    k_ref[0] = kv[:, :QK_PAD].astype(BF16)
    one_col = (lax.broadcasted_iota(jnp.int32, (tm, HEAD_PAD - MLA_V_DIM), 1) == 0).astype(F32)
    for hh in range(MLA_HEADS):
        vh = kv[:, QK_PAD + hh * MLA_V_DIM:QK_PAD + (hh + 1) * MLA_V_DIM]
        v_ref[0, hh] = jnp.concatenate([vh, one_col], axis=1).astype(BF16)


def _inproj(x, l, pre_ln, ln_g, ln_b, mod4, w_ext, gm_g, gm_b, gm_w, gm_bias, qn, kvn, wq_ext, wkv_ext,
            tq_c, tq_s, tk):
    bsz, L, d = x.shape
    tm = 512
    per_layer = lambda a: _layer_spec(a, l)
    const = lambda shape: pl.BlockSpec(shape, lambda b, t: (0,) * len(shape))
    out_shape = (
        jax.ShapeDtypeStruct((bsz, L, GROUP_WIDTH), F32),
        jax.ShapeDtypeStruct((L, bsz * GROUP_WIDTH), F32),
        jax.ShapeDtypeStruct((bsz, L, 4 * GROUP_WIDTH), F32),
        jax.ShapeDtypeStruct((bsz, L, QK_PAD), BF16),
        jax.ShapeDtypeStruct((bsz, L, QK_PAD), BF16),
        jax.ShapeDtypeStruct((bsz, MLA_HEADS, L, HEAD_PAD), BF16),
    )
    row = lambda w: pl.BlockSpec((1, tm, w), lambda b, t: (b, t, 0))
    return pl.pallas_call(
        functools.partial(_inproj_kernel, tm=tm, pre_ln=pre_ln, l=l),
        out_shape=out_shape,
        grid=(bsz, L // tm),
        in_specs=[row(d), const((1, d)), const((1, d)), _mod_spec(mod4, l),
                  per_layer(w_ext), _whole_spec(gm_g), _whole_spec(gm_b), per_layer(gm_w), per_layer(gm_bias),
                  per_layer(qn), per_layer(kvn), per_layer(wq_ext), per_layer(wkv_ext),
                  pl.BlockSpec((tm, HEAD_PAD), lambda b, t: (t, 0)),
                  pl.BlockSpec((tm, HEAD_PAD), lambda b, t: (t, 0)),
                  pl.BlockSpec((tm, 128), lambda b, t: (t, 0))],
        out_specs=(row(GROUP_WIDTH), pl.BlockSpec((tm, GROUP_WIDTH), lambda b, t: (t, b)),
                   row(4 * GROUP_WIDTH), row(QK_PAD), row(QK_PAD),
                   pl.BlockSpec((1, MLA_HEADS, tm, HEAD_PAD), lambda b, t: (b, 0, t, 0))),
        compiler_params=_cparams(("arbitrary", "arbitrary")),
        name="inproj",
    )(x, ln_g, ln_b, mod4, w_ext, gm_g, gm_b, gm_w, gm_bias, qn, kvn, wq_ext, wkv_ext, tq_c, tq_s, tk)


ATT_TQ = 256
ATT_HEADS_PER_STEP = 2


def _attn_kernel(q_ref, k_ref, v_ref, o_ref):
    L = q_ref.shape[1]
    tq = ATT_TQ
    q_chunk = lax.broadcasted_iota(jnp.int32, (tq, tq), 0) // CHUNK
    k_chunk = lax.broadcasted_iota(jnp.int32, (tq, tq), 1) // CHUNK
    allowed = k_chunk <= q_chunk
    nq = L // tq
    order = [t for pair in zip(range(nq // 2), range(nq - 1, nq // 2 - 1, -1)) for t in pair] + (
        [nq // 2] if nq % 2 else [])
    for qi in order:
        lk = (qi + 1) * tq
        outs = []
        for hh in range(ATT_HEADS_PER_STEP):
            lanes = slice(hh * HEAD_PAD, (hh + 1) * HEAD_PAD)
            qh = q_ref[0, qi * tq:(qi + 1) * tq, lanes]
            s = _dot_nt(qh, k_ref[0, 0:lk, lanes])
            s_diag = jnp.where(allowed, s[:, lk - tq:], -1e30)
            s = s_diag if qi == 0 else jnp.concatenate([s[:, :lk - tq], s_diag], axis=1)
            m = jnp.max(s, axis=-1, keepdims=True)
            p = jnp.exp2(s - m)
            pv = _dot(p.astype(BF16), v_ref[0, hh, 0:lk, :])
            outs.append(pv[:, 0:MLA_V_DIM] / pv[:, MLA_V_DIM:MLA_V_DIM + 1])
        o_ref[0, qi * tq:(qi + 1) * tq, :] = jnp.concatenate(outs, axis=1)


def _attention(q, k, v):
    bsz, L, _ = q.shape
    hp = ATT_HEADS_PER_STEP
    return pl.pallas_call(
        _attn_kernel,
        out_shape=jax.ShapeDtypeStruct((bsz, L, GROUP_WIDTH), F32),
        grid=(bsz, MLA_HEADS // hp),
        in_specs=[pl.BlockSpec((1, L, hp * HEAD_PAD), lambda b, g: (b, 0, g)),
                  pl.BlockSpec((1, L, hp * HEAD_PAD), lambda b, g: (b, 0, g)),
                  pl.BlockSpec((1, hp, L, HEAD_PAD), lambda b, g: (b, g, 0, 0))],
        out_specs=pl.BlockSpec((1, L, hp * MLA_V_DIM), lambda b, g: (b, 0, g)),
        compiler_params=_cparams(("arbitrary", "arbitrary")),
        name="mla_attention",
    )(q, k, v)


def _s5_kernel(u_ref, bmat_ref, are_ref, aim_ref, cmat_ref, dvec_ref, wglu_ref, bglu_ref,
               o_ref, h_ref, us_ref, bu_ref, hs_ref, os_ref, *, tc, nb, l):
    @pl.when(pl.program_id(0) == 0)
    def _():
        h_ref[...] = jnp.zeros(h_ref.shape, F32)

    nslab = GROUP_WIDTH // LANES
    for bb in range(nb):
        for s in range(nslab):
            lanes = slice(bb * GROUP_WIDTH + s * LANES, bb * GROUP_WIDTH + (s + 1) * LANES)
            us_ref[s, pl.ds(bb, tc, stride=nb), :] = u_ref[:, lanes]
    u = jnp.concatenate([us_ref[s] for s in range(nslab)], axis=1)
    bu_ref[...] = _dot(u.astype(BF16), bmat_ref[...])
    a_re = jnp.broadcast_to(are_ref[...], (nb, S5_NSTATE))
    a_im = jnp.broadcast_to(aim_ref[...], (nb, S5_NSTATE))
    re = slice(0, S5_NSTATE)
    im = slice(S5_NSTATE, 2 * S5_NSTATE)

    def step(i, carry):
        h_re, h_im = carry
        r0 = pl.ds(pl.multiple_of(i * 2 * nb, 2 * nb), nb)
        r1 = pl.ds(pl.multiple_of(i * 2 * nb, 2 * nb) + nb, nb)
        m_re = a_re * h_re - a_im * h_im + bu_ref[r0, re]
        m_im = a_re * h_im + a_im * h_re + bu_ref[r0, im]
        n_re = a_re * m_re - a_im * m_im + bu_ref[r1, re]
        n_im = a_re * m_im + a_im * m_re + bu_ref[r1, im]
        rr = pl.ds(pl.multiple_of(i * 2 * nb, 2 * nb), 2 * nb)
        hs_ref[rr, re] = jnp.concatenate([m_re, n_re], axis=0).astype(BF16)
        hs_ref[rr, im] = jnp.concatenate([m_im, n_im], axis=0).astype(BF16)
        return n_re, n_im

    h_re, h_im = lax.fori_loop(0, tc // 2, step, (h_ref[:, re], h_ref[:, im]))
    h_ref[:, re] = h_re
    h_ref[:, im] = h_im

    half = (tc * nb) // 2
    y = jnp.concatenate([_dot(hs_ref[0:half, :], cmat_ref[...]), _dot(hs_ref[half:2 * half, :], cmat_ref[...])],
                        axis=0) + dvec_ref[l:l + 1, :] * u
    z = _gelu(y)
    z_bf = z.astype(BF16)
    gate = _sigmoid(jnp.concatenate([_dot(z_bf[0:half], wglu_ref[...]), _dot(z_bf[half:2 * half], wglu_ref[...])],
                                    axis=0) + bglu_ref[l:l + 1, :])
    res = z * gate
    for s in range(nslab):
        os_ref[s] = res[:, s * LANES:(s + 1) * LANES]
    for bb in range(nb):
        for s in range(nslab):
            lanes = slice(bb * GROUP_WIDTH + s * LANES, bb * GROUP_WIDTH + (s + 1) * LANES)
            o_ref[:, lanes] = os_ref[s, pl.ds(bb, tc, stride=nb), :]


def _s5(u_t, nb, l, bmat, a_re, a_im, cmat, dvec, wglu, bglu):
    L = u_t.shape[0]
    gw = GROUP_WIDTH
    tc = 256
    per_layer = lambda a: _layer_spec(a, l)
    return pl.pallas_call(
        functools.partial(_s5_kernel, tc=tc, nb=nb, l=l),
        out_shape=jax.ShapeDtypeStruct((L, nb * gw), F32),
        grid=(L // tc,),
        in_specs=[pl.BlockSpec((tc, nb * gw), lambda t: (t, 0)),
                  per_layer(bmat), per_layer(a_re), per_layer(a_im), per_layer(cmat), _whole_spec(dvec),
                  per_layer(wglu), _whole_spec(bglu)],
        out_specs=pl.BlockSpec((tc, nb * gw), lambda t: (t, 0)),
        scratch_shapes=[pltpu.VMEM((nb, 2 * S5_NSTATE), F32),
                        pltpu.VMEM((gw // LANES, tc * nb, LANES), F32),
                        pltpu.VMEM((tc * nb, 2 * S5_NSTATE), F32),
                        pltpu.VMEM((tc * nb, 2 * S5_NSTATE), BF16),
                        pltpu.VMEM((gw // LANES, tc * nb, LANES), F32)],
        compiler_params=_cparams(("arbitrary",)),
        name="s5_scan",
    )(u_t, bmat, a_re, a_im, cmat, dvec, wglu, bglu)


HG_BATCH = 2


def _hgrn_chunk(q, fl, iv, lb, tri, ones_bd, st):
    nsub = CHUNK // HG_SUB
    f = lb + (1.0 - lb) * _sigmoid(fl)
    logf = jnp.log(f)
    kk = 1.0 - f
    hi, lo = _split2(logf)
    b2 = (_dot(tri, hi) + _dot(tri, lo)) * LOG2E
    lane_head = lax.broadcasted_iota(jnp.int32, (HG_SUB, GROUP_WIDTH), 1) // HG_DK
    row_id = lax.broadcasted_iota(jnp.int32, (HG_SUB, GROUP_WIDTH), 0)
    v_bf = iv.astype(BF16)

    terms = []
    for si in range(nsub):
        r0 = si * HG_SUB
        q_i, k_i, b_i = (a[r0:r0 + HG_SUB] for a in (q, kk, b2))
        for j in range(HG_SUB):
            e = jnp.exp2(jnp.where(row_id >= j, b_i - b_i[j:j + 1], -1e30))
            terms.append(q_i * k_i[j:j + 1] * e)
    p = _dot(jnp.concatenate(terms, axis=0).astype(BF16), ones_bd)
    o_parts = []
    for si in range(nsub):
        r0 = si * HG_SUB
        base = si * HG_SUB * HG_SUB
        o_i = p[base:base + HG_SUB] * iv[r0:r0 + 1]
        for j in range(1, HG_SUB):
            o_i = o_i + p[base + j * HG_SUB:base + (j + 1) * HG_SUB] * iv[r0 + j:r0 + j + 1]
        o_parts.append(o_i)

    atts = []
    for si in range(1, nsub):
        r0 = si * HG_SUB
        b_edge = b2[r0 - 1:r0]
        q_t = q[r0:r0 + HG_SUB] * jnp.exp2(b2[r0:r0 + HG_SUB] - b_edge)
        k_t = jnp.concatenate([kk[0:r0] * jnp.exp2(b_edge - b2[0:r0]),
                               jnp.zeros((CHUNK - r0, GROUP_WIDTH), F32)], axis=0).astype(BF16)
        q_exp = jnp.concatenate([jnp.where(lane_head == hh, q_t, 0.0) for hh in range(HG_HEADS)],
                                axis=0).astype(BF16)
        atts.append(_dot_nt(q_exp, k_t))
    o2 = _dot(jnp.concatenate(atts, axis=0).astype(BF16), v_bf)
    for si in range(1, nsub):
        base = (si - 1) * HG_HEADS * HG_SUB
        for hh in range(HG_HEADS):
            o_parts[si] = o_parts[si] + jnp.where(lane_head == hh,
                                                  o2[base + hh * HG_SUB:base + (hh + 1) * HG_SUB], 0.0)
    o_intra = jnp.concatenate(o_parts, axis=0)

    o_inter = _dot_nt((q * jnp.exp2(b2)).astype(BF16), st.astype(BF16))
    b_last = b2[CHUNK - 1:CHUNK]
    k_hat = (kk * jnp.exp2(b_last - b2)).astype(BF16)
    upd = _dot_tn(v_bf, k_hat)
    bd = (lax.broadcasted_iota(jnp.int32, (GROUP_WIDTH, GROUP_WIDTH), 0) // HG_DK
          == lax.broadcasted_iota(jnp.int32, (GROUP_WIDTH, GROUP_WIDTH), 1) // HG_DK)
    st_new = st * jnp.exp2(b_last) + jnp.where(bd, upd, 0.0)
    return o_intra + o_inter, st_new


def _hgrn_kernel(d_ref, lb_ref, gn_ref, tri_ref, ones_ref, o_ref, st_ref, *, th, l):
    @pl.when(pl.program_id(1) == 0)
    def _():
        st_ref[...] = jnp.zeros(st_ref.shape, F32)

    lb = lb_ref[l:l + 1, :]
    tri = tri_ref[...]
    ones_bd = ones_ref[...]
    for c in range(th // CHUNK):
        rows = slice(c * CHUNK, (c + 1) * CHUNK)
        for bb in range(d_ref.shape[0]):
            q = d_ref[bb, rows, 0:GROUP_WIDTH]
            fl = d_ref[bb, rows, GROUP_WIDTH:2 * GROUP_WIDTH]
            iv = d_ref[bb, rows, 2 * GROUP_WIDTH:3 * GROUP_WIDTH]
            g = d_ref[bb, rows, 3 * GROUP_WIDTH:4 * GROUP_WIDTH]
            o, st_new = _hgrn_chunk(q, fl, iv, lb, tri, ones_bd, st_ref[bb])
            st_ref[bb] = st_new
            ms = _dot((o * o).astype(BF16), ones_bd) * (1.0 / HG_DK)
            o = o * lax.rsqrt(ms + RMS_EPS) * gn_ref[l:l + 1, :]
            o_ref[bb, rows, :] = o * _sigmoid(g)


def _hgrn(d, l, lb, gn, tri, ones_bd):
    bsz, L, _ = d.shape
    th = 256
    nbb = HG_BATCH if bsz % HG_BATCH == 0 else 1
    const = lambda shape: pl.BlockSpec(shape, lambda b, t: (0,) * len(shape))
    return pl.pallas_call(
        functools.partial(_hgrn_kernel, th=th, l=l),
        out_shape=jax.ShapeDtypeStruct((bsz, L, GROUP_WIDTH), F32),
        grid=(bsz // nbb, L // th),
        in_specs=[pl.BlockSpec((nbb, th, 4 * GROUP_WIDTH), lambda b, t: (b, t, 0)),
                  _whole_spec(lb), _whole_spec(gn),
                  const((CHUNK, CHUNK)), const((GROUP_WIDTH, GROUP_WIDTH))],
        out_specs=pl.BlockSpec((nbb, th, GROUP_WIDTH), lambda b, t: (b, t, 0)),
        scratch_shapes=[pltpu.VMEM((nbb, GROUP_WIDTH, GROUP_WIDTH), F32)],
        compiler_params=_cparams(("arbitrary", "arbitrary")),
        name="hgrn2",
    )(d, lb, gn, tri, ones_bd)


def _merge_kernel(ya_ref, yb_ref, yc_ref, yd_ref, x_ref, lng_ref, lnb_ref, mod_ref, mg_ref, wo_ref, g_ref, b_ref,
                  o_ref, *, pre_ln, l):
    acc = None
    for gi, y_ref in enumerate((ya_ref, yb_ref, yc_ref, yd_ref)):
        y = y_ref[...] if gi == 2 else y_ref[0]
        yn = y * lax.rsqrt(jnp.mean(y * y, axis=-1, keepdims=True) + RMS_EPS)
        yn = (yn * mg_ref[l:l + 1, gi * GROUP_WIDTH:(gi + 1) * GROUP_WIDTH]).astype(BF16)
        part = _dot(yn, wo_ref[gi * GROUP_WIDTH:(gi + 1) * GROUP_WIDTH, :])
        acc = part if acc is None else acc + part
    g1 = mod_ref[0, :, 2 * D_MODEL:3 * D_MODEL]
    x = x_ref[0]
    if pre_ln:
        x = _layer_norm(x, lng_ref[...], lnb_ref[...])
    o_ref[0] = _layer_norm(DN_ALPHA * x + (1.0 + g1) * acc, g_ref[l:l + 1, :], b_ref[l:l + 1, :])


def _merge(ya, yb, yc, yd, x, l, pre_ln, ln_g, ln_b, mod4, mg, wo, g, b):
    bsz, L, d = x.shape
    tm = 1024
    per_layer = lambda a: _layer_spec(a, l)
    const = lambda shape: pl.BlockSpec(shape, lambda bb, t: (0,) * len(shape))
    row = lambda w: pl.BlockSpec((1, tm, w), lambda bb, t: (bb, t, 0))
    return pl.pallas_call(
        functools.partial(_merge_kernel, pre_ln=pre_ln, l=l),
        out_shape=jax.ShapeDtypeStruct((bsz, L, d), F32),
        grid=(bsz, L // tm),
        in_specs=[row(GROUP_WIDTH), row(GROUP_WIDTH), pl.BlockSpec((tm, GROUP_WIDTH), lambda bb, t: (t, bb)),
                  row(GROUP_WIDTH), row(d), const((1, d)), const((1, d)), _mod_spec(mod4, l),
                  _whole_spec(mg), per_layer(wo), _whole_spec(g), _whole_spec(b)],
        out_specs=row(d),
        compiler_params=_cparams(("arbitrary", "arbitrary")),
        name="merge_out",
    )(ya, yb, yc, yd, x, ln_g, ln_b, mod4, mg, wo, g, b)


def _ffn_kernel(x_ref, xh_ref, mod_ref, wup_ref, cw_ref, cb_ref, wdn_ref, g_ref, b_ref, o_ref,
                hp_ref, hx_ref, a_ref, act_ref, ap_ref, *, tm, l):
    ti = pl.program_id(1)
    seg = tm // FF_SEGS
    nslab = D_MODEL // LANES
    sh2 = mod_ref[0, :, 3 * D_MODEL:4 * D_MODEL]
    sc2 = mod_ref[0, :, 4 * D_MODEL:5 * D_MODEL]
    g2 = mod_ref[0, :, 5 * D_MODEL:6 * D_MODEL]
    x = x_ref[0]
    h2 = x * (1.0 + sc2) + sh2
    halo = jnp.where(ti > 0, xh_ref[0] * (1.0 + sc2) + sh2, 0.0)
    for s in range(nslab):
        lanes = slice(s * LANES, (s + 1) * LANES)
        hp_ref[s, 0:FF_HALO, :] = halo[:, lanes]
        for p in range(FF_SEGS):
            hp_ref[s, pl.ds(FF_HALO + p, seg, stride=FF_SEGS), :] = h2[p * seg:(p + 1) * seg, lanes]
    hx_ref[...] = jnp.concatenate([hp_ref[s] for s in range(nslab)], axis=1).astype(BF16)
    nj = D_FF // FF_TILE
    sub0 = lax.broadcasted_iota(jnp.int32, (FF_SEGS, FF_TILE), 0) == 0

    def up(j):
        hx = hx_ref[...]
        for half in range(2):
            a = a_ref.at[j % 2, half]
            cols = slice(half * D_FF + j * FF_TILE, half * D_FF + (j + 1) * FF_TILE)
            a[...] = _dot(hx, wup_ref[:, cols])
            prev1 = jnp.where(sub0, a[FF_HALO - 1:FF_HALO, :], pltpu.roll(a[FF_HALO + tm - 8:FF_HALO + tm, :], 1, 0))
            prev2 = jnp.where(sub0, a[FF_HALO - 2:FF_HALO - 1, :],
                              pltpu.roll(a[FF_HALO + tm - 16:FF_HALO + tm - 8, :], 1, 0))
            a[8:16, :] = prev1
            a[0:8, :] = prev2

    def conv(j, half):
        cols = slice(half * D_FF + j * FF_TILE, half * D_FF + (j + 1) * FF_TILE)
        a = a_ref.at[j % 2, half]
        return (a[0:tm, :] * cw_ref[0:1, cols] + a[8:tm + 8, :] * cw_ref[1:2, cols]
                + a[16:tm + 16, :] * cw_ref[2:3, cols] + cb_ref[l:l + 1, cols])

    up(0)
    acc = None
    for j in range(nj):
        if j + 1 < nj:
            up(j + 1)
        gate = conv(j, 1)
        act_ref[:, j * FF_TILE:(j + 1) * FF_TILE] = (conv(j, 0) * (gate * _sigmoid(gate))).astype(BF16)
        if j + 1 in FF_DOWN_SPLITS:
            k0 = ([0] + [e for e in FF_DOWN_SPLITS if e <= j])[-1] * FF_TILE
            k1 = (j + 1) * FF_TILE
            part = _dot(act_ref[:, k0:k1], wdn_ref[k0:k1, :])
            acc = part if acc is None else acc + part
    for s in range(nslab):
        ap_ref[s] = acc[:, s * LANES:(s + 1) * LANES]
    acc_nat = jnp.concatenate(
        [jnp.concatenate([ap_ref[s, pl.ds(p, seg, stride=FF_SEGS), :] for p in range(FF_SEGS)], axis=0)
         for s in range(nslab)], axis=1)
    o_ref[0] = _layer_norm(DN_ALPHA * x + (1.0 + g2) * acc_nat, g_ref[l:l + 1, :], b_ref[l:l + 1, :])


def _ffn(x, l, mod4, wup, cw, cb, wdn, g, b):
    bsz, L, d = x.shape
    tm = FF_ROWS
    per_layer = lambda a: _layer_spec(a, l, pipeline_mode=pl.Buffered(1))
    hb = tm // FF_HALO
    return pl.pallas_call(
        functools.partial(_ffn_kernel, tm=tm, l=l),
        out_shape=jax.ShapeDtypeStruct((bsz, L, d), F32),
        grid=(bsz, L // tm),
        in_specs=[pl.BlockSpec((1, tm, d), lambda bb, t: (bb, t, 0)),
                  pl.BlockSpec((1, FF_HALO, d), lambda bb, t: (bb, jnp.maximum(t * hb - 1, 0), 0)),
                  _mod_spec(mod4, l),
                  per_layer(wup), per_layer(cw), _whole_spec(cb), per_layer(wdn), _whole_spec(g), _whole_spec(b)],
        out_specs=pl.BlockSpec((1, tm, d), lambda bb, t: (bb, t, 0)),
        scratch_shapes=[pltpu.VMEM((d // LANES, tm + FF_HALO, LANES), F32),
                        pltpu.VMEM((tm + FF_HALO, d), BF16),
                        pltpu.VMEM((2, 2, tm + FF_HALO, FF_TILE), F32),
                        pltpu.VMEM((tm, D_FF), BF16),
                        pltpu.VMEM((d // LANES, tm, LANES), F32)],
        compiler_params=_cparams(("arbitrary", "arbitrary")),
        name="conv_ffn",
    )(x, x, mod4, wup, cw, cb, wdn, g, b)


def _rope_tables(L):
    inv = 1.0 / (ROPE_THETA ** (jnp.arange(0, MLA_ROPE_DIM, 2, dtype=F32) / MLA_ROPE_DIM))
    ang = jnp.arange(L, dtype=F32)[:, None] * inv[None, :]
    cos, sin = jnp.cos(ang), jnp.sin(ang)
    cc = jnp.concatenate([cos, cos], axis=1)
    ss = jnp.concatenate([-sin, sin], axis=1)
    scale = MLA_QK_DIM ** -0.5 * LOG2E
    zeros32 = jnp.zeros((L, HEAD_PAD - MLA_QK_DIM), F32)
    tq_c = jnp.concatenate([jnp.ones((L, MLA_NOPE_DIM), F32), cc, zeros32], axis=1) * scale
    tq_s = jnp.concatenate([jnp.zeros((L, MLA_NOPE_DIM), F32), ss, zeros32], axis=1) * scale
    tk = jnp.concatenate([cc, ss, jnp.zeros((L, 64), F32)], axis=1)
    return tq_c, tq_s, tk


def _swap_halves(w):
    half = MLA_ROPE_DIM // 2
    return jnp.concatenate([w[..., half:], w[..., :half]], axis=-1)


def _prep_inproj_weights(w_in_l, w_uq_l, w_ukv_l, q_norm_l, kv_norm_l):
    d = w_in_l.shape[0]
    a_u, a_v, cq, ckv, kr, c_u, d_q, d_f, d_i, d_g = jnp.split(
        w_in_l, (256, 512, 704, 832, 864, 1120, 1376, 1632, 1888), axis=1)
    z64 = jnp.zeros((d, 64), F32)
    w_ext = jnp.concatenate([a_u, a_v, c_u, d_q, d_f, d_i, d_g, cq, z64, ckv, kr, _swap_halves(kr), z64],
                            axis=1).astype(BF16)
    wq = w_uq_l.reshape(MLA_Q_RANK, MLA_HEADS, MLA_QK_DIM)
    zq32 = jnp.zeros((MLA_Q_RANK, MLA_HEADS, HEAD_PAD - MLA_QK_DIM), F32)
    zq64 = jnp.zeros((MLA_Q_RANK, MLA_HEADS, MLA_NOPE_DIM), F32)
    wq_a = jnp.concatenate([wq, zq32], axis=2).reshape(MLA_Q_RANK, QK_PAD)
    wq_b = jnp.concatenate([zq64, _swap_halves(wq[:, :, MLA_NOPE_DIM:]), zq32], axis=2).reshape(MLA_Q_RANK, QK_PAD)
    wq_ext = jnp.concatenate([wq_a, wq_b], axis=1)
    wq_ext = jnp.concatenate([wq_ext, jnp.zeros((256 - MLA_Q_RANK, 2 * QK_PAD), F32)], axis=0).astype(BF16)
    qn = jnp.concatenate([q_norm_l, jnp.zeros((256 - MLA_Q_RANK,), F32)]).reshape(1, 256)
    wkv = w_ukv_l.reshape(MLA_KV_RANK, MLA_HEADS, MLA_NOPE_DIM + MLA_V_DIM)
    zk64 = jnp.zeros((MLA_KV_RANK, MLA_HEADS, HEAD_PAD - MLA_NOPE_DIM), F32)
    wk = jnp.concatenate([wkv[:, :, :MLA_NOPE_DIM], zk64], axis=2).reshape(MLA_KV_RANK, QK_PAD)
    wv = wkv[:, :, MLA_NOPE_DIM:].reshape(MLA_KV_RANK, MLA_HEADS * MLA_V_DIM)
    eye = jnp.eye(MLA_ROPE_DIM, dtype=F32)
    place_h = jnp.concatenate([jnp.zeros((MLA_ROPE_DIM, MLA_NOPE_DIM), F32), eye,
                               jnp.zeros((MLA_ROPE_DIM, HEAD_PAD - MLA_QK_DIM), F32)], axis=1)
    place = jnp.concatenate([place_h] * MLA_HEADS, axis=1)
    place = jnp.concatenate([place, jnp.zeros((MLA_ROPE_DIM, MLA_HEADS * MLA_V_DIM), F32)], axis=1)
    top = jnp.concatenate([wk, wv], axis=1)
    wkv_ext = jnp.concatenate([top, place, place, jnp.zeros((64, QK_PAD + 256), F32)], axis=0).astype(BF16)
    kvn = kv_norm_l.reshape(1, MLA_KV_RANK)
    return w_ext, wq_ext, wkv_ext, qn, kvn


def _prep_gm(gm_ws_l, gm_bs_l):
    ck = jnp.arange(GM_BLOCK) // CHUNK
    mask = ck[None, :] <= ck[:, None]
    w = jnp.where(mask[None], gm_ws_l, 0)
    w = jnp.transpose(w, (1, 0, 2)).reshape(GM_BLOCK, GM_HEADS * GM_BLOCK).astype(BF16)
    bias = jnp.repeat(gm_bs_l.T, GROUP_WIDTH // GM_HEADS, axis=1)
    return w, bias


def _prep_s5(a_re, a_im, b_re, b_im, c_re, c_im, log_step):
    dt = jnp.exp(log_step)[:, None]
    mag = jnp.exp(dt * a_re)
    abar_re, abar_im = mag * jnp.cos(dt * a_im), mag * jnp.sin(dt * a_im)
    den = a_re * a_re + a_im * a_im
    nr, ni = abar_re - 1.0, abar_im
    zoh_re = (nr * a_re + ni * a_im) / den
    zoh_im = (ni * a_re - nr * a_im) / den
    bbar_re = zoh_re[..., None] * b_re - zoh_im[..., None] * b_im
    bbar_im = zoh_re[..., None] * b_im + zoh_im[..., None] * b_re
    eye_g = jnp.eye(S5_GROUPS, dtype=F32)
    bm_re = jnp.einsum('gpn,gh->gnhp', bbar_re, eye_g).reshape(GROUP_WIDTH, S5_NSTATE)
    bm_im = jnp.einsum('gpn,gh->gnhp', bbar_im, eye_g).reshape(GROUP_WIDTH, S5_NSTATE)
    bmat = jnp.concatenate([bm_re, bm_im], axis=1).astype(BF16)
    cm_re = jnp.einsum('gnp,gh->gphn', c_re, eye_g).reshape(S5_NSTATE, GROUP_WIDTH)
    cm_im = jnp.einsum('gnp,gh->gphn', c_im, eye_g).reshape(S5_NSTATE, GROUP_WIDTH)
    cmat = jnp.concatenate([cm_re, -cm_im], axis=0).astype(BF16)
    return bmat, abar_re.reshape(1, S5_NSTATE), abar_im.reshape(1, S5_NSTATE), cmat


def kernel(x, c, ln_in_g, ln_in_b, hg_lb_logits, w_ada, b_ada, w_in, gm_ln_g, gm_ln_b, gm_ws, gm_bs, mla_q_norm, mla_kv_norm, mla_w_uq, mla_w_ukv, s5_a_re, s5_a_im, s5_b_re, s5_b_im, s5_c_re, s5_c_im, s5_d, s5_log_step, s5_w_glu, s5_b_glu, hg_norm_g, merge_g, w_out, ln1_g, ln1_b, w_up, conv_w, conv_b, w_down, ln2_g, ln2_b):
    bsz, L, d = x.shape
    nl = w_in.shape[0]
    mod = _ada(c, w_ada, b_ada)
    lb_cum = jnp.cumsum(jax.nn.softmax(hg_lb_logits.astype(F32), axis=0), axis=0)
    tq_c, tq_s, tk = _rope_tables(L)
    tri = jnp.tril(jnp.ones((CHUNK, CHUNK), F32)).astype(BF16)
    hid = jnp.arange(GROUP_WIDTH) // HG_DK
    ones_bd = (hid[:, None] == hid[None, :]).astype(BF16)
    w_ext, wq_ext, wkv_ext, qn, kvn = jax.vmap(_prep_inproj_weights)(w_in, mla_w_uq, mla_w_ukv, mla_q_norm,
                                                                     mla_kv_norm)
    gm_w, gm_bias = jax.vmap(_prep_gm)(gm_ws, gm_bs)
    bmat, a_re, a_im, cmat = jax.vmap(_prep_s5)(s5_a_re, s5_a_im, s5_b_re, s5_b_im, s5_c_re, s5_c_im, s5_log_step)
    wglu_bf, wout_bf, wup_bf, wdn_bf = (w.astype(BF16) for w in (s5_w_glu, w_out, w_up, w_down))

    mod4 = mod.reshape(nl, bsz, 1, 6 * d)
    lbs = lb_cum - lb_cum[0:1]

    xs = x
    ln_g, ln_b = ln_in_g.reshape(1, d), ln_in_b.reshape(1, d)
    for l in range(nl):
        pre_ln = l == 0
        y_a, c_u, dd, q, k, v = _inproj(xs, l, pre_ln, ln_g, ln_b, mod4, w_ext, gm_ln_g, gm_ln_b, gm_w, gm_bias,
                                        qn, kvn, wq_ext, wkv_ext, tq_c, tq_s, tk)
        y_b = _attention(q, k, v)
        y_c = _s5(c_u, bsz, l, bmat, a_re, a_im, cmat, s5_d, wglu_bf, s5_b_glu)
        y_d = _hgrn(dd, l, lbs, hg_norm_g, tri, ones_bd)
        xs = _merge(y_a, y_b, y_c, y_d, xs, l, pre_ln, ln_g, ln_b, mod4, merge_g, wout_bf, ln1_g, ln1_b)
        xs = _ffn(xs, l, mod4, wup_bf, conv_w, conv_b, wdn_bf, ln2_g, ln2_b)
    return xs
```

```python
import functools
import math

import jax
import jax.numpy as jnp
from jax import lax
from jax.experimental import pallas as pl
from jax.experimental.pallas import tpu as pltpu

F32 = jnp.float32
BF16 = jnp.bfloat16

D_MODEL = 1024
DEPTH = 2
CHUNK = 64
GROUP_WIDTH = 256
N_MIXERS = 4
GM_HEADS = 4
GM_BLOCK = 128
MLA_HEADS = 4
MLA_NOPE_DIM = 64
MLA_ROPE_DIM = 32
MLA_QK_DIM = MLA_NOPE_DIM + MLA_ROPE_DIM
MLA_V_DIM = 64
MLA_Q_RANK = 192
MLA_KV_RANK = 128
ROPE_THETA = 10000.0
S5_GROUP = 16
S5_GROUPS = 16
S5_STATE = 64
HG_HEADS = 4
HG_DK = 64
D_FF = 2816
CONV_W = 3
DN_ALPHA = (2 * DEPTH) ** 0.25
LN_EPS = 1e-5
RMS_EPS = 1e-6
LOG2E = 1.4426950408889634

LANES = 128
HEAD_PAD = 128
QK_PAD = MLA_HEADS * HEAD_PAD
S5_NSTATE = S5_GROUPS * S5_STATE
HG_SUB = 8

COL_AU, COL_AV, COL_CU, COL_DQ = 0, 256, 512, 768
COL_CQ, COL_CKV, COL_KR = 1792, 2048, 2176
IN_EXT = 2304

FF_TILE = 256
FF_HALO = 16
FF_ROWS = 512
FF_SEGS = 8
FF_DOWN_SPLITS = (4, 8, 10, 11)

VMEM_LIMIT = 56 * 1024 * 1024


def _sigmoid(x):
    return 1.0 / (1.0 + jnp.exp(-x))


def _gelu(x):
    return 0.5 * x * (1.0 + jnp.tanh(math.sqrt(2.0 / math.pi) * (x + 0.044715 * (x * x * x))))


def _split2(x):
    hi = x.astype(BF16)
    lo = (x - hi.astype(F32)).astype(BF16)
    return hi, lo


def _dot(a, b):
    return jnp.dot(a, b, preferred_element_type=F32)


def _dot_nt(a, b):
    return lax.dot_general(a, b, (((1,), (1,)), ((), ())), preferred_element_type=F32)


def _dot_tn(a, b):
    return lax.dot_general(a, b, (((0,), (0,)), ((), ())), preferred_element_type=F32)


def _layer_norm(x, g, b):
    mu = jnp.mean(x, axis=-1, keepdims=True)
    xc = x - mu
    var = jnp.mean(xc * xc, axis=-1, keepdims=True)
    return xc * lax.rsqrt(var + LN_EPS) * g + b


def _layer_spec(stacked, l, **kwargs):
    shape = stacked.shape[1:]
    return pl.BlockSpec((None,) + shape, lambda *_: (l,) + (0,) * len(shape), **kwargs)


def _whole_spec(arr):
    return pl.BlockSpec(arr.shape, lambda *_: (0,) * arr.ndim)


def _mod_spec(mod4, l):
    return pl.BlockSpec((None, 1, 1, mod4.shape[-1]), lambda b, *_: (l, b, 0, 0))


def _cparams(sem, vmem=VMEM_LIMIT):
    return pltpu.CompilerParams(dimension_semantics=sem, vmem_limit_bytes=vmem)


def _ada_kernel(c_ref, w_ref, b_ref, o_ref):
    c = c_ref[...]
    chi, clo = _split2(c * _sigmoid(c))
    whi, wlo = _split2(w_ref[0])
    acc = _dot(chi, whi) + _dot(clo, whi) + _dot(chi, wlo)
    o_ref[0] = acc + b_ref[0]


def _ada(c, w_ada, b_ada):
    nl, d, n = w_ada.shape
    tn = 1536
    bsz = c.shape[0]
    return pl.pallas_call(
        _ada_kernel,
        out_shape=jax.ShapeDtypeStruct((nl, bsz, n), F32),
        grid=(nl, n // tn),
        in_specs=[pl.BlockSpec((bsz, d), lambda l, j: (0, 0)),
                  pl.BlockSpec((1, d, tn), lambda l, j: (l, 0, j)),
                  pl.BlockSpec((1, 1, tn), lambda l, j: (l, 0, j))],
        out_specs=pl.BlockSpec((1, bsz, tn), lambda l, j: (l, 0, j)),
        compiler_params=_cparams(("arbitrary", "arbitrary")),
        name="ada_mod",
    )(c, w_ada, b_ada.reshape(nl, 1, n))


def _inproj_kernel(x_ref, lng_ref, lnb_ref, mod_ref, w_ref, gmg_ref, gmb_ref, gmw_ref, gmbias_ref,
                   qn_ref, kvn_ref, wq_ref, wkv_ref, tq_c_ref, tq_s_ref, tk_ref,
                   ya_ref, cu_ref, d_ref, q_ref, k_ref, v_ref, *, tm, pre_ln, l):
    x = x_ref[0]
    if pre_ln:
        x = _layer_norm(x, lng_ref[...], lnb_ref[...])
    sh1 = mod_ref[0, :, 0:D_MODEL]
    sc1 = mod_ref[0, :, D_MODEL:2 * D_MODEL]
    h = (x * (1.0 + sc1) + sh1).astype(BF16)
    z = _dot(h, w_ref[...])

    cu_ref[...] = z[:, COL_CU:COL_CU + GROUP_WIDTH]
    d_ref[0] = z[:, COL_DQ:COL_DQ + 4 * GROUP_WIDTH]

    lane_head = lax.broadcasted_iota(jnp.int32, (GM_BLOCK, GROUP_WIDTH), 1) // (GROUP_WIDTH // GM_HEADS)
    for r in range(tm // GM_BLOCK):
        rows = slice(r * GM_BLOCK, (r + 1) * GM_BLOCK)
        u = _gelu(z[rows, COL_AU:COL_AU + GROUP_WIDTH])
        v = _gelu(z[rows, COL_AV:COL_AV + GROUP_WIDTH])
        vn = _layer_norm(v, gmg_ref[l:l + 1, :], gmb_ref[l:l + 1, :])
        vn_heads = jnp.concatenate([jnp.where(lane_head == hh, vn, 0.0) for hh in range(GM_HEADS)],
                                   axis=0).astype(BF16)
        ya_ref[0, rows, :] = u * (_dot(gmw_ref[...], vn_heads) + gmbias_ref[...])

    cq = z[:, COL_CQ:COL_CQ + 256]
    cqn = cq * lax.rsqrt(jnp.sum(cq * cq, axis=-1, keepdims=True) * (1.0 / MLA_Q_RANK) + RMS_EPS)
    cqn = (cqn * qn_ref[...]).astype(BF16)
    qq = _dot(cqn, wq_ref[...])
    tq_c = jnp.concatenate([tq_c_ref[...]] * MLA_HEADS, axis=1)
    tq_s = jnp.concatenate([tq_s_ref[...]] * MLA_HEADS, axis=1)
    q_ref[0] = (qq[:, :QK_PAD] * tq_c + qq[:, QK_PAD:] * tq_s).astype(BF16)

    ckv = z[:, COL_CKV:COL_CKV + MLA_KV_RANK]
    ckvn = ckv * lax.rsqrt(jnp.mean(ckv * ckv, axis=-1, keepdims=True) + RMS_EPS)
    ckvn = (ckvn * kvn_ref[...]).astype(BF16)
    krt = (z[:, COL_KR:COL_KR + 128] * tk_ref[...]).astype(BF16)
    kv = _dot(jnp.concatenate([ckvn, krt], axis=1), wkv_ref[...])
    k_ref[0] = kv[:, :QK_PAD].astype(BF16)
    one_col = (lax.broadcasted_iota(jnp.int32, (tm, HEAD_PAD - MLA_V_DIM), 1) == 0).astype(F32)
    for hh in range(MLA_HEADS):
        vh = kv[:, QK_PAD + hh * MLA_V_DIM:QK_PAD + (hh + 1) * MLA_V_DIM]
        v_ref[0, hh] = jnp.concatenate([vh, one_col], axis=1).astype(BF16)


def _inproj(x, l, pre_ln, ln_g, ln_b, mod4, w_ext, gm_g, gm_b, gm_w, gm_bias, qn, kvn, wq_ext, wkv_ext,
            tq_c, tq_s, tk):
    bsz, L, d = x.shape
    tm = 512
    per_layer = lambda a: _layer_spec(a, l)
    const = lambda shape: pl.BlockSpec(shape, lambda b, t: (0,) * len(shape))
    out_shape = (
        jax.ShapeDtypeStruct((bsz, L, GROUP_WIDTH), F32),
        jax.ShapeDtypeStruct((L, bsz * GROUP_WIDTH), F32),
        jax.ShapeDtypeStruct((bsz, L, 4 * GROUP_WIDTH), F32),
        jax.ShapeDtypeStruct((bsz, L, QK_PAD), BF16),
        jax.ShapeDtypeStruct((bsz, L, QK_PAD), BF16),
        jax.ShapeDtypeStruct((bsz, MLA_HEADS, L, HEAD_PAD), BF16),
    )
    row = lambda w: pl.BlockSpec((1, tm, w), lambda b, t: (b, t, 0))
    return pl.pallas_call(
        functools.partial(_inproj_kernel, tm=tm, pre_ln=pre_ln, l=l),
        out_shape=out_shape,
        grid=(bsz, L // tm),
        in_specs=[row(d), const((1, d)), const((1, d)), _mod_spec(mod4, l),
                  per_layer(w_ext), _whole_spec(gm_g), _whole_spec(gm_b), per_layer(gm_w), per_layer(gm_bias),
                  per_layer(qn), per_layer(kvn), per_layer(wq_ext), per_layer(wkv_ext),
                  pl.BlockSpec((tm, HEAD_PAD), lambda b, t: (t, 0)),
                  pl.BlockSpec((tm, HEAD_PAD), lambda b, t: (t, 0)),
                  pl.BlockSpec((tm, 128), lambda b, t: (t, 0))],
        out_specs=(row(GROUP_WIDTH), pl.BlockSpec((tm, GROUP_WIDTH), lambda b, t: (t, b)),
                   row(4 * GROUP_WIDTH), row(QK_PAD), row(QK_PAD),
                   pl.BlockSpec((1, MLA_HEADS, tm, HEAD_PAD), lambda b, t: (b, 0, t, 0))),
        compiler_params=_cparams(("arbitrary", "arbitrary")),
        name="inproj",
    )(x, ln_g, ln_b, mod4, w_ext, gm_g, gm_b, gm_w, gm_bias, qn, kvn, wq_ext, wkv_ext, tq_c, tq_s, tk)


ATT_TQ = 512
ATT_HEADS_PER_STEP = 4


def _attn_kernel(q_ref, k_ref, v_ref, o_ref):
    L = q_ref.shape[1]
    tq = ATT_TQ
    q_chunk = lax.broadcasted_iota(jnp.int32, (tq, tq), 0) // CHUNK
    k_chunk = lax.broadcasted_iota(jnp.int32, (tq, tq), 1) // CHUNK
    allowed = k_chunk <= q_chunk
    nq = L // tq
    order = [t for pair in zip(range(nq // 2), range(nq - 1, nq // 2 - 1, -1)) for t in pair] + (
        [nq // 2] if nq % 2 else [])
    for qi in order:
        lk = (qi + 1) * tq
        outs = []
        for hh in range(ATT_HEADS_PER_STEP):
            lanes = slice(hh * HEAD_PAD, (hh + 1) * HEAD_PAD)
            qh = q_ref[0, qi * tq:(qi + 1) * tq, lanes]
            s = _dot_nt(qh, k_ref[0, 0:lk, lanes])
            s_diag = jnp.where(allowed, s[:, lk - tq:], -1e30)
            s = s_diag if qi == 0 else jnp.concatenate([s[:, :lk - tq], s_diag], axis=1)
            m = jnp.max(s, axis=-1, keepdims=True)
            p = jnp.exp2(s - m)
            pv = _dot(p.astype(BF16), v_ref[0, hh, 0:lk, :])
            outs.append(pv[:, 0:MLA_V_DIM] / pv[:, MLA_V_DIM:MLA_V_DIM + 1])
        o_ref[0, qi * tq:(qi + 1) * tq, :] = jnp.concatenate(outs, axis=1)


def _attention(q, k, v):
    bsz, L, _ = q.shape
    hp = ATT_HEADS_PER_STEP
    return pl.pallas_call(
        _attn_kernel,
        out_shape=jax.ShapeDtypeStruct((bsz, L, GROUP_WIDTH), F32),
        grid=(bsz, MLA_HEADS // hp),
        in_specs=[pl.BlockSpec((1, L, hp * HEAD_PAD), lambda b, g: (b, 0, g)),
                  pl.BlockSpec((1, L, hp * HEAD_PAD), lambda b, g: (b, 0, g)),
                  pl.BlockSpec((1, hp, L, HEAD_PAD), lambda b, g: (b, g, 0, 0))],
        out_specs=pl.BlockSpec((1, L, hp * MLA_V_DIM), lambda b, g: (b, 0, g)),
        compiler_params=_cparams(("arbitrary", "arbitrary")),
        name="mla_attention",
    )(q, k, v)


def _s5_kernel(u_ref, bmat_ref, are_ref, aim_ref, cmat_ref, dvec_ref, wglu_ref, bglu_ref,
               o_ref, h_ref, us_ref, bu_ref, hs_ref, os_ref, *, tc, nb, l):
    @pl.when(pl.program_id(0) == 0)
    def _():
        h_ref[...] = jnp.zeros(h_ref.shape, F32)

    nslab = GROUP_WIDTH // LANES
    for bb in range(nb):
        for s in range(nslab):
            lanes = slice(bb * GROUP_WIDTH + s * LANES, bb * GROUP_WIDTH + (s + 1) * LANES)
            us_ref[s, pl.ds(bb, tc, stride=nb), :] = u_ref[:, lanes]
    u = jnp.concatenate([us_ref[s] for s in range(nslab)], axis=1)
    bu_ref[...] = _dot(u.astype(BF16), bmat_ref[...])
    a_re = jnp.broadcast_to(are_ref[...], (nb, S5_NSTATE))
    a_im = jnp.broadcast_to(aim_ref[...], (nb, S5_NSTATE))
    re = slice(0, S5_NSTATE)
    im = slice(S5_NSTATE, 2 * S5_NSTATE)

    def step(i, carry):
        h_re, h_im = carry
        r0 = pl.ds(pl.multiple_of(i * 2 * nb, 2 * nb), nb)
        r1 = pl.ds(pl.multiple_of(i * 2 * nb, 2 * nb) + nb, nb)
        m_re = a_re * h_re - a_im * h_im + bu_ref[r0, re]
        m_im = a_re * h_im + a_im * h_re + bu_ref[r0, im]
        n_re = a_re * m_re - a_im * m_im + bu_ref[r1, re]
        n_im = a_re * m_im + a_im * m_re + bu_ref[r1, im]
        rr = pl.ds(pl.multiple_of(i * 2 * nb, 2 * nb), 2 * nb)
        hs_ref[rr, re] = jnp.concatenate([m_re, n_re], axis=0).astype(BF16)
        hs_ref[rr, im] = jnp.concatenate([m_im, n_im], axis=0).astype(BF16)
        return n_re, n_im

    h_re, h_im = lax.fori_loop(0, tc // 2, step, (h_ref[:, re], h_ref[:, im]))
    h_ref[:, re] = h_re
    h_ref[:, im] = h_im

    half = (tc * nb) // 2
    y = jnp.concatenate([_dot(hs_ref[0:half, :], cmat_ref[...]), _dot(hs_ref[half:2 * half, :], cmat_ref[...])],
                        axis=0) + dvec_ref[l:l + 1, :] * u
    z = _gelu(y)
    z_bf = z.astype(BF16)
    gate = _sigmoid(jnp.concatenate([_dot(z_bf[0:half], wglu_ref[...]), _dot(z_bf[half:2 * half], wglu_ref[...])],
                                    axis=0) + bglu_ref[l:l + 1, :])
    res = z * gate
    for s in range(nslab):
        os_ref[s] = res[:, s * LANES:(s + 1) * LANES]
    for bb in range(nb):
        for s in range(nslab):
            lanes = slice(bb * GROUP_WIDTH + s * LANES, bb * GROUP_WIDTH + (s + 1) * LANES)
            o_ref[:, lanes] = os_ref[s, pl.ds(bb, tc, stride=nb), :]


def _s5(u_t, nb, l, bmat, a_re, a_im, cmat, dvec, wglu, bglu):
    L = u_t.shape[0]
    gw = GROUP_WIDTH
    tc = 256
    per_layer = lambda a: _layer_spec(a, l)
    return pl.pallas_call(
        functools.partial(_s5_kernel, tc=tc, nb=nb, l=l),
        out_shape=jax.ShapeDtypeStruct((L, nb * gw), F32),
        grid=(L // tc,),
        in_specs=[pl.BlockSpec((tc, nb * gw), lambda t: (t, 0)),
                  per_layer(bmat), per_layer(a_re), per_layer(a_im), per_layer(cmat), _whole_spec(dvec),
                  per_layer(wglu), _whole_spec(bglu)],
        out_specs=pl.BlockSpec((tc, nb * gw), lambda t: (t, 0)),
        scratch_shapes=[pltpu.VMEM((nb, 2 * S5_NSTATE), F32),
                        pltpu.VMEM((gw // LANES, tc * nb, LANES), F32),
                        pltpu.VMEM((tc * nb, 2 * S5_NSTATE), F32),
                        pltpu.VMEM((tc * nb, 2 * S5_NSTATE), BF16),
                        pltpu.VMEM((gw // LANES, tc * nb, LANES), F32)],
        compiler_params=_cparams(("arbitrary",)),
        name="s5_scan",
    )(u_t, bmat, a_re, a_im, cmat, dvec, wglu, bglu)


HG_BATCH = 2


def _hgrn_chunk(q, fl, iv, lb, tri, ones_bd, st):
    nsub = CHUNK // HG_SUB
    f = lb + (1.0 - lb) * _sigmoid(fl)
    logf = jnp.log(f)
    kk = 1.0 - f
    hi, lo = _split2(logf)
    b2 = (_dot(tri, hi) + _dot(tri, lo)) * LOG2E
    lane_head = lax.broadcasted_iota(jnp.int32, (HG_SUB, GROUP_WIDTH), 1) // HG_DK
    row_id = lax.broadcasted_iota(jnp.int32, (HG_SUB, GROUP_WIDTH), 0)
    v_bf = iv.astype(BF16)

    terms = []
    for si in range(nsub):
        r0 = si * HG_SUB
        q_i, k_i, b_i = (a[r0:r0 + HG_SUB] for a in (q, kk, b2))
        for j in range(HG_SUB):
            e = jnp.exp2(jnp.where(row_id >= j, b_i - b_i[j:j + 1], -1e30))
            terms.append(q_i * k_i[j:j + 1] * e)
    p = _dot(jnp.concatenate(terms, axis=0).astype(BF16), ones_bd)
    o_parts = []
    for si in range(nsub):
        r0 = si * HG_SUB
        base = si * HG_SUB * HG_SUB
        o_i = p[base:base + HG_SUB] * iv[r0:r0 + 1]
        for j in range(1, HG_SUB):
            o_i = o_i + p[base + j * HG_SUB:base + (j + 1) * HG_SUB] * iv[r0 + j:r0 + j + 1]
        o_parts.append(o_i)

    atts = []
    for si in range(1, nsub):
        r0 = si * HG_SUB
        b_edge = b2[r0 - 1:r0]
        q_t = q[r0:r0 + HG_SUB] * jnp.exp2(b2[r0:r0 + HG_SUB] - b_edge)
        k_t = jnp.concatenate([kk[0:r0] * jnp.exp2(b_edge - b2[0:r0]),
                               jnp.zeros((CHUNK - r0, GROUP_WIDTH), F32)], axis=0).astype(BF16)
        q_exp = jnp.concatenate([jnp.where(lane_head == hh, q_t, 0.0) for hh in range(HG_HEADS)],
                                axis=0).astype(BF16)
        atts.append(_dot_nt(q_exp, k_t))
    o2 = _dot(jnp.concatenate(atts, axis=0).astype(BF16), v_bf)
    for si in range(1, nsub):
        base = (si - 1) * HG_HEADS * HG_SUB
        for hh in range(HG_HEADS):
            o_parts[si] = o_parts[si] + jnp.where(lane_head == hh,
                                                  o2[base + hh * HG_SUB:base + (hh + 1) * HG_SUB], 0.0)
    o_intra = jnp.concatenate(o_parts, axis=0)

    o_inter = _dot_nt((q * jnp.exp2(b2)).astype(BF16), st.astype(BF16))
    b_last = b2[CHUNK - 1:CHUNK]
    k_hat = (kk * jnp.exp2(b_last - b2)).astype(BF16)
    upd = _dot_tn(v_bf, k_hat)
    bd = (lax.broadcasted_iota(jnp.int32, (GROUP_WIDTH, GROUP_WIDTH), 0) // HG_DK
          == lax.broadcasted_iota(jnp.int32, (GROUP_WIDTH, GROUP_WIDTH), 1) // HG_DK)
    st_new = st * jnp.exp2(b_last) + jnp.where(bd, upd, 0.0)
    return o_intra + o_inter, st_new


def _hgrn_kernel(d_ref, lb_ref, gn_ref, tri_ref, ones_ref, o_ref, st_ref, *, th, l):
    @pl.when(pl.program_id(1) == 0)
    def _():
        st_ref[...] = jnp.zeros(st_ref.shape, F32)

    lb = lb_ref[l:l + 1, :]
    tri = tri_ref[...]
    ones_bd = ones_ref[...]
    for c in range(th // CHUNK):
        rows = slice(c * CHUNK, (c + 1) * CHUNK)
        for bb in range(d_ref.shape[0]):
            q = d_ref[bb, rows, 0:GROUP_WIDTH]
            fl = d_ref[bb, rows, GROUP_WIDTH:2 * GROUP_WIDTH]
            iv = d_ref[bb, rows, 2 * GROUP_WIDTH:3 * GROUP_WIDTH]
            g = d_ref[bb, rows, 3 * GROUP_WIDTH:4 * GROUP_WIDTH]
            o, st_new = _hgrn_chunk(q, fl, iv, lb, tri, ones_bd, st_ref[bb])
            st_ref[bb] = st_new
            ms = _dot((o * o).astype(BF16), ones_bd) * (1.0 / HG_DK)
            o = o * lax.rsqrt(ms + RMS_EPS) * gn_ref[l:l + 1, :]
            o_ref[bb, rows, :] = o * _sigmoid(g)


def _hgrn(d, l, lb, gn, tri, ones_bd):
    bsz, L, _ = d.shape
    th = 256
    nbb = HG_BATCH if bsz % HG_BATCH == 0 else 1
    const = lambda shape: pl.BlockSpec(shape, lambda b, t: (0,) * len(shape))
    return pl.pallas_call(
        functools.partial(_hgrn_kernel, th=th, l=l),
        out_shape=jax.ShapeDtypeStruct((bsz, L, GROUP_WIDTH), F32),
        grid=(bsz // nbb, L // th),
        in_specs=[pl.BlockSpec((nbb, th, 4 * GROUP_WIDTH), lambda b, t: (b, t, 0)),
                  _whole_spec(lb), _whole_spec(gn),
                  const((CHUNK, CHUNK)), const((GROUP_WIDTH, GROUP_WIDTH))],
        out_specs=pl.BlockSpec((nbb, th, GROUP_WIDTH), lambda b, t: (b, t, 0)),
        scratch_shapes=[pltpu.VMEM((nbb, GROUP_WIDTH, GROUP_WIDTH), F32)],
        compiler_params=_cparams(("arbitrary", "arbitrary")),
        name="hgrn2",
    )(d, lb, gn, tri, ones_bd)


def _merge_kernel(ya_ref, yb_ref, yc_ref, yd_ref, x_ref, lng_ref, lnb_ref, mod_ref, mg_ref, wo_ref, g_ref, b_ref,
                  o_ref, *, pre_ln, l):
    acc = None
    for gi, y_ref in enumerate((ya_ref, yb_ref, yc_ref, yd_ref)):
        y = y_ref[...] if gi == 2 else y_ref[0]
        yn = y * lax.rsqrt(jnp.mean(y * y, axis=-1, keepdims=True) + RMS_EPS)
        yn = (yn * mg_ref[l:l + 1, gi * GROUP_WIDTH:(gi + 1) * GROUP_WIDTH]).astype(BF16)
        part = _dot(yn, wo_ref[gi * GROUP_WIDTH:(gi + 1) * GROUP_WIDTH, :])
        acc = part if acc is None else acc + part
    g1 = mod_ref[0, :, 2 * D_MODEL:3 * D_MODEL]
    x = x_ref[0]
    if pre_ln:
        x = _layer_norm(x, lng_ref[...], lnb_ref[...])
    o_ref[0] = _layer_norm(DN_ALPHA * x + (1.0 + g1) * acc, g_ref[l:l + 1, :], b_ref[l:l + 1, :])


def _merge(ya, yb, yc, yd, x, l, pre_ln, ln_g, ln_b, mod4, mg, wo, g, b):
    bsz, L, d = x.shape
    tm = 1024
    per_layer = lambda a: _layer_spec(a, l)
    const = lambda shape: pl.BlockSpec(shape, lambda bb, t: (0,) * len(shape))
    row = lambda w: pl.BlockSpec((1, tm, w), lambda bb, t: (bb, t, 0))
    return pl.pallas_call(
        functools.partial(_merge_kernel, pre_ln=pre_ln, l=l),
        out_shape=jax.ShapeDtypeStruct((bsz, L, d), F32),
        grid=(bsz, L // tm),
        in_specs=[row(GROUP_WIDTH), row(GROUP_WIDTH), pl.BlockSpec((tm, GROUP_WIDTH), lambda bb, t: (t, bb)),
                  row(GROUP_WIDTH), row(d), const((1, d)), const((1, d)), _mod_spec(mod4, l),
                  _whole_spec(mg), per_layer(wo), _whole_spec(g), _whole_spec(b)],
        out_specs=row(d),
        compiler_params=_cparams(("arbitrary", "arbitrary")),
        name="merge_out",
    )(ya, yb, yc, yd, x, ln_g, ln_b, mod4, mg, wo, g, b)


def _ffn_kernel(x_ref, xh_ref, mod_ref, wup_ref, cw_ref, cb_ref, wdn_ref, g_ref, b_ref, o_ref,
                hp_ref, hx_ref, a_ref, act_ref, ap_ref, *, tm, l):
    ti = pl.program_id(1)
    seg = tm // FF_SEGS
    nslab = D_MODEL // LANES
    sh2 = mod_ref[0, :, 3 * D_MODEL:4 * D_MODEL]
    sc2 = mod_ref[0, :, 4 * D_MODEL:5 * D_MODEL]
    g2 = mod_ref[0, :, 5 * D_MODEL:6 * D_MODEL]
    x = x_ref[0]
    h2 = x * (1.0 + sc2) + sh2
    halo = jnp.where(ti > 0, xh_ref[0] * (1.0 + sc2) + sh2, 0.0)
    for s in range(nslab):
        lanes = slice(s * LANES, (s + 1) * LANES)
        hp_ref[s, 0:FF_HALO, :] = halo[:, lanes]
        for p in range(FF_SEGS):
            hp_ref[s, pl.ds(FF_HALO + p, seg, stride=FF_SEGS), :] = h2[p * seg:(p + 1) * seg, lanes]
    hx_ref[...] = jnp.concatenate([hp_ref[s] for s in range(nslab)], axis=1).astype(BF16)
    nj = D_FF // FF_TILE
    sub0 = lax.broadcasted_iota(jnp.int32, (FF_SEGS, FF_TILE), 0) == 0

    def up(j):
        hx = hx_ref[...]
        for half in range(2):
            a = a_ref.at[j % 2, half]
            cols = slice(half * D_FF + j * FF_TILE, half * D_FF + (j + 1) * FF_TILE)
            a[...] = _dot(hx, wup_ref[:, cols])
            prev1 = jnp.where(sub0, a[FF_HALO - 1:FF_HALO, :], pltpu.roll(a[FF_HALO + tm - 8:FF_HALO + tm, :], 1, 0))
            prev2 = jnp.where(sub0, a[FF_HALO - 2:FF_HALO - 1, :],
                              pltpu.roll(a[FF_HALO + tm - 16:FF_HALO + tm - 8, :], 1, 0))
            a[8:16, :] = prev1
            a[0:8, :] = prev2

    def conv(j, half):
        cols = slice(half * D_FF + j * FF_TILE, half * D_FF + (j + 1) * FF_TILE)
        a = a_ref.at[j % 2, half]
        return (a[0:tm, :] * cw_ref[0:1, cols] + a[8:tm + 8, :] * cw_ref[1:2, cols]
                + a[16:tm + 16, :] * cw_ref[2:3, cols] + cb_ref[l:l + 1, cols])

    up(0)
    acc = None
    for j in range(nj):
        if j + 1 < nj:
            up(j + 1)
        gate = conv(j, 1)
        act_ref[:, j * FF_TILE:(j + 1) * FF_TILE] = (conv(j, 0) * (gate * _sigmoid(gate))).astype(BF16)
        if j + 1 in FF_DOWN_SPLITS:
            k0 = ([0] + [e for e in FF_DOWN_SPLITS if e <= j])[-1] * FF_TILE
            k1 = (j + 1) * FF_TILE
            part = _dot(act_ref[:, k0:k1], wdn_ref[k0:k1, :])
            acc = part if acc is None else acc + part
    for s in range(nslab):
        ap_ref[s] = acc[:, s * LANES:(s + 1) * LANES]
    acc_nat = jnp.concatenate(
        [jnp.concatenate([ap_ref[s, pl.ds(p, seg, stride=FF_SEGS), :] for p in range(FF_SEGS)], axis=0)
         for s in range(nslab)], axis=1)
    o_ref[0] = _layer_norm(DN_ALPHA * x + (1.0 + g2) * acc_nat, g_ref[l:l + 1, :], b_ref[l:l + 1, :])


def _ffn(x, l, mod4, wup, cw, cb, wdn, g, b):
    bsz, L, d = x.shape
    tm = FF_ROWS
    per_layer = lambda a: _layer_spec(a, l, pipeline_mode=pl.Buffered(1))
    hb = tm // FF_HALO
    return pl.pallas_call(
        functools.partial(_ffn_kernel, tm=tm, l=l),
        out_shape=jax.ShapeDtypeStruct((bsz, L, d), F32),
        grid=(bsz, L // tm),
        in_specs=[pl.BlockSpec((1, tm, d), lambda bb, t: (bb, t, 0)),
                  pl.BlockSpec((1, FF_HALO, d), lambda bb, t: (bb, jnp.maximum(t * hb - 1, 0), 0)),
                  _mod_spec(mod4, l),
                  per_layer(wup), per_layer(cw), _whole_spec(cb), per_layer(wdn), _whole_spec(g), _whole_spec(b)],
        out_specs=pl.BlockSpec((1, tm, d), lambda bb, t: (bb, t, 0)),
        scratch_shapes=[pltpu.VMEM((d // LANES, tm + FF_HALO, LANES), F32),
                        pltpu.VMEM((tm + FF_HALO, d), BF16),
                        pltpu.VMEM((2, 2, tm + FF_HALO, FF_TILE), F32),
                        pltpu.VMEM((tm, D_FF), BF16),
                        pltpu.VMEM((d // LANES, tm, LANES), F32)],
        compiler_params=_cparams(("arbitrary", "arbitrary")),
        name="conv_ffn",
    )(x, x, mod4, wup, cw, cb, wdn, g, b)


def _rope_tables(L):
    inv = 1.0 / (ROPE_THETA ** (jnp.arange(0, MLA_ROPE_DIM, 2, dtype=F32) / MLA_ROPE_DIM))
    ang = jnp.arange(L, dtype=F32)[:, None] * inv[None, :]
    cos, sin = jnp.cos(ang), jnp.sin(ang)
    cc = jnp.concatenate([cos, cos], axis=1)
    ss = jnp.concatenate([-sin, sin], axis=1)
    scale = MLA_QK_DIM ** -0.5 * LOG2E
    zeros32 = jnp.zeros((L, HEAD_PAD - MLA_QK_DIM), F32)
    tq_c = jnp.concatenate([jnp.ones((L, MLA_NOPE_DIM), F32), cc, zeros32], axis=1) * scale
    tq_s = jnp.concatenate([jnp.zeros((L, MLA_NOPE_DIM), F32), ss, zeros32], axis=1) * scale
    tk = jnp.concatenate([cc, ss, jnp.zeros((L, 64), F32)], axis=1)
    return tq_c, tq_s, tk


def _swap_halves(w):
    half = MLA_ROPE_DIM // 2
    return jnp.concatenate([w[..., half:], w[..., :half]], axis=-1)


def _prep_inproj_weights(w_in_l, w_uq_l, w_ukv_l, q_norm_l, kv_norm_l):
    d = w_in_l.shape[0]
    a_u, a_v, cq, ckv, kr, c_u, d_q, d_f, d_i, d_g = jnp.split(
        w_in_l, (256, 512, 704, 832, 864, 1120, 1376, 1632, 1888), axis=1)
    z64 = jnp.zeros((d, 64), F32)
    w_ext = jnp.concatenate([a_u, a_v, c_u, d_q, d_f, d_i, d_g, cq, z64, ckv, kr, _swap_halves(kr), z64],
                            axis=1).astype(BF16)
    wq = w_uq_l.reshape(MLA_Q_RANK, MLA_HEADS, MLA_QK_DIM)
    zq32 = jnp.zeros((MLA_Q_RANK, MLA_HEADS, HEAD_PAD - MLA_QK_DIM), F32)
    zq64 = jnp.zeros((MLA_Q_RANK, MLA_HEADS, MLA_NOPE_DIM), F32)
    wq_a = jnp.concatenate([wq, zq32], axis=2).reshape(MLA_Q_RANK, QK_PAD)
    wq_b = jnp.concatenate([zq64, _swap_halves(wq[:, :, MLA_NOPE_DIM:]), zq32], axis=2).reshape(MLA_Q_RANK, QK_PAD)
    wq_ext = jnp.concatenate([wq_a, wq_b], axis=1)
    wq_ext = jnp.concatenate([wq_ext, jnp.zeros((256 - MLA_Q_RANK, 2 * QK_PAD), F32)], axis=0).astype(BF16)
    qn = jnp.concatenate([q_norm_l, jnp.zeros((256 - MLA_Q_RANK,), F32)]).reshape(1, 256)
    wkv = w_ukv_l.reshape(MLA_KV_RANK, MLA_HEADS, MLA_NOPE_DIM + MLA_V_DIM)
    zk64 = jnp.zeros((MLA_KV_RANK, MLA_HEADS, HEAD_PAD - MLA_NOPE_DIM), F32)
    wk = jnp.concatenate([wkv[:, :, :MLA_NOPE_DIM], zk64], axis=2).reshape(MLA_KV_RANK, QK_PAD)
    wv = wkv[:, :, MLA_NOPE_DIM:].reshape(MLA_KV_RANK, MLA_HEADS * MLA_V_DIM)
    eye = jnp.eye(MLA_ROPE_DIM, dtype=F32)
    place_h = jnp.concatenate([jnp.zeros((MLA_ROPE_DIM, MLA_NOPE_DIM), F32), eye,
                               jnp.zeros((MLA_ROPE_DIM, HEAD_PAD - MLA_QK_DIM), F32)], axis=1)
    place = jnp.concatenate([place_h] * MLA_HEADS, axis=1)
    place = jnp.concatenate([place, jnp.zeros((MLA_ROPE_DIM, MLA_HEADS * MLA_V_DIM), F32)], axis=1)
    top = jnp.concatenate([wk, wv], axis=1)
    wkv_ext = jnp.concatenate([top, place, place, jnp.zeros((64, QK_PAD + 256), F32)], axis=0).astype(BF16)
    kvn = kv_norm_l.reshape(1, MLA_KV_RANK)
    return w_ext, wq_ext, wkv_ext, qn, kvn


def _prep_gm(gm_ws_l, gm_bs_l):
    ck = jnp.arange(GM_BLOCK) // CHUNK
    mask = ck[None, :] <= ck[:, None]
    w = jnp.where(mask[None], gm_ws_l, 0)
    w = jnp.transpose(w, (1, 0, 2)).reshape(GM_BLOCK, GM_HEADS * GM_BLOCK).astype(BF16)
    bias = jnp.repeat(gm_bs_l.T, GROUP_WIDTH // GM_HEADS, axis=1)
    return w, bias


def _prep_s5(a_re, a_im, b_re, b_im, c_re, c_im, log_step):
    dt = jnp.exp(log_step)[:, None]
    mag = jnp.exp(dt * a_re)
    abar_re, abar_im = mag * jnp.cos(dt * a_im), mag * jnp.sin(dt * a_im)
    den = a_re * a_re + a_im * a_im
    nr, ni = abar_re - 1.0, abar_im
    zoh_re = (nr * a_re + ni * a_im) / den
    zoh_im = (ni * a_re - nr * a_im) / den
    bbar_re = zoh_re[..., None] * b_re - zoh_im[..., None] * b_im
    bbar_im = zoh_re[..., None] * b_im + zoh_im[..., None] * b_re
    eye_g = jnp.eye(S5_GROUPS, dtype=F32)
    bm_re = jnp.einsum('gpn,gh->gnhp', bbar_re, eye_g).reshape(GROUP_WIDTH, S5_NSTATE)
    bm_im = jnp.einsum('gpn,gh->gnhp', bbar_im, eye_g).reshape(GROUP_WIDTH, S5_NSTATE)
    bmat = jnp.concatenate([bm_re, bm_im], axis=1).astype(BF16)
    cm_re = jnp.einsum('gnp,gh->gphn', c_re, eye_g).reshape(S5_NSTATE, GROUP_WIDTH)
    cm_im = jnp.einsum('gnp,gh->gphn', c_im, eye_g).reshape(S5_NSTATE, GROUP_WIDTH)
    cmat = jnp.concatenate([cm_re, -cm_im], axis=0).astype(BF16)
    return bmat, abar_re.reshape(1, S5_NSTATE), abar_im.reshape(1, S5_NSTATE), cmat


def kernel(x, c, ln_in_g, ln_in_b, hg_lb_logits, w_ada, b_ada, w_in, gm_ln_g, gm_ln_b, gm_ws, gm_bs, mla_q_norm, mla_kv_norm, mla_w_uq, mla_w_ukv, s5_a_re, s5_a_im, s5_b_re, s5_b_im, s5_c_re, s5_c_im, s5_d, s5_log_step, s5_w_glu, s5_b_glu, hg_norm_g, merge_g, w_out, ln1_g, ln1_b, w_up, conv_w, conv_b, w_down, ln2_g, ln2_b):
    bsz, L, d = x.shape
    nl = w_in.shape[0]
    mod = _ada(c, w_ada, b_ada)
    lb_cum = jnp.cumsum(jax.nn.softmax(hg_lb_logits.astype(F32), axis=0), axis=0)
    tq_c, tq_s, tk = _rope_tables(L)
    tri = jnp.tril(jnp.ones((CHUNK, CHUNK), F32)).astype(BF16)
    hid = jnp.arange(GROUP_WIDTH) // HG_DK
    ones_bd = (hid[:, None] == hid[None, :]).astype(BF16)
    w_ext, wq_ext, wkv_ext, qn, kvn = jax.vmap(_prep_inproj_weights)(w_in, mla_w_uq, mla_w_ukv, mla_q_norm,
                                                                     mla_kv_norm)
    gm_w, gm_bias = jax.vmap(_prep_gm)(gm_ws, gm_bs)
    bmat, a_re, a_im, cmat = jax.vmap(_prep_s5)(s5_a_re, s5_a_im, s5_b_re, s5_b_im, s5_c_re, s5_c_im, s5_log_step)
    wglu_bf, wout_bf, wup_bf, wdn_bf = (w.astype(BF16) for w in (s5_w_glu, w_out, w_up, w_down))

    mod4 = mod.reshape(nl, bsz, 1, 6 * d)
    lbs = lb_cum - lb_cum[0:1]

    xs = x
    ln_g, ln_b = ln_in_g.reshape(1, d), ln_in_b.reshape(1, d)
    for l in range(nl):
        pre_ln = l == 0
        y_a, c_u, dd, q, k, v = _inproj(xs, l, pre_ln, ln_g, ln_b, mod4, w_ext, gm_ln_g, gm_ln_b, gm_w, gm_bias,
                                        qn, kvn, wq_ext, wkv_ext, tq_c, tq_s, tk)
        y_b = _attention(q, k, v)
        y_c = _s5(c_u, bsz, l, bmat, a_re, a_im, cmat, s5_d, wglu_bf, s5_b_glu)
        y_d = _hgrn(dd, l, lbs, hg_norm_g, tri, ones_bd)
        xs = _merge(y_a, y_b, y_c, y_d, xs, l, pre_ln, ln_g, ln_b, mod4, merge_g, wout_bf, ln1_g, ln1_b)
        xs = _ffn(xs, l, mod4, wup_bf, conv_w, conv_b, wdn_bf, ln2_g, ln2_b)
    return xs
```

```python
import functools
import math

import jax
import jax.numpy as jnp
from jax import lax
from jax.experimental import pallas as pl
from jax.experimental.pallas import tpu as pltpu

F32 = jnp.float32
BF16 = jnp.bfloat16

D_MODEL = 1024
DEPTH = 2
CHUNK = 64
GROUP_WIDTH = 256
N_MIXERS = 4
GM_HEADS = 4
GM_BLOCK = 128
MLA_HEADS = 4
MLA_NOPE_DIM = 64
MLA_ROPE_DIM = 32
MLA_QK_DIM = MLA_NOPE_DIM + MLA_ROPE_DIM
MLA_V_DIM = 64
MLA_Q_RANK = 192
MLA_KV_RANK = 128
ROPE_THETA = 10000.0
S5_GROUP = 16
S5_GROUPS = 16
S5_STATE = 64
HG_HEADS = 4
HG_DK = 64
D_FF = 2816
CONV_W = 3
DN_ALPHA = (2 * DEPTH) ** 0.25
LN_EPS = 1e-5
RMS_EPS = 1e-6
LOG2E = 1.4426950408889634

LANES = 128
HEAD_PAD = 128
QK_PAD = MLA_HEADS * HEAD_PAD
S5_NSTATE = S5_GROUPS * S5_STATE
HG_SUB = 8

COL_AU, COL_AV, COL_CU, COL_DQ = 0, 256, 512, 768
COL_CQ, COL_CKV, COL_KR = 1792, 2048, 2176
IN_EXT = 2304

FF_TILE = 256
FF_HALO = 16
FF_ROWS = 512
FF_SEGS = 8
FF_DOWN_SPLITS = (4, 8, 10, 11)

VMEM_LIMIT = 56 * 1024 * 1024


def _sigmoid(x):
    return 1.0 / (1.0 + jnp.exp(-x))


def _gelu(x):
    return 0.5 * x * (1.0 + jnp.tanh(math.sqrt(2.0 / math.pi) * (x + 0.044715 * (x * x * x))))


def _split2(x):
    hi = x.astype(BF16)
    lo = (x - hi.astype(F32)).astype(BF16)
    return hi, lo


def _dot(a, b):
    return jnp.dot(a, b, preferred_element_type=F32)


def _dot_nt(a, b):
    return lax.dot_general(a, b, (((1,), (1,)), ((), ())), preferred_element_type=F32)


def _dot_tn(a, b):
    return lax.dot_general(a, b, (((0,), (0,)), ((), ())), preferred_element_type=F32)


def _layer_norm(x, g, b):
    mu = jnp.mean(x, axis=-1, keepdims=True)
    xc = x - mu
    var = jnp.mean(xc * xc, axis=-1, keepdims=True)
    return xc * lax.rsqrt(var + LN_EPS) * g + b


def _layer_spec(stacked, l, **kwargs):
    shape = stacked.shape[1:]
    return pl.BlockSpec((None,) + shape, lambda *_: (l,) + (0,) * len(shape), **kwargs)


def _whole_spec(arr):
    return pl.BlockSpec(arr.shape, lambda *_: (0,) * arr.ndim)


def _mod_spec(mod4, l):
    return pl.BlockSpec((None, 1, 1, mod4.shape[-1]), lambda b, *_: (l, b, 0, 0))


def _cparams(sem, vmem=VMEM_LIMIT):
    return pltpu.CompilerParams(dimension_semantics=sem, vmem_limit_bytes=vmem)


def _ada_kernel(c_ref, w_ref, b_ref, o_ref):
    c = c_ref[...]
    chi, clo = _split2(c * _sigmoid(c))
    whi, wlo = _split2(w_ref[0])
    acc = _dot(chi, whi) + _dot(clo, whi) + _dot(chi, wlo)
    o_ref[0] = acc + b_ref[0]


def _ada(c, w_ada, b_ada):
    nl, d, n = w_ada.shape
    tn = 3072
    bsz = c.shape[0]
    return pl.pallas_call(
        _ada_kernel,
        out_shape=jax.ShapeDtypeStruct((nl, bsz, n), F32),
        grid=(nl, n // tn),
        in_specs=[pl.BlockSpec((bsz, d), lambda l, j: (0, 0)),
                  pl.BlockSpec((1, d, tn), lambda l, j: (l, 0, j)),
                  pl.BlockSpec((1, 1, tn), lambda l, j: (l, 0, j))],
        out_specs=pl.BlockSpec((1, bsz, tn), lambda l, j: (l, 0, j)),
        compiler_params=_cparams(("arbitrary", "arbitrary")),
        name="ada_mod",
    )(c, w_ada, b_ada.reshape(nl, 1, n))


def _inproj_kernel(x_ref, lng_ref, lnb_ref, mod_ref, w_ref, gmg_ref, gmb_ref, gmw_ref, gmbias_ref,
                   qn_ref, kvn_ref, wq_ref, wkv_ref, tq_c_ref, tq_s_ref, tk_ref,
                   ya_ref, cu_ref, d_ref, q_ref, k_ref, v_ref, *, tm, pre_ln, l):
    x = x_ref[0]
    if pre_ln:
        x = _layer_norm(x, lng_ref[...], lnb_ref[...])
    sh1 = mod_ref[0, :, 0:D_MODEL]
    sc1 = mod_ref[0, :, D_MODEL:2 * D_MODEL]
    h = (x * (1.0 + sc1) + sh1).astype(BF16)
    z = _dot(h, w_ref[...])

    cu_ref[...] = z[:, COL_CU:COL_CU + GROUP_WIDTH]
    d_ref[0] = z[:, COL_DQ:COL_DQ + 4 * GROUP_WIDTH]

    lane_head = lax.broadcasted_iota(jnp.int32, (GM_BLOCK, GROUP_WIDTH), 1) // (GROUP_WIDTH // GM_HEADS)
    for r in range(tm // GM_BLOCK):
        rows = slice(r * GM_BLOCK, (r + 1) * GM_BLOCK)
        u = _gelu(z[rows, COL_AU:COL_AU + GROUP_WIDTH])
        v = _gelu(z[rows, COL_AV:COL_AV + GROUP_WIDTH])
        vn = _layer_norm(v, gmg_ref[l:l + 1, :], gmb_ref[l:l + 1, :])
        vn_heads = jnp.concatenate([jnp.where(lane_head == hh, vn, 0.0) for hh in range(GM_HEADS)],
                                   axis=0).astype(BF16)
        ya_ref[0, rows, :] = u * (_dot(gmw_ref[...], vn_heads) + gmbias_ref[...])

    cq = z[:, COL_CQ:COL_CQ + 256]
    cqn = cq * lax.rsqrt(jnp.sum(cq * cq, axis=-1, keepdims=True) * (1.0 / MLA_Q_RANK) + RMS_EPS)
    cqn = (cqn * qn_ref[...]).astype(BF16)
    qq = _dot(cqn, wq_ref[...])
    tq_c = jnp.concatenate([tq_c_ref[...]] * MLA_HEADS, axis=1)
    tq_s = jnp.concatenate([tq_s_ref[...]] * MLA_HEADS, axis=1)
    q_ref[0] = (qq[:, :QK_PAD] * tq_c + qq[:, QK_PAD:] * tq_s).astype(BF16)

    ckv = z[:, COL_CKV:COL_CKV + MLA_KV_RANK]
    ckvn = ckv * lax.rsqrt(jnp.mean(ckv * ckv, axis=-1, keepdims=True) + RMS_EPS)
    ckvn = (ckvn * kvn_ref[...]).astype(BF16)
    krt = (z[:, COL_KR:COL_KR + 128] * tk_ref[...]).astype(BF16)
    kv = _dot(jnp.concatenate([ckvn, krt], axis=1), wkv_ref[...])
    k_ref[0] = kv[:, :QK_PAD].astype(BF16)
    one_col = (lax.broadcasted_iota(jnp.int32, (tm, HEAD_PAD - MLA_V_DIM), 1) == 0).astype(F32)
    for hh in range(MLA_HEADS):
        vh = kv[:, QK_PAD + hh * MLA_V_DIM:QK_PAD + (hh + 1) * MLA_V_DIM]
        v_ref[0, hh] = jnp.concatenate([vh, one_col], axis=1).astype(BF16)


def _inproj(x, l, pre_ln, ln_g, ln_b, mod4, w_ext, gm_g, gm_b, gm_w, gm_bias, qn, kvn, wq_ext, wkv_ext,
            tq_c, tq_s, tk):
    bsz, L, d = x.shape
    tm = 1024
    per_layer = lambda a: _layer_spec(a, l)
    const = lambda shape: pl.BlockSpec(shape, lambda b, t: (0,) * len(shape))
    out_shape = (
        jax.ShapeDtypeStruct((bsz, L, GROUP_WIDTH), F32),
        jax.ShapeDtypeStruct((L, bsz * GROUP_WIDTH), F32),
        jax.ShapeDtypeStruct((bsz, L, 4 * GROUP_WIDTH), F32),
        jax.ShapeDtypeStruct((bsz, L, QK_PAD), BF16),
        jax.ShapeDtypeStruct((bsz, L, QK_PAD), BF16),
        jax.ShapeDtypeStruct((bsz, MLA_HEADS, L, HEAD_PAD), BF16),
    )
    row = lambda w: pl.BlockSpec((1, tm, w), lambda b, t: (b, t, 0))
    return pl.pallas_call(
        functools.partial(_inproj_kernel, tm=tm, pre_ln=pre_ln, l=l),
        out_shape=out_shape,
        grid=(bsz, L // tm),
        in_specs=[row(d), const((1, d)), const((1, d)), _mod_spec(mod4, l),
                  per_layer(w_ext), _whole_spec(gm_g), _whole_spec(gm_b), per_layer(gm_w), per_layer(gm_bias),
                  per_layer(qn), per_layer(kvn), per_layer(wq_ext), per_layer(wkv_ext),
                  pl.BlockSpec((tm, HEAD_PAD), lambda b, t: (t, 0)),
                  pl.BlockSpec((tm, HEAD_PAD), lambda b, t: (t, 0)),
                  pl.BlockSpec((tm, 128), lambda b, t: (t, 0))],
        out_specs=(row(GROUP_WIDTH), pl.BlockSpec((tm, GROUP_WIDTH), lambda b, t: (t, b)),
                   row(4 * GROUP_WIDTH), row(QK_PAD), row(QK_PAD),
                   pl.BlockSpec((1, MLA_HEADS, tm, HEAD_PAD), lambda b, t: (b, 0, t, 0))),
        compiler_params=_cparams(("arbitrary", "arbitrary")),
        name="inproj",
    )(x, ln_g, ln_b, mod4, w_ext, gm_g, gm_b, gm_w, gm_bias, qn, kvn, wq_ext, wkv_ext, tq_c, tq_s, tk)


ATT_TQ = 512
ATT_HEADS_PER_STEP = 4


def _attn_kernel(q_ref, k_ref, v_ref, o_ref):
    L = q_ref.shape[1]
    tq = ATT_TQ
    q_chunk = lax.broadcasted_iota(jnp.int32, (tq, tq), 0) // CHUNK
    k_chunk = lax.broadcasted_iota(jnp.int32, (tq, tq), 1) // CHUNK
    allowed = k_chunk <= q_chunk
    nq = L // tq
    order = [t for pair in zip(range(nq // 2), range(nq - 1, nq // 2 - 1, -1)) for t in pair] + (
        [nq // 2] if nq % 2 else [])
    for qi in order:
        lk = (qi + 1) * tq
        outs = []
        for hh in range(ATT_HEADS_PER_STEP):
            lanes = slice(hh * HEAD_PAD, (hh + 1) * HEAD_PAD)
            qh = q_ref[0, qi * tq:(qi + 1) * tq, lanes]
            s = _dot_nt(qh, k_ref[0, 0:lk, lanes])
            s_diag = jnp.where(allowed, s[:, lk - tq:], -1e30)
            s = s_diag if qi == 0 else jnp.concatenate([s[:, :lk - tq], s_diag], axis=1)
            m = jnp.max(s, axis=-1, keepdims=True)
            p = jnp.exp2(s - m)
            pv = _dot(p.astype(BF16), v_ref[0, hh, 0:lk, :])
            outs.append(pv[:, 0:MLA_V_DIM] / pv[:, MLA_V_DIM:MLA_V_DIM + 1])
        o_ref[0, qi * tq:(qi + 1) * tq, :] = jnp.concatenate(outs, axis=1)


def _attention(q, k, v):
    bsz, L, _ = q.shape
    hp = ATT_HEADS_PER_STEP
    return pl.pallas_call(
        _attn_kernel,
        out_shape=jax.ShapeDtypeStruct((bsz, L, GROUP_WIDTH), F32),
        grid=(bsz, MLA_HEADS // hp),
        in_specs=[pl.BlockSpec((1, L, hp * HEAD_PAD), lambda b, g: (b, 0, g)),
                  pl.BlockSpec((1, L, hp * HEAD_PAD), lambda b, g: (b, 0, g)),
                  pl.BlockSpec((1, hp, L, HEAD_PAD), lambda b, g: (b, g, 0, 0))],
        out_specs=pl.BlockSpec((1, L, hp * MLA_V_DIM), lambda b, g: (b, 0, g)),
        compiler_params=_cparams(("arbitrary", "arbitrary")),
        name="mla_attention",
    )(q, k, v)


def _s5_kernel(u_ref, bmat_ref, are_ref, aim_ref, cmat_ref, dvec_ref, wglu_ref, bglu_ref,
               o_ref, h_ref, us_ref, bu_ref, hs_ref, os_ref, *, tc, nb, l):
    @pl.when(pl.program_id(0) == 0)
    def _():
        h_ref[...] = jnp.zeros(h_ref.shape, F32)

    nslab = GROUP_WIDTH // LANES
    for bb in range(nb):
        for s in range(nslab):
            lanes = slice(bb * GROUP_WIDTH + s * LANES, bb * GROUP_WIDTH + (s + 1) * LANES)
            us_ref[s, pl.ds(bb, tc, stride=nb), :] = u_ref[:, lanes]
    u = jnp.concatenate([us_ref[s] for s in range(nslab)], axis=1)
    bu_ref[...] = _dot(u.astype(BF16), bmat_ref[...])
    a_re = jnp.broadcast_to(are_ref[...], (nb, S5_NSTATE))
    a_im = jnp.broadcast_to(aim_ref[...], (nb, S5_NSTATE))
    re = slice(0, S5_NSTATE)
    im = slice(S5_NSTATE, 2 * S5_NSTATE)

    def step(i, carry):
        h_re, h_im = carry
        r0 = pl.ds(pl.multiple_of(i * 2 * nb, 2 * nb), nb)
        r1 = pl.ds(pl.multiple_of(i * 2 * nb, 2 * nb) + nb, nb)
        m_re = a_re * h_re - a_im * h_im + bu_ref[r0, re]
        m_im = a_re * h_im + a_im * h_re + bu_ref[r0, im]
        n_re = a_re * m_re - a_im * m_im + bu_ref[r1, re]
        n_im = a_re * m_im + a_im * m_re + bu_ref[r1, im]
        rr = pl.ds(pl.multiple_of(i * 2 * nb, 2 * nb), 2 * nb)
        hs_ref[rr, re] = jnp.concatenate([m_re, n_re], axis=0).astype(BF16)
        hs_ref[rr, im] = jnp.concatenate([m_im, n_im], axis=0).astype(BF16)
        return n_re, n_im

    h_re, h_im = lax.fori_loop(0, tc // 2, step, (h_ref[:, re], h_ref[:, im]))
    h_ref[:, re] = h_re
    h_ref[:, im] = h_im

    half = (tc * nb) // 2
    y = jnp.concatenate([_dot(hs_ref[0:half, :], cmat_ref[...]), _dot(hs_ref[half:2 * half, :], cmat_ref[...])],
                        axis=0) + dvec_ref[l:l + 1, :] * u
    z = _gelu(y)
    z_bf = z.astype(BF16)
    gate = _sigmoid(jnp.concatenate([_dot(z_bf[0:half], wglu_ref[...]), _dot(z_bf[half:2 * half], wglu_ref[...])],
                                    axis=0) + bglu_ref[l:l + 1, :])
    res = z * gate
    for s in range(nslab):
        os_ref[s] = res[:, s * LANES:(s + 1) * LANES]
    for bb in range(nb):
        for s in range(nslab):
            lanes = slice(bb * GROUP_WIDTH + s * LANES, bb * GROUP_WIDTH + (s + 1) * LANES)
            o_ref[:, lanes] = os_ref[s, pl.ds(bb, tc, stride=nb), :]


def _s5(u_t, nb, l, bmat, a_re, a_im, cmat, dvec, wglu, bglu):
    L = u_t.shape[0]
    gw = GROUP_WIDTH
    tc = 256
    per_layer = lambda a: _layer_spec(a, l)
    return pl.pallas_call(
        functools.partial(_s5_kernel, tc=tc, nb=nb, l=l),
        out_shape=jax.ShapeDtypeStruct((L, nb * gw), F32),
        grid=(L // tc,),
        in_specs=[pl.BlockSpec((tc, nb * gw), lambda t: (t, 0)),
                  per_layer(bmat), per_layer(a_re), per_layer(a_im), per_layer(cmat), _whole_spec(dvec),
                  per_layer(wglu), _whole_spec(bglu)],
        out_specs=pl.BlockSpec((tc, nb * gw), lambda t: (t, 0)),
        scratch_shapes=[pltpu.VMEM((nb, 2 * S5_NSTATE), F32),
                        pltpu.VMEM((gw // LANES, tc * nb, LANES), F32),
                        pltpu.VMEM((tc * nb, 2 * S5_NSTATE), F32),
                        pltpu.VMEM((tc * nb, 2 * S5_NSTATE), BF16),
                        pltpu.VMEM((gw // LANES, tc * nb, LANES), F32)],
        compiler_params=_cparams(("arbitrary",)),
        name="s5_scan",
    )(u_t, bmat, a_re, a_im, cmat, dvec, wglu, bglu)


HG_BATCH = 2


def _hgrn_chunk(q, fl, iv, lb, tri, ones_bd, st):
    nsub = CHUNK // HG_SUB
    f = lb + (1.0 - lb) * _sigmoid(fl)
    logf = jnp.log(f)
    kk = 1.0 - f
    hi, lo = _split2(logf)
    b2 = (_dot(tri, hi) + _dot(tri, lo)) * LOG2E
    lane_head = lax.broadcasted_iota(jnp.int32, (HG_SUB, GROUP_WIDTH), 1) // HG_DK
    row_id = lax.broadcasted_iota(jnp.int32, (HG_SUB, GROUP_WIDTH), 0)
    v_bf = iv.astype(BF16)

    terms = []
    for si in range(nsub):
        r0 = si * HG_SUB
        q_i, k_i, b_i = (a[r0:r0 + HG_SUB] for a in (q, kk, b2))
        for j in range(HG_SUB):
            e = jnp.exp2(jnp.where(row_id >= j, b_i - b_i[j:j + 1], -1e30))
            terms.append(q_i * k_i[j:j + 1] * e)
    p = _dot(jnp.concatenate(terms, axis=0).astype(BF16), ones_bd)
    o_parts = []
    for si in range(nsub):
        r0 = si * HG_SUB
        base = si * HG_SUB * HG_SUB
        o_i = p[base:base + HG_SUB] * iv[r0:r0 + 1]
        for j in range(1, HG_SUB):
            o_i = o_i + p[base + j * HG_SUB:base + (j + 1) * HG_SUB] * iv[r0 + j:r0 + j + 1]
        o_parts.append(o_i)

    atts = []
    for si in range(1, nsub):
        r0 = si * HG_SUB
        b_edge = b2[r0 - 1:r0]
        q_t = q[r0:r0 + HG_SUB] * jnp.exp2(b2[r0:r0 + HG_SUB] - b_edge)
        k_t = jnp.concatenate([kk[0:r0] * jnp.exp2(b_edge - b2[0:r0]),
                               jnp.zeros((CHUNK - r0, GROUP_WIDTH), F32)], axis=0).astype(BF16)
        q_exp = jnp.concatenate([jnp.where(lane_head == hh, q_t, 0.0) for hh in range(HG_HEADS)],
                                axis=0).astype(BF16)
        atts.append(_dot_nt(q_exp, k_t))
    o2 = _dot(jnp.concatenate(atts, axis=0).astype(BF16), v_bf)
    for si in range(1, nsub):
        base = (si - 1) * HG_HEADS * HG_SUB
        for hh in range(HG_HEADS):
            o_parts[si] = o_parts[si] + jnp.where(lane_head == hh,
                                                  o2[base + hh * HG_SUB:base + (hh + 1) * HG_SUB], 0.0)
    o_intra = jnp.concatenate(o_parts, axis=0)

    o_inter = _dot_nt((q * jnp.exp2(b2)).astype(BF16), st.astype(BF16))
    b_last = b2[CHUNK - 1:CHUNK]
    k_hat = (kk * jnp.exp2(b_last - b2)).astype(BF16)
    upd = _dot_tn(v_bf, k_hat)
    bd = (lax.broadcasted_iota(jnp.int32, (GROUP_WIDTH, GROUP_WIDTH), 0) // HG_DK
          == lax.broadcasted_iota(jnp.int32, (GROUP_WIDTH, GROUP_WIDTH), 1) // HG_DK)
    st_new = st * jnp.exp2(b_last) + jnp.where(bd, upd, 0.0)
    return o_intra + o_inter, st_new


def _hgrn_kernel(d_ref, lb_ref, gn_ref, tri_ref, ones_ref, o_ref, st_ref, *, th, l):
    @pl.when(pl.program_id(1) == 0)
    def _():
        st_ref[...] = jnp.zeros(st_ref.shape, F32)

    lb = lb_ref[l:l + 1, :]
    tri = tri_ref[...]
    ones_bd = ones_ref[...]
    for c in range(th // CHUNK):
        rows = slice(c * CHUNK, (c + 1) * CHUNK)
        for bb in range(d_ref.shape[0]):
            q = d_ref[bb, rows, 0:GROUP_WIDTH]
            fl = d_ref[bb, rows, GROUP_WIDTH:2 * GROUP_WIDTH]
            iv = d_ref[bb, rows, 2 * GROUP_WIDTH:3 * GROUP_WIDTH]
            g = d_ref[bb, rows, 3 * GROUP_WIDTH:4 * GROUP_WIDTH]
            o, st_new = _hgrn_chunk(q, fl, iv, lb, tri, ones_bd, st_ref[bb])
            st_ref[bb] = st_new
            ms = _dot((o * o).astype(BF16), ones_bd) * (1.0 / HG_DK)
            o = o * lax.rsqrt(ms + RMS_EPS) * gn_ref[l:l + 1, :]
            o_ref[bb, rows, :] = o * _sigmoid(g)


def _hgrn(d, l, lb, gn, tri, ones_bd):
    bsz, L, _ = d.shape
    th = 512
    nbb = HG_BATCH if bsz % HG_BATCH == 0 else 1
    const = lambda shape: pl.BlockSpec(shape, lambda b, t: (0,) * len(shape))
    return pl.pallas_call(
        functools.partial(_hgrn_kernel, th=th, l=l),
        out_shape=jax.ShapeDtypeStruct((bsz, L, GROUP_WIDTH), F32),
        grid=(bsz // nbb, L // th),
        in_specs=[pl.BlockSpec((nbb, th, 4 * GROUP_WIDTH), lambda b, t: (b, t, 0)),
                  _whole_spec(lb), _whole_spec(gn),
                  const((CHUNK, CHUNK)), const((GROUP_WIDTH, GROUP_WIDTH))],
        out_specs=pl.BlockSpec((nbb, th, GROUP_WIDTH), lambda b, t: (b, t, 0)),
        scratch_shapes=[pltpu.VMEM((nbb, GROUP_WIDTH, GROUP_WIDTH), F32)],
        compiler_params=_cparams(("arbitrary", "arbitrary")),
        name="hgrn2",
    )(d, lb, gn, tri, ones_bd)


def _merge_kernel(ya_ref, yb_ref, yc_ref, yd_ref, x_ref, lng_ref, lnb_ref, mod_ref, mg_ref, wo_ref, g_ref, b_ref,
                  o_ref, *, pre_ln, l):
    acc = None
    for gi, y_ref in enumerate((ya_ref, yb_ref, yc_ref, yd_ref)):
        y = y_ref[...] if gi == 2 else y_ref[0]
        yn = y * lax.rsqrt(jnp.mean(y * y, axis=-1, keepdims=True) + RMS_EPS)
        yn = (yn * mg_ref[l:l + 1, gi * GROUP_WIDTH:(gi + 1) * GROUP_WIDTH]).astype(BF16)
        part = _dot(yn, wo_ref[gi * GROUP_WIDTH:(gi + 1) * GROUP_WIDTH, :])
        acc = part if acc is None else acc + part
    g1 = mod_ref[0, :, 2 * D_MODEL:3 * D_MODEL]
    x = x_ref[0]
    if pre_ln:
        x = _layer_norm(x, lng_ref[...], lnb_ref[...])
    o_ref[0] = _layer_norm(DN_ALPHA * x + (1.0 + g1) * acc, g_ref[l:l + 1, :], b_ref[l:l + 1, :])


def _merge(ya, yb, yc, yd, x, l, pre_ln, ln_g, ln_b, mod4, mg, wo, g, b):
    bsz, L, d = x.shape
    tm = 1024
    per_layer = lambda a: _layer_spec(a, l)
    const = lambda shape: pl.BlockSpec(shape, lambda bb, t: (0,) * len(shape))
    row = lambda w: pl.BlockSpec((1, tm, w), lambda bb, t: (bb, t, 0))
    return pl.pallas_call(
        functools.partial(_merge_kernel, pre_ln=pre_ln, l=l),
        out_shape=jax.ShapeDtypeStruct((bsz, L, d), F32),
        grid=(bsz, L // tm),
        in_specs=[row(GROUP_WIDTH), row(GROUP_WIDTH), pl.BlockSpec((tm, GROUP_WIDTH), lambda bb, t: (t, bb)),
                  row(GROUP_WIDTH), row(d), const((1, d)), const((1, d)), _mod_spec(mod4, l),
                  _whole_spec(mg), per_layer(wo), _whole_spec(g), _whole_spec(b)],
        out_specs=row(d),
        compiler_params=_cparams(("arbitrary", "arbitrary")),
        name="merge_out",
    )(ya, yb, yc, yd, x, ln_g, ln_b, mod4, mg, wo, g, b)


def _ffn_kernel(x_ref, xh_ref, mod_ref, wup_ref, cw_ref, cb_ref, wdn_ref, g_ref, b_ref, o_ref,
                hp_ref, hx_ref, a_ref, act_ref, ap_ref, *, tm, l):
    ti = pl.program_id(1)
    seg = tm // FF_SEGS
    nslab = D_MODEL // LANES
    sh2 = mod_ref[0, :, 3 * D_MODEL:4 * D_MODEL]
    sc2 = mod_ref[0, :, 4 * D_MODEL:5 * D_MODEL]
    g2 = mod_ref[0, :, 5 * D_MODEL:6 * D_MODEL]
    x = x_ref[0]
    h2 = x * (1.0 + sc2) + sh2
    halo = jnp.where(ti > 0, xh_ref[0] * (1.0 + sc2) + sh2, 0.0)
    for s in range(nslab):
        lanes = slice(s * LANES, (s + 1) * LANES)
        hp_ref[s, 0:FF_HALO, :] = halo[:, lanes]
        for p in range(FF_SEGS):
            hp_ref[s, pl.ds(FF_HALO + p, seg, stride=FF_SEGS), :] = h2[p * seg:(p + 1) * seg, lanes]
    hx_ref[...] = jnp.concatenate([hp_ref[s] for s in range(nslab)], axis=1).astype(BF16)
    nj = D_FF // FF_TILE
    sub0 = lax.broadcasted_iota(jnp.int32, (FF_SEGS, FF_TILE), 0) == 0

    def up(j):
        hx = hx_ref[...]
        for half in range(2):
            a = a_ref.at[j % 2, half]
            cols = slice(half * D_FF + j * FF_TILE, half * D_FF + (j + 1) * FF_TILE)
            a[...] = _dot(hx, wup_ref[:, cols])
            prev1 = jnp.where(sub0, a[FF_HALO - 1:FF_HALO, :], pltpu.roll(a[FF_HALO + tm - 8:FF_HALO + tm, :], 1, 0))
            prev2 = jnp.where(sub0, a[FF_HALO - 2:FF_HALO - 1, :],
                              pltpu.roll(a[FF_HALO + tm - 16:FF_HALO + tm - 8, :], 1, 0))
            a[8:16, :] = prev1
            a[0:8, :] = prev2

    def conv(j, half):
        cols = slice(half * D_FF + j * FF_TILE, half * D_FF + (j + 1) * FF_TILE)
        a = a_ref.at[j % 2, half]
        return (a[0:tm, :] * cw_ref[0:1, cols] + a[8:tm + 8, :] * cw_ref[1:2, cols]
                + a[16:tm + 16, :] * cw_ref[2:3, cols] + cb_ref[l:l + 1, cols])

    up(0)
    acc = None
    for j in range(nj):
        if j + 1 < nj:
            up(j + 1)
        gate = conv(j, 1)
        act_ref[:, j * FF_TILE:(j + 1) * FF_TILE] = (conv(j, 0) * (gate * _sigmoid(gate))).astype(BF16)
        if j + 1 in FF_DOWN_SPLITS:
            k0 = ([0] + [e for e in FF_DOWN_SPLITS if e <= j])[-1] * FF_TILE
            k1 = (j + 1) * FF_TILE
            part = _dot(act_ref[:, k0:k1], wdn_ref[k0:k1, :])
            acc = part if acc is None else acc + part
    for s in range(nslab):
        ap_ref[s] = acc[:, s * LANES:(s + 1) * LANES]
    acc_nat = jnp.concatenate(
        [jnp.concatenate([ap_ref[s, pl.ds(p, seg, stride=FF_SEGS), :] for p in range(FF_SEGS)], axis=0)
         for s in range(nslab)], axis=1)
    o_ref[0] = _layer_norm(DN_ALPHA * x + (1.0 + g2) * acc_nat, g_ref[l:l + 1, :], b_ref[l:l + 1, :])


def _ffn(x, l, mod4, wup, cw, cb, wdn, g, b):
    bsz, L, d = x.shape
    tm = FF_ROWS
    per_layer = lambda a: _layer_spec(a, l, pipeline_mode=pl.Buffered(1))
    hb = tm // FF_HALO
    return pl.pallas_call(
        functools.partial(_ffn_kernel, tm=tm, l=l),
        out_shape=jax.ShapeDtypeStruct((bsz, L, d), F32),
        grid=(bsz, L // tm),
        in_specs=[pl.BlockSpec((1, tm, d), lambda bb, t: (bb, t, 0)),
                  pl.BlockSpec((1, FF_HALO, d), lambda bb, t: (bb, jnp.maximum(t * hb - 1, 0), 0)),
                  _mod_spec(mod4, l),
                  per_layer(wup), per_layer(cw), _whole_spec(cb), per_layer(wdn), _whole_spec(g), _whole_spec(b)],
        out_specs=pl.BlockSpec((1, tm, d), lambda bb, t: (bb, t, 0)),
        scratch_shapes=[pltpu.VMEM((d // LANES, tm + FF_HALO, LANES), F32),
                        pltpu.VMEM((tm + FF_HALO, d), BF16),
                        pltpu.VMEM((2, 2, tm + FF_HALO, FF_TILE), F32),
                        pltpu.VMEM((tm, D_FF), BF16),
                        pltpu.VMEM((d // LANES, tm, LANES), F32)],
        compiler_params=_cparams(("arbitrary", "arbitrary")),
        name="conv_ffn",
    )(x, x, mod4, wup, cw, cb, wdn, g, b)


def _rope_tables(L):
    inv = 1.0 / (ROPE_THETA ** (jnp.arange(0, MLA_ROPE_DIM, 2, dtype=F32) / MLA_ROPE_DIM))
    ang = jnp.arange(L, dtype=F32)[:, None] * inv[None, :]
    cos, sin = jnp.cos(ang), jnp.sin(ang)
    cc = jnp.concatenate([cos, cos], axis=1)
    ss = jnp.concatenate([-sin, sin], axis=1)
    scale = MLA_QK_DIM ** -0.5 * LOG2E
    zeros32 = jnp.zeros((L, HEAD_PAD - MLA_QK_DIM), F32)
    tq_c = jnp.concatenate([jnp.ones((L, MLA_NOPE_DIM), F32), cc, zeros32], axis=1) * scale
    tq_s = jnp.concatenate([jnp.zeros((L, MLA_NOPE_DIM), F32), ss, zeros32], axis=1) * scale
    tk = jnp.concatenate([cc, ss, jnp.zeros((L, 64), F32)], axis=1)
    return tq_c, tq_s, tk


def _swap_halves(w):
    half = MLA_ROPE_DIM // 2
    return jnp.concatenate([w[..., half:], w[..., :half]], axis=-1)


def _prep_inproj_weights(w_in_l, w_uq_l, w_ukv_l, q_norm_l, kv_norm_l):
    d = w_in_l.shape[0]
    a_u, a_v, cq, ckv, kr, c_u, d_q, d_f, d_i, d_g = jnp.split(
        w_in_l, (256, 512, 704, 832, 864, 1120, 1376, 1632, 1888), axis=1)
    z64 = jnp.zeros((d, 64), F32)
    w_ext = jnp.concatenate([a_u, a_v, c_u, d_q, d_f, d_i, d_g, cq, z64, ckv, kr, _swap_halves(kr), z64],
                            axis=1).astype(BF16)
    wq = w_uq_l.reshape(MLA_Q_RANK, MLA_HEADS, MLA_QK_DIM)
    zq32 = jnp.zeros((MLA_Q_RANK, MLA_HEADS, HEAD_PAD - MLA_QK_DIM), F32)
    zq64 = jnp.zeros((MLA_Q_RANK, MLA_HEADS, MLA_NOPE_DIM), F32)
    wq_a = jnp.concatenate([wq, zq32], axis=2).reshape(MLA_Q_RANK, QK_PAD)
    wq_b = jnp.concatenate([zq64, _swap_halves(wq[:, :, MLA_NOPE_DIM:]), zq32], axis=2).reshape(MLA_Q_RANK, QK_PAD)
    wq_ext = jnp.concatenate([wq_a, wq_b], axis=1)
    wq_ext = jnp.concatenate([wq_ext, jnp.zeros((256 - MLA_Q_RANK, 2 * QK_PAD), F32)], axis=0).astype(BF16)
    qn = jnp.concatenate([q_norm_l, jnp.zeros((256 - MLA_Q_RANK,), F32)]).reshape(1, 256)
    wkv = w_ukv_l.reshape(MLA_KV_RANK, MLA_HEADS, MLA_NOPE_DIM + MLA_V_DIM)
    zk64 = jnp.zeros((MLA_KV_RANK, MLA_HEADS, HEAD_PAD - MLA_NOPE_DIM), F32)
    wk = jnp.concatenate([wkv[:, :, :MLA_NOPE_DIM], zk64], axis=2).reshape(MLA_KV_RANK, QK_PAD)
    wv = wkv[:, :, MLA_NOPE_DIM:].reshape(MLA_KV_RANK, MLA_HEADS * MLA_V_DIM)
    eye = jnp.eye(MLA_ROPE_DIM, dtype=F32)
    place_h = jnp.concatenate([jnp.zeros((MLA_ROPE_DIM, MLA_NOPE_DIM), F32), eye,
                               jnp.zeros((MLA_ROPE_DIM, HEAD_PAD - MLA_QK_DIM), F32)], axis=1)
    place = jnp.concatenate([place_h] * MLA_HEADS, axis=1)
    place = jnp.concatenate([place, jnp.zeros((MLA_ROPE_DIM, MLA_HEADS * MLA_V_DIM), F32)], axis=1)
    top = jnp.concatenate([wk, wv], axis=1)
    wkv_ext = jnp.concatenate([top, place, place, jnp.zeros((64, QK_PAD + 256), F32)], axis=0).astype(BF16)
    kvn = kv_norm_l.reshape(1, MLA_KV_RANK)
    return w_ext, wq_ext, wkv_ext, qn, kvn


def _prep_gm(gm_ws_l, gm_bs_l):
    ck = jnp.arange(GM_BLOCK) // CHUNK
    mask = ck[None, :] <= ck[:, None]
    w = jnp.where(mask[None], gm_ws_l, 0)
    w = jnp.transpose(w, (1, 0, 2)).reshape(GM_BLOCK, GM_HEADS * GM_BLOCK).astype(BF16)
    bias = jnp.repeat(gm_bs_l.T, GROUP_WIDTH // GM_HEADS, axis=1)
    return w, bias


def _prep_s5(a_re, a_im, b_re, b_im, c_re, c_im, log_step):
    dt = jnp.exp(log_step)[:, None]
    mag = jnp.exp(dt * a_re)
    abar_re, abar_im = mag * jnp.cos(dt * a_im), mag * jnp.sin(dt * a_im)
    den = a_re * a_re + a_im * a_im
    nr, ni = abar_re - 1.0, abar_im
    zoh_re = (nr * a_re + ni * a_im) / den
    zoh_im = (ni * a_re - nr * a_im) / den
    bbar_re = zoh_re[..., None] * b_re - zoh_im[..., None] * b_im
    bbar_im = zoh_re[..., None] * b_im + zoh_im[..., None] * b_re
    eye_g = jnp.eye(S5_GROUPS, dtype=F32)
    bm_re = jnp.einsum('gpn,gh->gnhp', bbar_re, eye_g).reshape(GROUP_WIDTH, S5_NSTATE)
    bm_im = jnp.einsum('gpn,gh->gnhp', bbar_im, eye_g).reshape(GROUP_WIDTH, S5_NSTATE)
    bmat = jnp.concatenate([bm_re, bm_im], axis=1).astype(BF16)
    cm_re = jnp.einsum('gnp,gh->gphn', c_re, eye_g).reshape(S5_NSTATE, GROUP_WIDTH)
    cm_im = jnp.einsum('gnp,gh->gphn', c_im, eye_g).reshape(S5_NSTATE, GROUP_WIDTH)
    cmat = jnp.concatenate([cm_re, -cm_im], axis=0).astype(BF16)
    return bmat, abar_re.reshape(1, S5_NSTATE), abar_im.reshape(1, S5_NSTATE), cmat


def kernel(x, c, ln_in_g, ln_in_b, hg_lb_logits, w_ada, b_ada, w_in, gm_ln_g, gm_ln_b, gm_ws, gm_bs, mla_q_norm, mla_kv_norm, mla_w_uq, mla_w_ukv, s5_a_re, s5_a_im, s5_b_re, s5_b_im, s5_c_re, s5_c_im, s5_d, s5_log_step, s5_w_glu, s5_b_glu, hg_norm_g, merge_g, w_out, ln1_g, ln1_b, w_up, conv_w, conv_b, w_down, ln2_g, ln2_b):
    bsz, L, d = x.shape
    nl = w_in.shape[0]
    mod = _ada(c, w_ada, b_ada)
    lb_cum = jnp.cumsum(jax.nn.softmax(hg_lb_logits.astype(F32), axis=0), axis=0)
    tq_c, tq_s, tk = _rope_tables(L)
    tri = jnp.tril(jnp.ones((CHUNK, CHUNK), F32)).astype(BF16)
    hid = jnp.arange(GROUP_WIDTH) // HG_DK
    ones_bd = (hid[:, None] == hid[None, :]).astype(BF16)
    w_ext, wq_ext, wkv_ext, qn, kvn = jax.vmap(_prep_inproj_weights)(w_in, mla_w_uq, mla_w_ukv, mla_q_norm,
                                                                     mla_kv_norm)
    gm_w, gm_bias = jax.vmap(_prep_gm)(gm_ws, gm_bs)
    bmat, a_re, a_im, cmat = jax.vmap(_prep_s5)(s5_a_re, s5_a_im, s5_b_re, s5_b_im, s5_c_re, s5_c_im, s5_log_step)
    wglu_bf, wout_bf, wup_bf, wdn_bf = (w.astype(BF16) for w in (s5_w_glu, w_out, w_up, w_down))

    mod4 = mod.reshape(nl, bsz, 1, 6 * d)
    lbs = lb_cum - lb_cum[0:1]

    xs = x
    ln_g, ln_b = ln_in_g.reshape(1, d), ln_in_b.reshape(1, d)
    for l in range(nl):
        pre_ln = l == 0
        y_a, c_u, dd, q, k, v = _inproj(xs, l, pre_ln, ln_g, ln_b, mod4, w_ext, gm_ln_g, gm_ln_b, gm_w, gm_bias,
                                        qn, kvn, wq_ext, wkv_ext, tq_c, tq_s, tk)
        y_b = _attention(q, k, v)
        y_c = _s5(c_u, bsz, l, bmat, a_re, a_im, cmat, s5_d, wglu_bf, s5_b_glu)
        y_d = _hgrn(dd, l, lbs, hg_norm_g, tri, ones_bd)
        xs = _merge(y_a, y_b, y_c, y_d, xs, l, pre_ln, ln_g, ln_b, mod4, merge_g, wout_bf, ln1_g, ln1_b)
        xs = _ffn(xs, l, mod4, wup_bf, conv_w, conv_b, wdn_bf, ln2_g, ln2_b)
    return xs
```

```python
import functools
import math

import jax
import jax.numpy as jnp
from jax import lax
from jax.experimental import pallas as pl
from jax.experimental.pallas import tpu as pltpu

F32 = jnp.float32
BF16 = jnp.bfloat16

D_MODEL = 1024
DEPTH = 2
CHUNK = 64
GROUP_WIDTH = 256
N_MIXERS = 4
GM_HEADS = 4
GM_BLOCK = 128
MLA_HEADS = 4
MLA_NOPE_DIM = 64
MLA_ROPE_DIM = 32
MLA_QK_DIM = MLA_NOPE_DIM + MLA_ROPE_DIM
MLA_V_DIM = 64
MLA_Q_RANK = 192
MLA_KV_RANK = 128
ROPE_THETA = 10000.0
S5_GROUP = 16
S5_GROUPS = 16
S5_STATE = 64
HG_HEADS = 4
HG_DK = 64
D_FF = 2816
CONV_W = 3
DN_ALPHA = (2 * DEPTH) ** 0.25
LN_EPS = 1e-5
RMS_EPS = 1e-6
LOG2E = 1.4426950408889634

LANES = 128
HEAD_PAD = 128
QK_PAD = MLA_HEADS * HEAD_PAD
S5_NSTATE = S5_GROUPS * S5_STATE
HG_SUB = 8

COL_AU, COL_AV, COL_CU, COL_DQ = 0, 256, 512, 768
COL_CQ, COL_CKV, COL_KR = 1792, 2048, 2176
IN_EXT = 2304

FF_TILE = 256
FF_HALO = 16
FF_ROWS = 512
FF_SEGS = 8
FF_DOWN_SPLITS = (4, 8, 10, 11)

VMEM_LIMIT = 56 * 1024 * 1024


def _sigmoid(x):
    return 1.0 / (1.0 + jnp.exp(-x))


def _gelu(x):
    return 0.5 * x * (1.0 + jnp.tanh(math.sqrt(2.0 / math.pi) * (x + 0.044715 * (x * x * x))))


def _split2(x):
    hi = x.astype(BF16)
    lo = (x - hi.astype(F32)).astype(BF16)
    return hi, lo


def _dot(a, b):
    return jnp.dot(a, b, preferred_element_type=F32)


def _dot_nt(a, b):
    return lax.dot_general(a, b, (((1,), (1,)), ((), ())), preferred_element_type=F32)


def _dot_tn(a, b):
    return lax.dot_general(a, b, (((0,), (0,)), ((), ())), preferred_element_type=F32)


def _layer_norm(x, g, b):
    mu = jnp.mean(x, axis=-1, keepdims=True)
    xc = x - mu
    var = jnp.mean(xc * xc, axis=-1, keepdims=True)
    return xc * lax.rsqrt(var + LN_EPS) * g + b


def _layer_spec(stacked, l, **kwargs):
    shape = stacked.shape[1:]
    return pl.BlockSpec((None,) + shape, lambda *_: (l,) + (0,) * len(shape), **kwargs)


def _whole_spec(arr):
    return pl.BlockSpec(arr.shape, lambda *_: (0,) * arr.ndim)


def _mod_spec(mod4, l):
    return pl.BlockSpec((None, 1, 1, mod4.shape[-1]), lambda b, *_: (l, b, 0, 0))


def _cparams(sem, vmem=VMEM_LIMIT):
    return pltpu.CompilerParams(dimension_semantics=sem, vmem_limit_bytes=vmem)


def _ada_kernel(c_ref, w_ref, b_ref, o_ref):
    c = c_ref[...]
    chi, clo = _split2(c * _sigmoid(c))
    whi, wlo = _split2(w_ref[0])
    acc = _dot(chi, whi) + _dot(clo, whi) + _dot(chi, wlo)
    o_ref[0] = acc + b_ref[0]


def _ada(c, w_ada, b_ada):
    nl, d, n = w_ada.shape
    tn = 1536
    bsz = c.shape[0]
    return pl.pallas_call(
        _ada_kernel,
        out_shape=jax.ShapeDtypeStruct((nl, bsz, n), F32),
        grid=(nl, n // tn),
        in_specs=[pl.BlockSpec((bsz, d), lambda l, j: (0, 0)),
                  pl.BlockSpec((1, d, tn), lambda l, j: (l, 0, j)),
                  pl.BlockSpec((1, 1, tn), lambda l, j: (l, 0, j))],
        out_specs=pl.BlockSpec((1, bsz, tn), lambda l, j: (l, 0, j)),
        compiler_params=_cparams(("arbitrary", "arbitrary")),
        name="ada_mod",
    )(c, w_ada, b_ada.reshape(nl, 1, n))


def _inproj_kernel(x_ref, lng_ref, lnb_ref, mod_ref, w_ref, gmg_ref, gmb_ref, gmw_ref, gmbias_ref,
                   qn_ref, kvn_ref, wq_ref, wkv_ref, tq_c_ref, tq_s_ref, tk_ref,
                   ya_ref, cu_ref, d_ref, q_ref, k_ref, v_ref, *, tm, pre_ln, l):
    x = x_ref[0]
    if pre_ln:
        x = _layer_norm(x, lng_ref[...], lnb_ref[...])
    sh1 = mod_ref[0, :, 0:D_MODEL]
    sc1 = mod_ref[0, :, D_MODEL:2 * D_MODEL]
    h = (x * (1.0 + sc1) + sh1).astype(BF16)
    z = _dot(h, w_ref[...])

    cu_ref[...] = z[:, COL_CU:COL_CU + GROUP_WIDTH]
    d_ref[0] = z[:, COL_DQ:COL_DQ + 4 * GROUP_WIDTH]

    lane_head = lax.broadcasted_iota(jnp.int32, (GM_BLOCK, GROUP_WIDTH), 1) // (GROUP_WIDTH // GM_HEADS)
    for r in range(tm // GM_BLOCK):
        rows = slice(r * GM_BLOCK, (r + 1) * GM_BLOCK)
        u = _gelu(z[rows, COL_AU:COL_AU + GROUP_WIDTH])
        v = _gelu(z[rows, COL_AV:COL_AV + GROUP_WIDTH])
        vn = _layer_norm(v, gmg_ref[l:l + 1, :], gmb_ref[l:l + 1, :])
        vn_heads = jnp.concatenate([jnp.where(lane_head == hh, vn, 0.0) for hh in range(GM_HEADS)],
                                   axis=0).astype(BF16)
        ya_ref[0, rows, :] = (u * (_dot(gmw_ref[...], vn_heads) + gmbias_ref[...])).astype(BF16)

    cq = z[:, COL_CQ:COL_CQ + 256]
    cqn = cq * lax.rsqrt(jnp.sum(cq * cq, axis=-1, keepdims=True) * (1.0 / MLA_Q_RANK) + RMS_EPS)
    cqn = (cqn * qn_ref[...]).astype(BF16)
    qq = _dot(cqn, wq_ref[...])
    tq_c = jnp.concatenate([tq_c_ref[...]] * MLA_HEADS, axis=1)
    tq_s = jnp.concatenate([tq_s_ref[...]] * MLA_HEADS, axis=1)
    q_ref[0] = (qq[:, :QK_PAD] * tq_c + qq[:, QK_PAD:] * tq_s).astype(BF16)

    ckv = z[:, COL_CKV:COL_CKV + MLA_KV_RANK]
    ckvn = ckv * lax.rsqrt(jnp.mean(ckv * ckv, axis=-1, keepdims=True) + RMS_EPS)
    ckvn = (ckvn * kvn_ref[...]).astype(BF16)
    krt = (z[:, COL_KR:COL_KR + 128] * tk_ref[...]).astype(BF16)
    kv = _dot(jnp.concatenate([ckvn, krt], axis=1), wkv_ref[...])
    k_ref[0] = kv[:, :QK_PAD].astype(BF16)
    one_col = (lax.broadcasted_iota(jnp.int32, (tm, HEAD_PAD - MLA_V_DIM), 1) == 0).astype(F32)
    for hh in range(MLA_HEADS):
        vh = kv[:, QK_PAD + hh * MLA_V_DIM:QK_PAD + (hh + 1) * MLA_V_DIM]
        v_ref[0, hh] = jnp.concatenate([vh, one_col], axis=1).astype(BF16)


def _inproj(x, l, pre_ln, ln_g, ln_b, mod4, w_ext, gm_g, gm_b, gm_w, gm_bias, qn, kvn, wq_ext, wkv_ext,
            tq_c, tq_s, tk):
    bsz, L, d = x.shape
    tm = 1024
    per_layer = lambda a: _layer_spec(a, l)
    const = lambda shape: pl.BlockSpec(shape, lambda b, t: (0,) * len(shape))
    out_shape = (
        jax.ShapeDtypeStruct((bsz, L, GROUP_WIDTH), BF16),
        jax.ShapeDtypeStruct((L, bsz * GROUP_WIDTH), F32),
        jax.ShapeDtypeStruct((bsz, L, 4 * GROUP_WIDTH), F32),
        jax.ShapeDtypeStruct((bsz, L, QK_PAD), BF16),
        jax.ShapeDtypeStruct((bsz, L, QK_PAD), BF16),
        jax.ShapeDtypeStruct((bsz, MLA_HEADS, L, HEAD_PAD), BF16),
    )
    row = lambda w: pl.BlockSpec((1, tm, w), lambda b, t: (b, t, 0))
    return pl.pallas_call(
        functools.partial(_inproj_kernel, tm=tm, pre_ln=pre_ln, l=l),
        out_shape=out_shape,
        grid=(bsz, L // tm),
        in_specs=[row(d), const((1, d)), const((1, d)), _mod_spec(mod4, l),
                  per_layer(w_ext), _whole_spec(gm_g), _whole_spec(gm_b), per_layer(gm_w), per_layer(gm_bias),
                  per_layer(qn), per_layer(kvn), per_layer(wq_ext), per_layer(wkv_ext),
                  pl.BlockSpec((tm, HEAD_PAD), lambda b, t: (t, 0)),
                  pl.BlockSpec((tm, HEAD_PAD), lambda b, t: (t, 0)),
                  pl.BlockSpec((tm, 128), lambda b, t: (t, 0))],
        out_specs=(row(GROUP_WIDTH), pl.BlockSpec((tm, GROUP_WIDTH), lambda b, t: (t, b)),
                   row(4 * GROUP_WIDTH), row(QK_PAD), row(QK_PAD),
                   pl.BlockSpec((1, MLA_HEADS, tm, HEAD_PAD), lambda b, t: (b, 0, t, 0))),
        compiler_params=_cparams(("arbitrary", "arbitrary")),
        name="inproj",
    )(x, ln_g, ln_b, mod4, w_ext, gm_g, gm_b, gm_w, gm_bias, qn, kvn, wq_ext, wkv_ext, tq_c, tq_s, tk)


ATT_TQ = 512
ATT_HEADS_PER_STEP = 4


def _attn_kernel(q_ref, k_ref, v_ref, o_ref):
    L = q_ref.shape[1]
    tq = ATT_TQ
    q_chunk = lax.broadcasted_iota(jnp.int32, (tq, tq), 0) // CHUNK
    k_chunk = lax.broadcasted_iota(jnp.int32, (tq, tq), 1) // CHUNK
    allowed = k_chunk <= q_chunk
    nq = L // tq
    order = [t for pair in zip(range(nq // 2), range(nq - 1, nq // 2 - 1, -1)) for t in pair] + (
        [nq // 2] if nq % 2 else [])
    for qi in order:
        lk = (qi + 1) * tq
        outs = []
        for hh in range(ATT_HEADS_PER_STEP):
            lanes = slice(hh * HEAD_PAD, (hh + 1) * HEAD_PAD)
            qh = q_ref[0, qi * tq:(qi + 1) * tq, lanes]
            s = _dot_nt(qh, k_ref[0, 0:lk, lanes])
            s_diag = jnp.where(allowed, s[:, lk - tq:], -1e30)
            s = s_diag if qi == 0 else jnp.concatenate([s[:, :lk - tq], s_diag], axis=1)
            m = jnp.max(s, axis=-1, keepdims=True)
            p = jnp.exp2(s - m)
            pv = _dot(p.astype(BF16), v_ref[0, hh, 0:lk, :])
            outs.append(pv[:, 0:MLA_V_DIM] / pv[:, MLA_V_DIM:MLA_V_DIM + 1])
        o_ref[0, qi * tq:(qi + 1) * tq, :] = jnp.concatenate(outs, axis=1).astype(BF16)


def _attention(q, k, v):
    bsz, L, _ = q.shape
    hp = ATT_HEADS_PER_STEP
    return pl.pallas_call(
        _attn_kernel,
        out_shape=jax.ShapeDtypeStruct((bsz, L, GROUP_WIDTH), BF16),
        grid=(bsz, MLA_HEADS // hp),
        in_specs=[pl.BlockSpec((1, L, hp * HEAD_PAD), lambda b, g: (b, 0, g)),
                  pl.BlockSpec((1, L, hp * HEAD_PAD), lambda b, g: (b, 0, g)),
                  pl.BlockSpec((1, hp, L, HEAD_PAD), lambda b, g: (b, g, 0, 0))],
        out_specs=pl.BlockSpec((1, L, hp * MLA_V_DIM), lambda b, g: (b, 0, g)),
        compiler_params=_cparams(("arbitrary", "arbitrary")),
        name="mla_attention",
    )(q, k, v)


def _s5_kernel(u_ref, bmat_ref, are_ref, aim_ref, cmat_ref, dvec_ref, wglu_ref, bglu_ref,
               o_ref, h_ref, us_ref, bu_ref, hs_ref, os_ref, *, tc, nb, l):
    @pl.when(pl.program_id(0) == 0)
    def _():
        h_ref[...] = jnp.zeros(h_ref.shape, F32)

    nslab = GROUP_WIDTH // LANES
    for bb in range(nb):
        for s in range(nslab):
            lanes = slice(bb * GROUP_WIDTH + s * LANES, bb * GROUP_WIDTH + (s + 1) * LANES)
            us_ref[s, pl.ds(bb, tc, stride=nb), :] = u_ref[:, lanes]
    u = jnp.concatenate([us_ref[s] for s in range(nslab)], axis=1)
    bu_ref[...] = _dot(u.astype(BF16), bmat_ref[...])
    a_re = jnp.broadcast_to(are_ref[...], (nb, S5_NSTATE))
    a_im = jnp.broadcast_to(aim_ref[...], (nb, S5_NSTATE))
    re = slice(0, S5_NSTATE)
    im = slice(S5_NSTATE, 2 * S5_NSTATE)

    def step(i, carry):
        h_re, h_im = carry
        r0 = pl.ds(pl.multiple_of(i * 2 * nb, 2 * nb), nb)
        r1 = pl.ds(pl.multiple_of(i * 2 * nb, 2 * nb) + nb, nb)
        m_re = a_re * h_re - a_im * h_im + bu_ref[r0, re]
        m_im = a_re * h_im + a_im * h_re + bu_ref[r0, im]
        n_re = a_re * m_re - a_im * m_im + bu_ref[r1, re]
        n_im = a_re * m_im + a_im * m_re + bu_ref[r1, im]
        rr = pl.ds(pl.multiple_of(i * 2 * nb, 2 * nb), 2 * nb)
        hs_ref[rr, re] = jnp.concatenate([m_re, n_re], axis=0).astype(BF16)
        hs_ref[rr, im] = jnp.concatenate([m_im, n_im], axis=0).astype(BF16)
        return n_re, n_im

    h_re, h_im = lax.fori_loop(0, tc // 2, step, (h_ref[:, re], h_ref[:, im]))
    h_ref[:, re] = h_re
    h_ref[:, im] = h_im

    half = (tc * nb) // 2
    y = jnp.concatenate([_dot(hs_ref[0:half, :], cmat_ref[...]), _dot(hs_ref[half:2 * half, :], cmat_ref[...])],
                        axis=0) + dvec_ref[l:l + 1, :] * u
    z = _gelu(y)
    z_bf = z.astype(BF16)
    gate = _sigmoid(jnp.concatenate([_dot(z_bf[0:half], wglu_ref[...]), _dot(z_bf[half:2 * half], wglu_ref[...])],
                                    axis=0) + bglu_ref[l:l + 1, :])
    res = z * gate
    for s in range(nslab):
        os_ref[s] = res[:, s * LANES:(s + 1) * LANES]
    for bb in range(nb):
        for s in range(nslab):
            lanes = slice(bb * GROUP_WIDTH + s * LANES, bb * GROUP_WIDTH + (s + 1) * LANES)
            o_ref[:, lanes] = os_ref[s, pl.ds(bb, tc, stride=nb), :].astype(BF16)


def _s5(u_t, nb, l, bmat, a_re, a_im, cmat, dvec, wglu, bglu):
    L = u_t.shape[0]
    gw = GROUP_WIDTH
    tc = 256
    per_layer = lambda a: _layer_spec(a, l)
    return pl.pallas_call(
        functools.partial(_s5_kernel, tc=tc, nb=nb, l=l),
        out_shape=jax.ShapeDtypeStruct((L, nb * gw), BF16),
        grid=(L // tc,),
        in_specs=[pl.BlockSpec((tc, nb * gw), lambda t: (t, 0)),
                  per_layer(bmat), per_layer(a_re), per_layer(a_im), per_layer(cmat), _whole_spec(dvec),
                  per_layer(wglu), _whole_spec(bglu)],
        out_specs=pl.BlockSpec((tc, nb * gw), lambda t: (t, 0)),
        scratch_shapes=[pltpu.VMEM((nb, 2 * S5_NSTATE), F32),
                        pltpu.VMEM((gw // LANES, tc * nb, LANES), F32),
                        pltpu.VMEM((tc * nb, 2 * S5_NSTATE), F32),
                        pltpu.VMEM((tc * nb, 2 * S5_NSTATE), BF16),
                        pltpu.VMEM((gw // LANES, tc * nb, LANES), F32)],
        compiler_params=_cparams(("arbitrary",)),
        name="s5_scan",
    )(u_t, bmat, a_re, a_im, cmat, dvec, wglu, bglu)


HG_BATCH = 2


def _hgrn_chunk(q, fl, iv, lb, tri, ones_bd, st):
    nsub = CHUNK // HG_SUB
    f = lb + (1.0 - lb) * _sigmoid(fl)
    logf = jnp.log(f)
    kk = 1.0 - f
    hi, lo = _split2(logf)
    b2 = (_dot(tri, hi) + _dot(tri, lo)) * LOG2E
    lane_head = lax.broadcasted_iota(jnp.int32, (HG_SUB, GROUP_WIDTH), 1) // HG_DK
    row_id = lax.broadcasted_iota(jnp.int32, (HG_SUB, GROUP_WIDTH), 0)
    v_bf = iv.astype(BF16)

    terms = []
    for si in range(nsub):
        r0 = si * HG_SUB
        q_i, k_i, b_i = (a[r0:r0 + HG_SUB] for a in (q, kk, b2))
        for j in range(HG_SUB):
            e = jnp.exp2(jnp.where(row_id >= j, b_i - b_i[j:j + 1], -1e30))
            terms.append(q_i * k_i[j:j + 1] * e)
    p = _dot(jnp.concatenate(terms, axis=0).astype(BF16), ones_bd)
    o_parts = []
    for si in range(nsub):
        r0 = si * HG_SUB
        base = si * HG_SUB * HG_SUB
        o_i = p[base:base + HG_SUB] * iv[r0:r0 + 1]
        for j in range(1, HG_SUB):
            o_i = o_i + p[base + j * HG_SUB:base + (j + 1) * HG_SUB] * iv[r0 + j:r0 + j + 1]
        o_parts.append(o_i)

    atts = []
    for si in range(1, nsub):
        r0 = si * HG_SUB
        b_edge = b2[r0 - 1:r0]
        q_t = q[r0:r0 + HG_SUB] * jnp.exp2(b2[r0:r0 + HG_SUB] - b_edge)
        k_t = jnp.concatenate([kk[0:r0] * jnp.exp2(b_edge - b2[0:r0]),
                               jnp.zeros((CHUNK - r0, GROUP_WIDTH), F32)], axis=0).astype(BF16)
        q_exp = jnp.concatenate([jnp.where(lane_head == hh, q_t, 0.0) for hh in range(HG_HEADS)],
                                axis=0).astype(BF16)
        atts.append(_dot_nt(q_exp, k_t))
    o2 = _dot(jnp.concatenate(atts, axis=0).astype(BF16), v_bf)
    for si in range(1, nsub):
        base = (si - 1) * HG_HEADS * HG_SUB
        for hh in range(HG_HEADS):
            o_parts[si] = o_parts[si] + jnp.where(lane_head == hh,
                                                  o2[base + hh * HG_SUB:base + (hh + 1) * HG_SUB], 0.0)
    o_intra = jnp.concatenate(o_parts, axis=0)

    o_inter = _dot_nt((q * jnp.exp2(b2)).astype(BF16), st.astype(BF16))
    b_last = b2[CHUNK - 1:CHUNK]
    k_hat = (kk * jnp.exp2(b_last - b2)).astype(BF16)
    upd = _dot_tn(v_bf, k_hat)
    bd = (lax.broadcasted_iota(jnp.int32, (GROUP_WIDTH, GROUP_WIDTH), 0) // HG_DK
          == lax.broadcasted_iota(jnp.int32, (GROUP_WIDTH, GROUP_WIDTH), 1) // HG_DK)
    st_new = st * jnp.exp2(b_last) + jnp.where(bd, upd, 0.0)
    return o_intra + o_inter, st_new


def _hgrn_kernel(d_ref, lb_ref, gn_ref, tri_ref, ones_ref, o_ref, st_ref, *, th, l):
    @pl.when(pl.program_id(1) == 0)
    def _():
        st_ref[...] = jnp.zeros(st_ref.shape, F32)

    lb = lb_ref[l:l + 1, :]
    tri = tri_ref[...]
    ones_bd = ones_ref[...]
    for c in range(th // CHUNK):
        rows = slice(c * CHUNK, (c + 1) * CHUNK)
        for bb in range(d_ref.shape[0]):
            q = d_ref[bb, rows, 0:GROUP_WIDTH]
            fl = d_ref[bb, rows, GROUP_WIDTH:2 * GROUP_WIDTH]
            iv = d_ref[bb, rows, 2 * GROUP_WIDTH:3 * GROUP_WIDTH]
            g = d_ref[bb, rows, 3 * GROUP_WIDTH:4 * GROUP_WIDTH]
            o, st_new = _hgrn_chunk(q, fl, iv, lb, tri, ones_bd, st_ref[bb])
            st_ref[bb] = st_new
            ms = _dot((o * o).astype(BF16), ones_bd) * (1.0 / HG_DK)
            o = o * lax.rsqrt(ms + RMS_EPS) * gn_ref[l:l + 1, :]
            o_ref[bb, rows, :] = (o * _sigmoid(g)).astype(BF16)


def _hgrn(d, l, lb, gn, tri, ones_bd):
    bsz, L, _ = d.shape
    th = 512
    nbb = HG_BATCH if bsz % HG_BATCH == 0 else 1
    const = lambda shape: pl.BlockSpec(shape, lambda b, t: (0,) * len(shape))
    return pl.pallas_call(
        functools.partial(_hgrn_kernel, th=th, l=l),
        out_shape=jax.ShapeDtypeStruct((bsz, L, GROUP_WIDTH), BF16),
        grid=(bsz // nbb, L // th),
        in_specs=[pl.BlockSpec((nbb, th, 4 * GROUP_WIDTH), lambda b, t: (b, t, 0)),
                  _whole_spec(lb), _whole_spec(gn),
                  const((CHUNK, CHUNK)), const((GROUP_WIDTH, GROUP_WIDTH))],
        out_specs=pl.BlockSpec((nbb, th, GROUP_WIDTH), lambda b, t: (b, t, 0)),
        scratch_shapes=[pltpu.VMEM((nbb, GROUP_WIDTH, GROUP_WIDTH), F32)],
        compiler_params=_cparams(("arbitrary", "arbitrary")),
        name="hgrn2",
    )(d, lb, gn, tri, ones_bd)


def _merge_kernel(ya_ref, yb_ref, yc_ref, yd_ref, x_ref, lng_ref, lnb_ref, mod_ref, mg_ref, wo_ref, g_ref, b_ref,
                  o_ref, *, pre_ln, l):
    acc = None
    for gi, y_ref in enumerate((ya_ref, yb_ref, yc_ref, yd_ref)):
        y = (y_ref[...] if gi == 2 else y_ref[0]).astype(F32)
        yn = y * lax.rsqrt(jnp.mean(y * y, axis=-1, keepdims=True) + RMS_EPS)
        yn = (yn * mg_ref[l:l + 1, gi * GROUP_WIDTH:(gi + 1) * GROUP_WIDTH]).astype(BF16)
        part = _dot(yn, wo_ref[gi * GROUP_WIDTH:(gi + 1) * GROUP_WIDTH, :])
        acc = part if acc is None else acc + part
    g1 = mod_ref[0, :, 2 * D_MODEL:3 * D_MODEL]
    x = x_ref[0]
    if pre_ln:
        x = _layer_norm(x, lng_ref[...], lnb_ref[...])
    o_ref[0] = _layer_norm(DN_ALPHA * x + (1.0 + g1) * acc, g_ref[l:l + 1, :], b_ref[l:l + 1, :])


def _merge(ya, yb, yc, yd, x, l, pre_ln, ln_g, ln_b, mod4, mg, wo, g, b):
    bsz, L, d = x.shape
    tm = 1024
    per_layer = lambda a: _layer_spec(a, l)
    const = lambda shape: pl.BlockSpec(shape, lambda bb, t: (0,) * len(shape))
    row = lambda w: pl.BlockSpec((1, tm, w), lambda bb, t: (bb, t, 0))
    return pl.pallas_call(
        functools.partial(_merge_kernel, pre_ln=pre_ln, l=l),
        out_shape=jax.ShapeDtypeStruct((bsz, L, d), F32),
        grid=(bsz, L // tm),
        in_specs=[row(GROUP_WIDTH), row(GROUP_WIDTH), pl.BlockSpec((tm, GROUP_WIDTH), lambda bb, t: (t, bb)),
                  row(GROUP_WIDTH), row(d), const((1, d)), const((1, d)), _mod_spec(mod4, l),
                  _whole_spec(mg), per_layer(wo), _whole_spec(g), _whole_spec(b)],
        out_specs=row(d),
        compiler_params=_cparams(("arbitrary", "arbitrary")),
        name="merge_out",
    )(ya, yb, yc, yd, x, ln_g, ln_b, mod4, mg, wo, g, b)


def _ffn_kernel(x_ref, xh_ref, mod_ref, wup_ref, cw_ref, cb_ref, wdn_ref, g_ref, b_ref, o_ref,
                hp_ref, hx_ref, a_ref, act_ref, ap_ref, *, tm, l):
    ti = pl.program_id(1)
    seg = tm // FF_SEGS
    nslab = D_MODEL // LANES
    sh2 = mod_ref[0, :, 3 * D_MODEL:4 * D_MODEL]
    sc2 = mod_ref[0, :, 4 * D_MODEL:5 * D_MODEL]
    g2 = mod_ref[0, :, 5 * D_MODEL:6 * D_MODEL]
    x = x_ref[0]
    h2 = x * (1.0 + sc2) + sh2
    halo = jnp.where(ti > 0, xh_ref[0] * (1.0 + sc2) + sh2, 0.0)
    for s in range(nslab):
        lanes = slice(s * LANES, (s + 1) * LANES)
        hp_ref[s, 0:FF_HALO, :] = halo[:, lanes]
        for p in range(FF_SEGS):
            hp_ref[s, pl.ds(FF_HALO + p, seg, stride=FF_SEGS), :] = h2[p * seg:(p + 1) * seg, lanes]
    hx_ref[...] = jnp.concatenate([hp_ref[s] for s in range(nslab)], axis=1).astype(BF16)
    nj = D_FF // FF_TILE
    sub0 = lax.broadcasted_iota(jnp.int32, (FF_SEGS, FF_TILE), 0) == 0

    def up(j):
        hx = hx_ref[...]
        for half in range(2):
            a = a_ref.at[j % 2, half]
            cols = slice(half * D_FF + j * FF_TILE, half * D_FF + (j + 1) * FF_TILE)
            a[...] = _dot(hx, wup_ref[:, cols])
            prev1 = jnp.where(sub0, a[FF_HALO - 1:FF_HALO, :], pltpu.roll(a[FF_HALO + tm - 8:FF_HALO + tm, :], 1, 0))
            prev2 = jnp.where(sub0, a[FF_HALO - 2:FF_HALO - 1, :],
                              pltpu.roll(a[FF_HALO + tm - 16:FF_HALO + tm - 8, :], 1, 0))
            a[8:16, :] = prev1
            a[0:8, :] = prev2

    def conv(j, half):
        cols = slice(half * D_FF + j * FF_TILE, half * D_FF + (j + 1) * FF_TILE)
        a = a_ref.at[j % 2, half]
        return (a[0:tm, :] * cw_ref[0:1, cols] + a[8:tm + 8, :] * cw_ref[1:2, cols]
                + a[16:tm + 16, :] * cw_ref[2:3, cols] + cb_ref[l:l + 1, cols])

    up(0)
    acc = None
    for j in range(nj):
        if j + 1 < nj:
            up(j + 1)
        gate = conv(j, 1)
        act_ref[:, j * FF_TILE:(j + 1) * FF_TILE] = (conv(j, 0) * (gate * _sigmoid(gate))).astype(BF16)
        if j + 1 in FF_DOWN_SPLITS:
            k0 = ([0] + [e for e in FF_DOWN_SPLITS if e <= j])[-1] * FF_TILE
            k1 = (j + 1) * FF_TILE
            part = _dot(act_ref[:, k0:k1], wdn_ref[k0:k1, :])
            acc = part if acc is None else acc + part
    for s in range(nslab):
        ap_ref[s] = acc[:, s * LANES:(s + 1) * LANES]
    acc_nat = jnp.concatenate(
        [jnp.concatenate([ap_ref[s, pl.ds(p, seg, stride=FF_SEGS), :] for p in range(FF_SEGS)], axis=0)
         for s in range(nslab)], axis=1)
    o_ref[0] = _layer_norm(DN_ALPHA * x + (1.0 + g2) * acc_nat, g_ref[l:l + 1, :], b_ref[l:l + 1, :])


def _ffn(x, l, mod4, wup, cw, cb, wdn, g, b):
    bsz, L, d = x.shape
    tm = FF_ROWS
    per_layer = lambda a: _layer_spec(a, l, pipeline_mode=pl.Buffered(1))
    hb = tm // FF_HALO
    return pl.pallas_call(
        functools.partial(_ffn_kernel, tm=tm, l=l),
        out_shape=jax.ShapeDtypeStruct((bsz, L, d), F32),
        grid=(bsz, L // tm),
        in_specs=[pl.BlockSpec((1, tm, d), lambda bb, t: (bb, t, 0)),
                  pl.BlockSpec((1, FF_HALO, d), lambda bb, t: (bb, jnp.maximum(t * hb - 1, 0), 0)),
                  _mod_spec(mod4, l),
                  per_layer(wup), per_layer(cw), _whole_spec(cb), per_layer(wdn), _whole_spec(g), _whole_spec(b)],
        out_specs=pl.BlockSpec((1, tm, d), lambda bb, t: (bb, t, 0)),
        scratch_shapes=[pltpu.VMEM((d // LANES, tm + FF_HALO, LANES), F32),
                        pltpu.VMEM((tm + FF_HALO, d), BF16),
                        pltpu.VMEM((2, 2, tm + FF_HALO, FF_TILE), F32),
                        pltpu.VMEM((tm, D_FF), BF16),
                        pltpu.VMEM((d // LANES, tm, LANES), F32)],
        compiler_params=_cparams(("arbitrary", "arbitrary")),
        name="conv_ffn",
    )(x, x, mod4, wup, cw, cb, wdn, g, b)


def _rope_tables(L):
    inv = 1.0 / (ROPE_THETA ** (jnp.arange(0, MLA_ROPE_DIM, 2, dtype=F32) / MLA_ROPE_DIM))
    ang = jnp.arange(L, dtype=F32)[:, None] * inv[None, :]
    cos, sin = jnp.cos(ang), jnp.sin(ang)
    cc = jnp.concatenate([cos, cos], axis=1)
    ss = jnp.concatenate([-sin, sin], axis=1)
    scale = MLA_QK_DIM ** -0.5 * LOG2E
    zeros32 = jnp.zeros((L, HEAD_PAD - MLA_QK_DIM), F32)
    tq_c = jnp.concatenate([jnp.ones((L, MLA_NOPE_DIM), F32), cc, zeros32], axis=1) * scale
    tq_s = jnp.concatenate([jnp.zeros((L, MLA_NOPE_DIM), F32), ss, zeros32], axis=1) * scale
    tk = jnp.concatenate([cc, ss, jnp.zeros((L, 64), F32)], axis=1)
    return tq_c, tq_s, tk


def _swap_halves(w):
    half = MLA_ROPE_DIM // 2
    return jnp.concatenate([w[..., half:], w[..., :half]], axis=-1)


def _prep_inproj_weights(w_in_l, w_uq_l, w_ukv_l, q_norm_l, kv_norm_l):
    d = w_in_l.shape[0]
    a_u, a_v, cq, ckv, kr, c_u, d_q, d_f, d_i, d_g = jnp.split(
        w_in_l, (256, 512, 704, 832, 864, 1120, 1376, 1632, 1888), axis=1)
    z64 = jnp.zeros((d, 64), F32)
    w_ext = jnp.concatenate([a_u, a_v, c_u, d_q, d_f, d_i, d_g, cq, z64, ckv, kr, _swap_halves(kr), z64],
                            axis=1).astype(BF16)
    wq = w_uq_l.reshape(MLA_Q_RANK, MLA_HEADS, MLA_QK_DIM)
    zq32 = jnp.zeros((MLA_Q_RANK, MLA_HEADS, HEAD_PAD - MLA_QK_DIM), F32)
    zq64 = jnp.zeros((MLA_Q_RANK, MLA_HEADS, MLA_NOPE_DIM), F32)
    wq_a = jnp.concatenate([wq, zq32], axis=2).reshape(MLA_Q_RANK, QK_PAD)
    wq_b = jnp.concatenate([zq64, _swap_halves(wq[:, :, MLA_NOPE_DIM:]), zq32], axis=2).reshape(MLA_Q_RANK, QK_PAD)
    wq_ext = jnp.concatenate([wq_a, wq_b], axis=1)
    wq_ext = jnp.concatenate([wq_ext, jnp.zeros((256 - MLA_Q_RANK, 2 * QK_PAD), F32)], axis=0).astype(BF16)
    qn = jnp.concatenate([q_norm_l, jnp.zeros((256 - MLA_Q_RANK,), F32)]).reshape(1, 256)
    wkv = w_ukv_l.reshape(MLA_KV_RANK, MLA_HEADS, MLA_NOPE_DIM + MLA_V_DIM)
    zk64 = jnp.zeros((MLA_KV_RANK, MLA_HEADS, HEAD_PAD - MLA_NOPE_DIM), F32)
    wk = jnp.concatenate([wkv[:, :, :MLA_NOPE_DIM], zk64], axis=2).reshape(MLA_KV_RANK, QK_PAD)
    wv = wkv[:, :, MLA_NOPE_DIM:].reshape(MLA_KV_RANK, MLA_HEADS * MLA_V_DIM)
    eye = jnp.eye(MLA_ROPE_DIM, dtype=F32)
    place_h = jnp.concatenate([jnp.zeros((MLA_ROPE_DIM, MLA_NOPE_DIM), F32), eye,
                               jnp.zeros((MLA_ROPE_DIM, HEAD_PAD - MLA_QK_DIM), F32)], axis=1)
    place = jnp.concatenate([place_h] * MLA_HEADS, axis=1)
    place = jnp.concatenate([place, jnp.zeros((MLA_ROPE_DIM, MLA_HEADS * MLA_V_DIM), F32)], axis=1)
    top = jnp.concatenate([wk, wv], axis=1)
    wkv_ext = jnp.concatenate([top, place, place, jnp.zeros((64, QK_PAD + 256), F32)], axis=0).astype(BF16)
    kvn = kv_norm_l.reshape(1, MLA_KV_RANK)
    return w_ext, wq_ext, wkv_ext, qn, kvn


def _prep_gm(gm_ws_l, gm_bs_l):
    ck = jnp.arange(GM_BLOCK) // CHUNK
    mask = ck[None, :] <= ck[:, None]
    w = jnp.where(mask[None], gm_ws_l, 0)
    w = jnp.transpose(w, (1, 0, 2)).reshape(GM_BLOCK, GM_HEADS * GM_BLOCK).astype(BF16)
    bias = jnp.repeat(gm_bs_l.T, GROUP_WIDTH // GM_HEADS, axis=1)
    return w, bias


def _prep_s5(a_re, a_im, b_re, b_im, c_re, c_im, log_step):
    dt = jnp.exp(log_step)[:, None]
    mag = jnp.exp(dt * a_re)
    abar_re, abar_im = mag * jnp.cos(dt * a_im), mag * jnp.sin(dt * a_im)
    den = a_re * a_re + a_im * a_im
    nr, ni = abar_re - 1.0, abar_im
    zoh_re = (nr * a_re + ni * a_im) / den
    zoh_im = (ni * a_re - nr * a_im) / den
    bbar_re = zoh_re[..., None] * b_re - zoh_im[..., None] * b_im
    bbar_im = zoh_re[..., None] * b_im + zoh_im[..., None] * b_re
    eye_g = jnp.eye(S5_GROUPS, dtype=F32)
    bm_re = jnp.einsum('gpn,gh->gnhp', bbar_re, eye_g).reshape(GROUP_WIDTH, S5_NSTATE)
    bm_im = jnp.einsum('gpn,gh->gnhp', bbar_im, eye_g).reshape(GROUP_WIDTH, S5_NSTATE)
    bmat = jnp.concatenate([bm_re, bm_im], axis=1).astype(BF16)
    cm_re = jnp.einsum('gnp,gh->gphn', c_re, eye_g).reshape(S5_NSTATE, GROUP_WIDTH)
    cm_im = jnp.einsum('gnp,gh->gphn', c_im, eye_g).reshape(S5_NSTATE, GROUP_WIDTH)
    cmat = jnp.concatenate([cm_re, -cm_im], axis=0).astype(BF16)
    return bmat, abar_re.reshape(1, S5_NSTATE), abar_im.reshape(1, S5_NSTATE), cmat


def kernel(x, c, ln_in_g, ln_in_b, hg_lb_logits, w_ada, b_ada, w_in, gm_ln_g, gm_ln_b, gm_ws, gm_bs, mla_q_norm, mla_kv_norm, mla_w_uq, mla_w_ukv, s5_a_re, s5_a_im, s5_b_re, s5_b_im, s5_c_re, s5_c_im, s5_d, s5_log_step, s5_w_glu, s5_b_glu, hg_norm_g, merge_g, w_out, ln1_g, ln1_b, w_up, conv_w, conv_b, w_down, ln2_g, ln2_b):
    bsz, L, d = x.shape
    nl = w_in.shape[0]
    mod = _ada(c, w_ada, b_ada)
    lb_cum = jnp.cumsum(jax.nn.softmax(hg_lb_logits.astype(F32), axis=0), axis=0)
    tq_c, tq_s, tk = _rope_tables(L)
    tri = jnp.tril(jnp.ones((CHUNK, CHUNK), F32)).astype(BF16)
    hid = jnp.arange(GROUP_WIDTH) // HG_DK
    ones_bd = (hid[:, None] == hid[None, :]).astype(BF16)
    w_ext, wq_ext, wkv_ext, qn, kvn = jax.vmap(_prep_inproj_weights)(w_in, mla_w_uq, mla_w_ukv, mla_q_norm,
                                                                     mla_kv_norm)
    gm_w, gm_bias = jax.vmap(_prep_gm)(gm_ws, gm_bs)
    bmat, a_re, a_im, cmat = jax.vmap(_prep_s5)(s5_a_re, s5_a_im, s5_b_re, s5_b_im, s5_c_re, s5_c_im, s5_log_step)
    wglu_bf, wout_bf, wup_bf, wdn_bf = (w.astype(BF16) for w in (s5_w_glu, w_out, w_up, w_down))

    mod4 = mod.reshape(nl, bsz, 1, 6 * d)
    lbs = lb_cum - lb_cum[0:1]

    xs = x
    ln_g, ln_b = ln_in_g.reshape(1, d), ln_in_b.reshape(1, d)
    for l in range(nl):
        pre_ln = l == 0
        y_a, c_u, dd, q, k, v = _inproj(xs, l, pre_ln, ln_g, ln_b, mod4, w_ext, gm_ln_g, gm_ln_b, gm_w, gm_bias,
                                        qn, kvn, wq_ext, wkv_ext, tq_c, tq_s, tk)
        y_b = _attention(q, k, v)
        y_c = _s5(c_u, bsz, l, bmat, a_re, a_im, cmat, s5_d, wglu_bf, s5_b_glu)
        y_d = _hgrn(dd, l, lbs, hg_norm_g, tri, ones_bd)
        xs = _merge(y_a, y_b, y_c, y_d, xs, l, pre_ln, ln_g, ln_b, mod4, merge_g, wout_bf, ln1_g, ln1_b)
        xs = _ffn(xs, l, mod4, wup_bf, conv_w, conv_b, wdn_bf, ln2_g, ln2_b)
    return xs
```

```python
import functools
import math

import jax
import jax.numpy as jnp
from jax import lax
from jax.experimental import pallas as pl
from jax.experimental.pallas import tpu as pltpu

F32 = jnp.float32
BF16 = jnp.bfloat16

D_MODEL = 1024
DEPTH = 2
CHUNK = 64
GROUP_WIDTH = 256
N_MIXERS = 4
GM_HEADS = 4
GM_BLOCK = 128
MLA_HEADS = 4
MLA_NOPE_DIM = 64
MLA_ROPE_DIM = 32
MLA_QK_DIM = MLA_NOPE_DIM + MLA_ROPE_DIM
MLA_V_DIM = 64
MLA_Q_RANK = 192
MLA_KV_RANK = 128
ROPE_THETA = 10000.0
S5_GROUP = 16
S5_GROUPS = 16
S5_STATE = 64
HG_HEADS = 4
HG_DK = 64
D_FF = 2816
CONV_W = 3
DN_ALPHA = (2 * DEPTH) ** 0.25
LN_EPS = 1e-5
RMS_EPS = 1e-6
LOG2E = 1.4426950408889634

LANES = 128
HEAD_PAD = 128
QK_PAD = MLA_HEADS * HEAD_PAD
S5_NSTATE = S5_GROUPS * S5_STATE
HG_SUB = 8

COL_AU, COL_AV, COL_CU, COL_DQ = 0, 256, 512, 768
COL_CQ, COL_CKV, COL_KR = 1792, 2048, 2176
IN_EXT = 2304

ADA_COLS = 1536
INPROJ_ROWS = 1024
ATT_TQ = 512
ATT_HEADS_PER_STEP = 4
S5_CHUNK = 256
HG_ROWS = 512
HG_BATCH = 2
MERGE_ROWS = 1024
FF_ROWS = 512
FF_TILE = 256
FF_HALO = 16
FF_SEGS = 8
FF_DOWN_SPLITS = (4, 8, 10, 11)

VMEM_LIMIT = 56 * 1024 * 1024


def _sigmoid(x):
    return 1.0 / (1.0 + jnp.exp(-x))


def _gelu(x):
    return 0.5 * x * (1.0 + jnp.tanh(math.sqrt(2.0 / math.pi) * (x + 0.044715 * (x * x * x))))


def _split2(x):
    hi = x.astype(BF16)
    lo = (x - hi.astype(F32)).astype(BF16)
    return hi, lo


def _dot(a, b):
    return jnp.dot(a, b, preferred_element_type=F32)


def _dot_nt(a, b):
    return lax.dot_general(a, b, (((1,), (1,)), ((), ())), preferred_element_type=F32)


def _dot_tn(a, b):
    return lax.dot_general(a, b, (((0,), (0,)), ((), ())), preferred_element_type=F32)


def _layer_norm(x, g, b):
    mu = jnp.mean(x, axis=-1, keepdims=True)
    xc = x - mu
    var = jnp.mean(xc * xc, axis=-1, keepdims=True)
    return xc * lax.rsqrt(var + LN_EPS) * g + b


def _layer_spec(stacked, l, **kwargs):
    shape = stacked.shape[1:]
    return pl.BlockSpec((None,) + shape, lambda *_: (l,) + (0,) * len(shape), **kwargs)


def _whole_spec(arr):
    return pl.BlockSpec(arr.shape, lambda *_: (0,) * arr.ndim)


def _mod_spec(mod4, l):
    return pl.BlockSpec((None, 1, 1, mod4.shape[-1]), lambda b, *_: (l, b, 0, 0))


def _cparams(sem, vmem=VMEM_LIMIT):
    return pltpu.CompilerParams(dimension_semantics=sem, vmem_limit_bytes=vmem)


def _ada_kernel(c_ref, w_ref, b_ref, o_ref):
    c = c_ref[...]
    chi, clo = _split2(c * _sigmoid(c))
    whi, wlo = _split2(w_ref[0])
    acc = _dot(chi, whi) + _dot(clo, whi) + _dot(chi, wlo)
    o_ref[0] = acc + b_ref[0]


def _ada(c, w_ada, b_ada):
    nl, d, n = w_ada.shape
    tn = ADA_COLS
    bsz = c.shape[0]
    return pl.pallas_call(
        _ada_kernel,
        out_shape=jax.ShapeDtypeStruct((nl, bsz, n), F32),
        grid=(nl, n // tn),
        in_specs=[pl.BlockSpec((bsz, d), lambda l, j: (0, 0)),
                  pl.BlockSpec((1, d, tn), lambda l, j: (l, 0, j)),
                  pl.BlockSpec((1, 1, tn), lambda l, j: (l, 0, j))],
        out_specs=pl.BlockSpec((1, bsz, tn), lambda l, j: (l, 0, j)),
        compiler_params=_cparams(("arbitrary", "arbitrary")),
        name="ada_mod",
    )(c, w_ada, b_ada.reshape(nl, 1, n))


def _inproj_kernel(x_ref, lng_ref, lnb_ref, mod_ref, w_ref, gmg_ref, gmb_ref, gmw_ref, gmbias_ref,
                   qn_ref, kvn_ref, wq_ref, wkv_ref, tq_c_ref, tq_s_ref, tk_ref,
                   ya_ref, cu_ref, d_ref, q_ref, k_ref, v_ref, *, tm, pre_ln, l):
    x = x_ref[0]
    if pre_ln:
        x = _layer_norm(x, lng_ref[...], lnb_ref[...])
    sh1 = mod_ref[0, :, 0:D_MODEL]
    sc1 = mod_ref[0, :, D_MODEL:2 * D_MODEL]
    h = (x * (1.0 + sc1) + sh1).astype(BF16)
    z = _dot(h, w_ref[...])

    cu_ref[...] = z[:, COL_CU:COL_CU + GROUP_WIDTH]
    d_ref[0] = z[:, COL_DQ:COL_DQ + 4 * GROUP_WIDTH]

    lane_head = lax.broadcasted_iota(jnp.int32, (GM_BLOCK, GROUP_WIDTH), 1) // (GROUP_WIDTH // GM_HEADS)
    for r in range(tm // GM_BLOCK):
        rows = slice(r * GM_BLOCK, (r + 1) * GM_BLOCK)
        u = _gelu(z[rows, COL_AU:COL_AU + GROUP_WIDTH])
        v = _gelu(z[rows, COL_AV:COL_AV + GROUP_WIDTH])
        vn = _layer_norm(v, gmg_ref[l:l + 1, :], gmb_ref[l:l + 1, :])
        vn_heads = jnp.concatenate([jnp.where(lane_head == hh, vn, 0.0) for hh in range(GM_HEADS)],
                                   axis=0).astype(BF16)
        ya_ref[0, rows, :] = (u * (_dot(gmw_ref[...], vn_heads) + gmbias_ref[...])).astype(BF16)

    cq = z[:, COL_CQ:COL_CQ + 256]
    cqn = cq * lax.rsqrt(jnp.sum(cq * cq, axis=-1, keepdims=True) * (1.0 / MLA_Q_RANK) + RMS_EPS)
    cqn = (cqn * qn_ref[...]).astype(BF16)
    qq = _dot(cqn, wq_ref[...])
    tq_c = jnp.concatenate([tq_c_ref[...]] * MLA_HEADS, axis=1)
    tq_s = jnp.concatenate([tq_s_ref[...]] * MLA_HEADS, axis=1)
    q_ref[0] = (qq[:, :QK_PAD] * tq_c + qq[:, QK_PAD:] * tq_s).astype(BF16)

    ckv = z[:, COL_CKV:COL_CKV + MLA_KV_RANK]
    ckvn = ckv * lax.rsqrt(jnp.mean(ckv * ckv, axis=-1, keepdims=True) + RMS_EPS)
    ckvn = (ckvn * kvn_ref[...]).astype(BF16)
    krt = (z[:, COL_KR:COL_KR + 128] * tk_ref[...]).astype(BF16)
    kv = _dot(jnp.concatenate([ckvn, krt], axis=1), wkv_ref[...])
    k_ref[0] = kv[:, :QK_PAD].astype(BF16)
    one_col = (lax.broadcasted_iota(jnp.int32, (tm, HEAD_PAD - MLA_V_DIM), 1) == 0).astype(F32)
    for hh in range(MLA_HEADS):
        vh = kv[:, QK_PAD + hh * MLA_V_DIM:QK_PAD + (hh + 1) * MLA_V_DIM]
        v_ref[0, hh] = jnp.concatenate([vh, one_col], axis=1).astype(BF16)


def _inproj(x, l, pre_ln, ln_g, ln_b, mod4, w_ext, gm_g, gm_b, gm_w, gm_bias, qn, kvn, wq_ext, wkv_ext,
            tq_c, tq_s, tk):
    bsz, L, d = x.shape
    tm = INPROJ_ROWS
    per_layer = lambda a: _layer_spec(a, l)
    const = lambda shape: pl.BlockSpec(shape, lambda b, t: (0,) * len(shape))
    out_shape = (
        jax.ShapeDtypeStruct((bsz, L, GROUP_WIDTH), BF16),
        jax.ShapeDtypeStruct((L, bsz * GROUP_WIDTH), F32),
        jax.ShapeDtypeStruct((bsz, L, 4 * GROUP_WIDTH), F32),
        jax.ShapeDtypeStruct((bsz, L, QK_PAD), BF16),
        jax.ShapeDtypeStruct((bsz, L, QK_PAD), BF16),
        jax.ShapeDtypeStruct((bsz, MLA_HEADS, L, HEAD_PAD), BF16),
    )
    row = lambda w: pl.BlockSpec((1, tm, w), lambda b, t: (b, t, 0))
    return pl.pallas_call(
        functools.partial(_inproj_kernel, tm=tm, pre_ln=pre_ln, l=l),
        out_shape=out_shape,
        grid=(bsz, L // tm),
        in_specs=[row(d), const((1, d)), const((1, d)), _mod_spec(mod4, l),
                  per_layer(w_ext), _whole_spec(gm_g), _whole_spec(gm_b), per_layer(gm_w), per_layer(gm_bias),
                  per_layer(qn), per_layer(kvn), per_layer(wq_ext), per_layer(wkv_ext),
                  pl.BlockSpec((tm, HEAD_PAD), lambda b, t: (t, 0)),
                  pl.BlockSpec((tm, HEAD_PAD), lambda b, t: (t, 0)),
                  pl.BlockSpec((tm, 128), lambda b, t: (t, 0))],
        out_specs=(row(GROUP_WIDTH), pl.BlockSpec((tm, GROUP_WIDTH), lambda b, t: (t, b)),
                   row(4 * GROUP_WIDTH), row(QK_PAD), row(QK_PAD),
                   pl.BlockSpec((1, MLA_HEADS, tm, HEAD_PAD), lambda b, t: (b, 0, t, 0))),
        compiler_params=_cparams(("arbitrary", "arbitrary")),
        name="inproj",
    )(x, ln_g, ln_b, mod4, w_ext, gm_g, gm_b, gm_w, gm_bias, qn, kvn, wq_ext, wkv_ext, tq_c, tq_s, tk)


def _attn_kernel(q_ref, k_ref, v_ref, o_ref):
    L = q_ref.shape[1]
    tq = ATT_TQ
    q_chunk = lax.broadcasted_iota(jnp.int32, (tq, tq), 0) // CHUNK
    k_chunk = lax.broadcasted_iota(jnp.int32, (tq, tq), 1) // CHUNK
    allowed = k_chunk <= q_chunk
    nq = L // tq
    order = [t for pair in zip(range(nq // 2), range(nq - 1, nq // 2 - 1, -1)) for t in pair] + (
        [nq // 2] if nq % 2 else [])
    for qi in order:
        lk = (qi + 1) * tq
        outs = []
        for hh in range(ATT_HEADS_PER_STEP):
            lanes = slice(hh * HEAD_PAD, (hh + 1) * HEAD_PAD)
            qh = q_ref[0, qi * tq:(qi + 1) * tq, lanes]
            s = _dot_nt(qh, k_ref[0, 0:lk, lanes])
            s_diag = jnp.where(allowed, s[:, lk - tq:], -1e30)
            s = s_diag if qi == 0 else jnp.concatenate([s[:, :lk - tq], s_diag], axis=1)
            m = jnp.max(s, axis=-1, keepdims=True)
            p = jnp.exp2(s - m)
            pv = _dot(p.astype(BF16), v_ref[0, hh, 0:lk, :])
            outs.append(pv[:, 0:MLA_V_DIM] / pv[:, MLA_V_DIM:MLA_V_DIM + 1])
        o_ref[0, qi * tq:(qi + 1) * tq, :] = jnp.concatenate(outs, axis=1).astype(BF16)


def _attention(q, k, v):
    bsz, L, _ = q.shape
    hp = ATT_HEADS_PER_STEP
    return pl.pallas_call(
        _attn_kernel,
        out_shape=jax.ShapeDtypeStruct((bsz, L, GROUP_WIDTH), BF16),
        grid=(bsz, MLA_HEADS // hp),
        in_specs=[pl.BlockSpec((1, L, hp * HEAD_PAD), lambda b, g: (b, 0, g)),
                  pl.BlockSpec((1, L, hp * HEAD_PAD), lambda b, g: (b, 0, g)),
                  pl.BlockSpec((1, hp, L, HEAD_PAD), lambda b, g: (b, g, 0, 0))],
        out_specs=pl.BlockSpec((1, L, hp * MLA_V_DIM), lambda b, g: (b, 0, g)),
        compiler_params=_cparams(("arbitrary", "arbitrary")),
        name="mla_attention",
    )(q, k, v)


def _s5_kernel(u_ref, bmat_ref, are_ref, aim_ref, cmat_ref, dvec_ref, wglu_ref, bglu_ref,
               o_ref, h_ref, us_ref, bu_ref, hs_ref, os_ref, *, tc, nb, l):
    @pl.when(pl.program_id(0) == 0)
    def _():
        h_ref[...] = jnp.zeros(h_ref.shape, F32)

    nslab = GROUP_WIDTH // LANES
    for bb in range(nb):
        for s in range(nslab):
            lanes = slice(bb * GROUP_WIDTH + s * LANES, bb * GROUP_WIDTH + (s + 1) * LANES)
            us_ref[s, pl.ds(bb, tc, stride=nb), :] = u_ref[:, lanes]
    u = jnp.concatenate([us_ref[s] for s in range(nslab)], axis=1)
    bu_ref[...] = _dot(u.astype(BF16), bmat_ref[...])
    a_re = jnp.broadcast_to(are_ref[...], (nb, S5_NSTATE))
    a_im = jnp.broadcast_to(aim_ref[...], (nb, S5_NSTATE))
    re = slice(0, S5_NSTATE)
    im = slice(S5_NSTATE, 2 * S5_NSTATE)

    def step(i, carry):
        h_re, h_im = carry
        r0 = pl.ds(pl.multiple_of(i * 2 * nb, 2 * nb), nb)
        r1 = pl.ds(pl.multiple_of(i * 2 * nb, 2 * nb) + nb, nb)
        m_re = a_re * h_re - a_im * h_im + bu_ref[r0, re]
        m_im = a_re * h_im + a_im * h_re + bu_ref[r0, im]
        n_re = a_re * m_re - a_im * m_im + bu_ref[r1, re]
        n_im = a_re * m_im + a_im * m_re + bu_ref[r1, im]
        rr = pl.ds(pl.multiple_of(i * 2 * nb, 2 * nb), 2 * nb)
        hs_ref[rr, re] = jnp.concatenate([m_re, n_re], axis=0).astype(BF16)
        hs_ref[rr, im] = jnp.concatenate([m_im, n_im], axis=0).astype(BF16)
        return n_re, n_im

    h_re, h_im = lax.fori_loop(0, tc // 2, step, (h_ref[:, re], h_ref[:, im]))
    h_ref[:, re] = h_re
    h_ref[:, im] = h_im

    half = (tc * nb) // 2
    y = jnp.concatenate([_dot(hs_ref[0:half, :], cmat_ref[...]), _dot(hs_ref[half:2 * half, :], cmat_ref[...])],
                        axis=0) + dvec_ref[l:l + 1, :] * u
    z = _gelu(y)
    z_bf = z.astype(BF16)
    gate = _sigmoid(jnp.concatenate([_dot(z_bf[0:half], wglu_ref[...]), _dot(z_bf[half:2 * half], wglu_ref[...])],
                                    axis=0) + bglu_ref[l:l + 1, :])
    res = z * gate
    for s in range(nslab):
        os_ref[s] = res[:, s * LANES:(s + 1) * LANES]
    for bb in range(nb):
        for s in range(nslab):
            lanes = slice(bb * GROUP_WIDTH + s * LANES, bb * GROUP_WIDTH + (s + 1) * LANES)
            o_ref[:, lanes] = os_ref[s, pl.ds(bb, tc, stride=nb), :].astype(BF16)


def _s5(u_t, nb, l, bmat, a_re, a_im, cmat, dvec, wglu, bglu):
    L = u_t.shape[0]
    gw = GROUP_WIDTH
    tc = S5_CHUNK
    per_layer = lambda a: _layer_spec(a, l)
    return pl.pallas_call(
        functools.partial(_s5_kernel, tc=tc, nb=nb, l=l),
        out_shape=jax.ShapeDtypeStruct((L, nb * gw), BF16),
        grid=(L // tc,),
        in_specs=[pl.BlockSpec((tc, nb * gw), lambda t: (t, 0)),
                  per_layer(bmat), per_layer(a_re), per_layer(a_im), per_layer(cmat), _whole_spec(dvec),
                  per_layer(wglu), _whole_spec(bglu)],
        out_specs=pl.BlockSpec((tc, nb * gw), lambda t: (t, 0)),
        scratch_shapes=[pltpu.VMEM((nb, 2 * S5_NSTATE), F32),
                        pltpu.VMEM((gw // LANES, tc * nb, LANES), F32),
                        pltpu.VMEM((tc * nb, 2 * S5_NSTATE), F32),
                        pltpu.VMEM((tc * nb, 2 * S5_NSTATE), BF16),
                        pltpu.VMEM((gw // LANES, tc * nb, LANES), F32)],
        compiler_params=_cparams(("arbitrary",)),
        name="s5_scan",
    )(u_t, bmat, a_re, a_im, cmat, dvec, wglu, bglu)


def _hgrn_chunk(q, fl, iv, lb, tri, ones_bd, st):
    nsub = CHUNK // HG_SUB
    f = lb + (1.0 - lb) * _sigmoid(fl)
    logf = jnp.log(f)
    kk = 1.0 - f
    hi, lo = _split2(logf)
    b2 = (_dot(tri, hi) + _dot(tri, lo)) * LOG2E
    lane_head = lax.broadcasted_iota(jnp.int32, (HG_SUB, GROUP_WIDTH), 1) // HG_DK
    row_id = lax.broadcasted_iota(jnp.int32, (HG_SUB, GROUP_WIDTH), 0)
    v_bf = iv.astype(BF16)

    terms = []
    for si in range(nsub):
        r0 = si * HG_SUB
        q_i, k_i, b_i = (a[r0:r0 + HG_SUB] for a in (q, kk, b2))
        for j in range(HG_SUB):
            e = jnp.exp2(jnp.where(row_id >= j, b_i - b_i[j:j + 1], -1e30))
            terms.append(q_i * k_i[j:j + 1] * e)
    p = _dot(jnp.concatenate(terms, axis=0).astype(BF16), ones_bd)
    o_parts = []
    for si in range(nsub):
        r0 = si * HG_SUB
        base = si * HG_SUB * HG_SUB
        o_i = p[base:base + HG_SUB] * iv[r0:r0 + 1]
        for j in range(1, HG_SUB):
            o_i = o_i + p[base + j * HG_SUB:base + (j + 1) * HG_SUB] * iv[r0 + j:r0 + j + 1]
        o_parts.append(o_i)

    atts = []
    for si in range(1, nsub):
        r0 = si * HG_SUB
        b_edge = b2[r0 - 1:r0]
        q_t = q[r0:r0 + HG_SUB] * jnp.exp2(b2[r0:r0 + HG_SUB] - b_edge)
        k_t = jnp.concatenate([kk[0:r0] * jnp.exp2(b_edge - b2[0:r0]),
                               jnp.zeros((CHUNK - r0, GROUP_WIDTH), F32)], axis=0).astype(BF16)
        q_exp = jnp.concatenate([jnp.where(lane_head == hh, q_t, 0.0) for hh in range(HG_HEADS)],
                                axis=0).astype(BF16)
        atts.append(_dot_nt(q_exp, k_t))
    o2 = _dot(jnp.concatenate(atts, axis=0).astype(BF16), v_bf)
    for si in range(1, nsub):
        base = (si - 1) * HG_HEADS * HG_SUB
        for hh in range(HG_HEADS):
            o_parts[si] = o_parts[si] + jnp.where(lane_head == hh,
                                                  o2[base + hh * HG_SUB:base + (hh + 1) * HG_SUB], 0.0)
    o_intra = jnp.concatenate(o_parts, axis=0)

    o_inter = _dot_nt((q * jnp.exp2(b2)).astype(BF16), st.astype(BF16))
    b_last = b2[CHUNK - 1:CHUNK]
    k_hat = (kk * jnp.exp2(b_last - b2)).astype(BF16)
    upd = _dot_tn(v_bf, k_hat)
    bd = (lax.broadcasted_iota(jnp.int32, (GROUP_WIDTH, GROUP_WIDTH), 0) // HG_DK
          == lax.broadcasted_iota(jnp.int32, (GROUP_WIDTH, GROUP_WIDTH), 1) // HG_DK)
    st_new = st * jnp.exp2(b_last) + jnp.where(bd, upd, 0.0)
    return o_intra + o_inter, st_new


def _hgrn_kernel(d_ref, lb_ref, gn_ref, tri_ref, ones_ref, o_ref, st_ref, *, th, l):
    @pl.when(pl.program_id(1) == 0)
    def _():
        st_ref[...] = jnp.zeros(st_ref.shape, F32)

    lb = lb_ref[l:l + 1, :]
    tri = tri_ref[...]
    ones_bd = ones_ref[...]
    for c in range(th // CHUNK):
        rows = slice(c * CHUNK, (c + 1) * CHUNK)
        for bb in range(d_ref.shape[0]):
            q = d_ref[bb, rows, 0:GROUP_WIDTH]
            fl = d_ref[bb, rows, GROUP_WIDTH:2 * GROUP_WIDTH]
            iv = d_ref[bb, rows, 2 * GROUP_WIDTH:3 * GROUP_WIDTH]
            g = d_ref[bb, rows, 3 * GROUP_WIDTH:4 * GROUP_WIDTH]
            o, st_new = _hgrn_chunk(q, fl, iv, lb, tri, ones_bd, st_ref[bb])
            st_ref[bb] = st_new
            ms = _dot((o * o).astype(BF16), ones_bd) * (1.0 / HG_DK)
            o = o * lax.rsqrt(ms + RMS_EPS) * gn_ref[l:l + 1, :]
            o_ref[bb, rows, :] = (o * _sigmoid(g)).astype(BF16)


def _hgrn(d, l, lb, gn, tri, ones_bd):
    bsz, L, _ = d.shape
    th = HG_ROWS
    nbb = HG_BATCH if bsz % HG_BATCH == 0 else 1
    const = lambda shape: pl.BlockSpec(shape, lambda b, t: (0,) * len(shape))
    return pl.pallas_call(
        functools.partial(_hgrn_kernel, th=th, l=l),
        out_shape=jax.ShapeDtypeStruct((bsz, L, GROUP_WIDTH), BF16),
        grid=(bsz // nbb, L // th),
        in_specs=[pl.BlockSpec((nbb, th, 4 * GROUP_WIDTH), lambda b, t: (b, t, 0)),
                  _whole_spec(lb), _whole_spec(gn),
                  const((CHUNK, CHUNK)), const((GROUP_WIDTH, GROUP_WIDTH))],
        out_specs=pl.BlockSpec((nbb, th, GROUP_WIDTH), lambda b, t: (b, t, 0)),
        scratch_shapes=[pltpu.VMEM((nbb, GROUP_WIDTH, GROUP_WIDTH), F32)],
        compiler_params=_cparams(("arbitrary", "arbitrary")),
        name="hgrn2",
    )(d, lb, gn, tri, ones_bd)


def _merge_kernel(ya_ref, yb_ref, yc_ref, yd_ref, x_ref, lng_ref, lnb_ref, mod_ref, mg_ref, wo_ref, g_ref, b_ref,
                  o_ref, *, pre_ln, l):
    acc = None
    for gi, y_ref in enumerate((ya_ref, yb_ref, yc_ref, yd_ref)):
        y = (y_ref[...] if gi == 2 else y_ref[0]).astype(F32)
        yn = y * lax.rsqrt(jnp.mean(y * y, axis=-1, keepdims=True) + RMS_EPS)
        yn = (yn * mg_ref[l:l + 1, gi * GROUP_WIDTH:(gi + 1) * GROUP_WIDTH]).astype(BF16)
        part = _dot(yn, wo_ref[gi * GROUP_WIDTH:(gi + 1) * GROUP_WIDTH, :])
        acc = part if acc is None else acc + part
    g1 = mod_ref[0, :, 2 * D_MODEL:3 * D_MODEL]
    x = x_ref[0]
    if pre_ln:
        x = _layer_norm(x, lng_ref[...], lnb_ref[...])
    o_ref[0] = _layer_norm(DN_ALPHA * x + (1.0 + g1) * acc, g_ref[l:l + 1, :], b_ref[l:l + 1, :])


def _merge(ya, yb, yc, yd, x, l, pre_ln, ln_g, ln_b, mod4, mg, wo, g, b):
    bsz, L, d = x.shape
    tm = MERGE_ROWS
    per_layer = lambda a: _layer_spec(a, l)
    const = lambda shape: pl.BlockSpec(shape, lambda bb, t: (0,) * len(shape))
    row = lambda w: pl.BlockSpec((1, tm, w), lambda bb, t: (bb, t, 0))
    return pl.pallas_call(
        functools.partial(_merge_kernel, pre_ln=pre_ln, l=l),
        out_shape=jax.ShapeDtypeStruct((bsz, L, d), F32),
        grid=(bsz, L // tm),
        in_specs=[row(GROUP_WIDTH), row(GROUP_WIDTH), pl.BlockSpec((tm, GROUP_WIDTH), lambda bb, t: (t, bb)),
                  row(GROUP_WIDTH), row(d), const((1, d)), const((1, d)), _mod_spec(mod4, l),
                  _whole_spec(mg), per_layer(wo), _whole_spec(g), _whole_spec(b)],
        out_specs=row(d),
        compiler_params=_cparams(("arbitrary", "arbitrary")),
        name="merge_out",
    )(ya, yb, yc, yd, x, ln_g, ln_b, mod4, mg, wo, g, b)


def _ffn_kernel(x_ref, xh_ref, mod_ref, wup_ref, cw_ref, cb_ref, wdn_ref, g_ref, b_ref, o_ref,
                hp_ref, hx_ref, a_ref, act_ref, ap_ref, *, tm, l):
    ti = pl.program_id(1)
    seg = tm // FF_SEGS
    nslab = D_MODEL // LANES
    sh2 = mod_ref[0, :, 3 * D_MODEL:4 * D_MODEL]
    sc2 = mod_ref[0, :, 4 * D_MODEL:5 * D_MODEL]
    g2 = mod_ref[0, :, 5 * D_MODEL:6 * D_MODEL]
    x = x_ref[0]
    h2 = x * (1.0 + sc2) + sh2
    halo = jnp.where(ti > 0, xh_ref[0] * (1.0 + sc2) + sh2, 0.0)
    for s in range(nslab):
        lanes = slice(s * LANES, (s + 1) * LANES)
        hp_ref[s, 0:FF_HALO, :] = halo[:, lanes]
        for p in range(FF_SEGS):
            hp_ref[s, pl.ds(FF_HALO + p, seg, stride=FF_SEGS), :] = h2[p * seg:(p + 1) * seg, lanes]
    hx_ref[...] = jnp.concatenate([hp_ref[s] for s in range(nslab)], axis=1).astype(BF16)
    nj = D_FF // FF_TILE
    sub0 = lax.broadcasted_iota(jnp.int32, (FF_SEGS, FF_TILE), 0) == 0

    def up(j):
        hx = hx_ref[...]
        for half in range(2):
            a = a_ref.at[j % 2, half]
            cols = slice(half * D_FF + j * FF_TILE, half * D_FF + (j + 1) * FF_TILE)
            a[...] = _dot(hx, wup_ref[:, cols])
            prev1 = jnp.where(sub0, a[FF_HALO - 1:FF_HALO, :], pltpu.roll(a[FF_HALO + tm - 8:FF_HALO + tm, :], 1, 0))
            prev2 = jnp.where(sub0, a[FF_HALO - 2:FF_HALO - 1, :],
                              pltpu.roll(a[FF_HALO + tm - 16:FF_HALO + tm - 8, :], 1, 0))
            a[8:16, :] = prev1
            a[0:8, :] = prev2

    def conv(j, half):
        cols = slice(half * D_FF + j * FF_TILE, half * D_FF + (j + 1) * FF_TILE)
        a = a_ref.at[j % 2, half]
        return (a[0:tm, :] * cw_ref[0:1, cols] + a[8:tm + 8, :] * cw_ref[1:2, cols]
                + a[16:tm + 16, :] * cw_ref[2:3, cols] + cb_ref[l:l + 1, cols])

    up(0)
    acc = None
    for j in range(nj):
        if j + 1 < nj:
            up(j + 1)
        gate = conv(j, 1)
        act_ref[:, j * FF_TILE:(j + 1) * FF_TILE] = (conv(j, 0) * (gate * _sigmoid(gate))).astype(BF16)
        if j + 1 in FF_DOWN_SPLITS:
            k0 = ([0] + [e for e in FF_DOWN_SPLITS if e <= j])[-1] * FF_TILE
            k1 = (j + 1) * FF_TILE
            part = _dot(act_ref[:, k0:k1], wdn_ref[k0:k1, :])
            acc = part if acc is None else acc + part
    for s in range(nslab):
        ap_ref[s] = acc[:, s * LANES:(s + 1) * LANES]
    acc_nat = jnp.concatenate(
        [jnp.concatenate([ap_ref[s, pl.ds(p, seg, stride=FF_SEGS), :] for p in range(FF_SEGS)], axis=0)
         for s in range(nslab)], axis=1)
    o_ref[0] = _layer_norm(DN_ALPHA * x + (1.0 + g2) * acc_nat, g_ref[l:l + 1, :], b_ref[l:l + 1, :])


def _ffn(x, l, mod4, wup, cw, cb, wdn, g, b):
    bsz, L, d = x.shape
    tm = FF_ROWS
    per_layer = lambda a: _layer_spec(a, l, pipeline_mode=pl.Buffered(1))
    hb = tm // FF_HALO
    return pl.pallas_call(
        functools.partial(_ffn_kernel, tm=tm, l=l),
        out_shape=jax.ShapeDtypeStruct((bsz, L, d), F32),
        grid=(bsz, L // tm),
        in_specs=[pl.BlockSpec((1, tm, d), lambda bb, t: (bb, t, 0)),
                  pl.BlockSpec((1, FF_HALO, d), lambda bb, t: (bb, jnp.maximum(t * hb - 1, 0), 0)),
                  _mod_spec(mod4, l),
                  per_layer(wup), per_layer(cw), _whole_spec(cb), per_layer(wdn), _whole_spec(g), _whole_spec(b)],
        out_specs=pl.BlockSpec((1, tm, d), lambda bb, t: (bb, t, 0)),
        scratch_shapes=[pltpu.VMEM((d // LANES, tm + FF_HALO, LANES), F32),
                        pltpu.VMEM((tm + FF_HALO, d), BF16),
                        pltpu.VMEM((2, 2, tm + FF_HALO, FF_TILE), F32),
                        pltpu.VMEM((tm, D_FF), BF16),
                        pltpu.VMEM((d // LANES, tm, LANES), F32)],
        compiler_params=_cparams(("arbitrary", "arbitrary")),
        name="conv_ffn",
    )(x, x, mod4, wup, cw, cb, wdn, g, b)


def _rope_tables(L):
    inv = 1.0 / (ROPE_THETA ** (jnp.arange(0, MLA_ROPE_DIM, 2, dtype=F32) / MLA_ROPE_DIM))
    ang = jnp.arange(L, dtype=F32)[:, None] * inv[None, :]
    cos, sin = jnp.cos(ang), jnp.sin(ang)
    cc = jnp.concatenate([cos, cos], axis=1)
    ss = jnp.concatenate([-sin, sin], axis=1)
    scale = MLA_QK_DIM ** -0.5 * LOG2E
    zeros32 = jnp.zeros((L, HEAD_PAD - MLA_QK_DIM), F32)
    tq_c = jnp.concatenate([jnp.ones((L, MLA_NOPE_DIM), F32), cc, zeros32], axis=1) * scale
    tq_s = jnp.concatenate([jnp.zeros((L, MLA_NOPE_DIM), F32), ss, zeros32], axis=1) * scale
    tk = jnp.concatenate([cc, ss, jnp.zeros((L, 64), F32)], axis=1)
    return tq_c, tq_s, tk


def _swap_halves(w):
    half = MLA_ROPE_DIM // 2
    return jnp.concatenate([w[..., half:], w[..., :half]], axis=-1)


def _prep_inproj_weights(w_in_l, w_uq_l, w_ukv_l, q_norm_l, kv_norm_l):
    d = w_in_l.shape[0]
    a_u, a_v, cq, ckv, kr, c_u, d_q, d_f, d_i, d_g = jnp.split(
        w_in_l, (256, 512, 704, 832, 864, 1120, 1376, 1632, 1888), axis=1)
    z64 = jnp.zeros((d, 64), F32)
    w_ext = jnp.concatenate([a_u, a_v, c_u, d_q, d_f, d_i, d_g, cq, z64, ckv, kr, _swap_halves(kr), z64],
                            axis=1).astype(BF16)
    wq = w_uq_l.reshape(MLA_Q_RANK, MLA_HEADS, MLA_QK_DIM)
    zq32 = jnp.zeros((MLA_Q_RANK, MLA_HEADS, HEAD_PAD - MLA_QK_DIM), F32)
    zq64 = jnp.zeros((MLA_Q_RANK, MLA_HEADS, MLA_NOPE_DIM), F32)
    wq_a = jnp.concatenate([wq, zq32], axis=2).reshape(MLA_Q_RANK, QK_PAD)
    wq_b = jnp.concatenate([zq64, _swap_halves(wq[:, :, MLA_NOPE_DIM:]), zq32], axis=2).reshape(MLA_Q_RANK, QK_PAD)
    wq_ext = jnp.concatenate([wq_a, wq_b], axis=1)
    wq_ext = jnp.concatenate([wq_ext, jnp.zeros((256 - MLA_Q_RANK, 2 * QK_PAD), F32)], axis=0).astype(BF16)
    qn = jnp.concatenate([q_norm_l, jnp.zeros((256 - MLA_Q_RANK,), F32)]).reshape(1, 256)
    wkv = w_ukv_l.reshape(MLA_KV_RANK, MLA_HEADS, MLA_NOPE_DIM + MLA_V_DIM)
    zk64 = jnp.zeros((MLA_KV_RANK, MLA_HEADS, HEAD_PAD - MLA_NOPE_DIM), F32)
    wk = jnp.concatenate([wkv[:, :, :MLA_NOPE_DIM], zk64], axis=2).reshape(MLA_KV_RANK, QK_PAD)
    wv = wkv[:, :, MLA_NOPE_DIM:].reshape(MLA_KV_RANK, MLA_HEADS * MLA_V_DIM)
    eye = jnp.eye(MLA_ROPE_DIM, dtype=F32)
    place_h = jnp.concatenate([jnp.zeros((MLA_ROPE_DIM, MLA_NOPE_DIM), F32), eye,
                               jnp.zeros((MLA_ROPE_DIM, HEAD_PAD - MLA_QK_DIM), F32)], axis=1)
    place = jnp.concatenate([place_h] * MLA_HEADS, axis=1)
    place = jnp.concatenate([place, jnp.zeros((MLA_ROPE_DIM, MLA_HEADS * MLA_V_DIM), F32)], axis=1)
    top = jnp.concatenate([wk, wv], axis=1)
    wkv_ext = jnp.concatenate([top, place, place, jnp.zeros((64, QK_PAD + 256), F32)], axis=0).astype(BF16)
    kvn = kv_norm_l.reshape(1, MLA_KV_RANK)
    return w_ext, wq_ext, wkv_ext, qn, kvn


def _prep_gm(gm_ws_l, gm_bs_l):
    ck = jnp.arange(GM_BLOCK) // CHUNK
    mask = ck[None, :] <= ck[:, None]
    w = jnp.where(mask[None], gm_ws_l, 0)
    w = jnp.transpose(w, (1, 0, 2)).reshape(GM_BLOCK, GM_HEADS * GM_BLOCK).astype(BF16)
    bias = jnp.repeat(gm_bs_l.T, GROUP_WIDTH // GM_HEADS, axis=1)
    return w, bias


def _prep_s5(a_re, a_im, b_re, b_im, c_re, c_im, log_step):
    dt = jnp.exp(log_step)[:, None]
    mag = jnp.exp(dt * a_re)
    abar_re, abar_im = mag * jnp.cos(dt * a_im), mag * jnp.sin(dt * a_im)
    den = a_re * a_re + a_im * a_im
    nr, ni = abar_re - 1.0, abar_im
    zoh_re = (nr * a_re + ni * a_im) / den
    zoh_im = (ni * a_re - nr * a_im) / den
    bbar_re = zoh_re[..., None] * b_re - zoh_im[..., None] * b_im
    bbar_im = zoh_re[..., None] * b_im + zoh_im[..., None] * b_re
    eye_g = jnp.eye(S5_GROUPS, dtype=F32)
    bm_re = jnp.einsum('gpn,gh->gnhp', bbar_re, eye_g).reshape(GROUP_WIDTH, S5_NSTATE)
    bm_im = jnp.einsum('gpn,gh->gnhp', bbar_im, eye_g).reshape(GROUP_WIDTH, S5_NSTATE)
    bmat = jnp.concatenate([bm_re, bm_im], axis=1).astype(BF16)
    cm_re = jnp.einsum('gnp,gh->gphn', c_re, eye_g).reshape(S5_NSTATE, GROUP_WIDTH)
    cm_im = jnp.einsum('gnp,gh->gphn', c_im, eye_g).reshape(S5_NSTATE, GROUP_WIDTH)
    cmat = jnp.concatenate([cm_re, -cm_im], axis=0).astype(BF16)
    return bmat, abar_re.reshape(1, S5_NSTATE), abar_im.reshape(1, S5_NSTATE), cmat


def kernel(x, c, ln_in_g, ln_in_b, hg_lb_logits, w_ada, b_ada, w_in, gm_ln_g, gm_ln_b, gm_ws, gm_bs, mla_q_norm, mla_kv_norm, mla_w_uq, mla_w_ukv, s5_a_re, s5_a_im, s5_b_re, s5_b_im, s5_c_re, s5_c_im, s5_d, s5_log_step, s5_w_glu, s5_b_glu, hg_norm_g, merge_g, w_out, ln1_g, ln1_b, w_up, conv_w, conv_b, w_down, ln2_g, ln2_b):
    bsz, L, d = x.shape
    nl = w_in.shape[0]
    mod = _ada(c, w_ada, b_ada)
    lb_cum = jnp.cumsum(jax.nn.softmax(hg_lb_logits.astype(F32), axis=0), axis=0)
    tq_c, tq_s, tk = _rope_tables(L)
    tri = jnp.tril(jnp.ones((CHUNK, CHUNK), F32)).astype(BF16)
    hid = jnp.arange(GROUP_WIDTH) // HG_DK
    ones_bd = (hid[:, None] == hid[None, :]).astype(BF16)
    w_ext, wq_ext, wkv_ext, qn, kvn = jax.vmap(_prep_inproj_weights)(w_in, mla_w_uq, mla_w_ukv, mla_q_norm,
                                                                     mla_kv_norm)
    gm_w, gm_bias = jax.vmap(_prep_gm)(gm_ws, gm_bs)
    bmat, a_re, a_im, cmat = jax.vmap(_prep_s5)(s5_a_re, s5_a_im, s5_b_re, s5_b_im, s5_c_re, s5_c_im, s5_log_step)
    wglu_bf, wout_bf, wup_bf, wdn_bf = (w.astype(BF16) for w in (s5_w_glu, w_out, w_up, w_down))

    mod4 = mod.reshape(nl, bsz, 1, 6 * d)
    lbs = lb_cum - lb_cum[0:1]

    xs = x
    ln_g, ln_b = ln_in_g.reshape(1, d), ln_in_b.reshape(1, d)
    for l in range(nl):
        pre_ln = l == 0
        y_a, c_u, dd, q, k, v = _inproj(xs, l, pre_ln, ln_g, ln_b, mod4, w_ext, gm_ln_g, gm_ln_b, gm_w, gm_bias,
                                        qn, kvn, wq_ext, wkv_ext, tq_c, tq_s, tk)
        y_b = _attention(q, k, v)
        y_c = _s5(c_u, bsz, l, bmat, a_re, a_im, cmat, s5_d, wglu_bf, s5_b_glu)
        y_d = _hgrn(dd, l, lbs, hg_norm_g, tri, ones_bd)
        xs = _merge(y_a, y_b, y_c, y_d, xs, l, pre_ln, ln_g, ln_b, mod4, merge_g, wout_bf, ln1_g, ln1_b)
        xs = _ffn(xs, l, mod4, wup_bf, conv_w, conv_b, wdn_bf, ln2_g, ln2_b)
    return xs
```

```python
import functools
import math

import jax
import jax.numpy as jnp
from jax import lax
from jax.experimental import pallas as pl
from jax.experimental.pallas import tpu as pltpu

F32 = jnp.float32
BF16 = jnp.bfloat16

D_MODEL = 1024
DEPTH = 2
CHUNK = 64
GROUP_WIDTH = 256
N_MIXERS = 4
GM_HEADS = 4
GM_BLOCK = 128
MLA_HEADS = 4
MLA_NOPE_DIM = 64
MLA_ROPE_DIM = 32
MLA_QK_DIM = MLA_NOPE_DIM + MLA_ROPE_DIM
MLA_V_DIM = 64
MLA_Q_RANK = 192
MLA_KV_RANK = 128
ROPE_THETA = 10000.0
S5_GROUP = 16
S5_GROUPS = 16
S5_STATE = 64
HG_HEADS = 4
HG_DK = 64
D_FF = 2816
CONV_W = 3
DN_ALPHA = (2 * DEPTH) ** 0.25
LN_EPS = 1e-5
RMS_EPS = 1e-6
LOG2E = 1.4426950408889634

LANES = 128
HEAD_PAD = 128
QK_PAD = MLA_HEADS * HEAD_PAD
S5_NSTATE = S5_GROUPS * S5_STATE
HG_SUB = 8

COL_AU, COL_AV, COL_CU, COL_DQ = 0, 256, 512, 768
COL_CQ, COL_CKV, COL_KR = 1792, 2048, 2176
IN_EXT = 2304

ADA_COLS = 1536
INPROJ_ROWS = 1024
ATT_TQ = 512
ATT_HEADS_PER_STEP = 4
S5_CHUNK = 256
HG_ROWS = 512
HG_BATCH = 2
MERGE_ROWS = 1024
FF_ROWS = 512
FF_TILE = 256
FF_HALO = 16
FF_SEGS = 8
FF_DOWN_SPLITS = (5, 10, 11)

VMEM_LIMIT = 56 * 1024 * 1024


def _sigmoid(x):
    return 1.0 / (1.0 + jnp.exp(-x))


def _gelu(x):
    return 0.5 * x * (1.0 + jnp.tanh(math.sqrt(2.0 / math.pi) * (x + 0.044715 * (x * x * x))))


def _split2(x):
    hi = x.astype(BF16)
    lo = (x - hi.astype(F32)).astype(BF16)
    return hi, lo


def _dot(a, b):
    return jnp.dot(a, b, preferred_element_type=F32)


def _dot_nt(a, b):
    return lax.dot_general(a, b, (((1,), (1,)), ((), ())), preferred_element_type=F32)


def _dot_tn(a, b):
    return lax.dot_general(a, b, (((0,), (0,)), ((), ())), preferred_element_type=F32)


def _layer_norm(x, g, b):
    mu = jnp.mean(x, axis=-1, keepdims=True)
    xc = x - mu
    var = jnp.mean(xc * xc, axis=-1, keepdims=True)
    return xc * lax.rsqrt(var + LN_EPS) * g + b


def _layer_spec(stacked, l, **kwargs):
    shape = stacked.shape[1:]
    return pl.BlockSpec((None,) + shape, lambda *_: (l,) + (0,) * len(shape), **kwargs)


def _whole_spec(arr):
    return pl.BlockSpec(arr.shape, lambda *_: (0,) * arr.ndim)


def _mod_spec(mod4, l):
    return pl.BlockSpec((None, 1, 1, mod4.shape[-1]), lambda b, *_: (l, b, 0, 0))


def _cparams(sem, vmem=VMEM_LIMIT):
    return pltpu.CompilerParams(dimension_semantics=sem, vmem_limit_bytes=vmem)


def _ada_kernel(c_ref, w_ref, b_ref, o_ref):
    c = c_ref[...]
    chi, clo = _split2(c * _sigmoid(c))
    whi, wlo = _split2(w_ref[0])
    acc = _dot(chi, whi) + _dot(clo, whi) + _dot(chi, wlo)
    o_ref[0] = acc + b_ref[0]


def _ada(c, w_ada, b_ada):
    nl, d, n = w_ada.shape
    tn = ADA_COLS
    bsz = c.shape[0]
    return pl.pallas_call(
        _ada_kernel,
        out_shape=jax.ShapeDtypeStruct((nl, bsz, n), F32),
        grid=(nl, n // tn),
        in_specs=[pl.BlockSpec((bsz, d), lambda l, j: (0, 0)),
                  pl.BlockSpec((1, d, tn), lambda l, j: (l, 0, j)),
                  pl.BlockSpec((1, 1, tn), lambda l, j: (l, 0, j))],
        out_specs=pl.BlockSpec((1, bsz, tn), lambda l, j: (l, 0, j)),
        compiler_params=_cparams(("arbitrary", "arbitrary")),
        name="ada_mod",
    )(c, w_ada, b_ada.reshape(nl, 1, n))


def _inproj_kernel(x_ref, lng_ref, lnb_ref, mod_ref, w_ref, gmg_ref, gmb_ref, gmw_ref, gmbias_ref,
                   qn_ref, kvn_ref, wq_ref, wkv_ref, tq_c_ref, tq_s_ref, tk_ref,
                   ya_ref, cu_ref, d_ref, q_ref, k_ref, v_ref, *, tm, pre_ln, l):
    x = x_ref[0]
    if pre_ln:
        x = _layer_norm(x, lng_ref[...], lnb_ref[...])
    sh1 = mod_ref[0, :, 0:D_MODEL]
    sc1 = mod_ref[0, :, D_MODEL:2 * D_MODEL]
    h = (x * (1.0 + sc1) + sh1).astype(BF16)
    z = _dot(h, w_ref[...])

    cu_ref[...] = z[:, COL_CU:COL_CU + GROUP_WIDTH]
    d_ref[0] = z[:, COL_DQ:COL_DQ + 4 * GROUP_WIDTH]

    lane_head = lax.broadcasted_iota(jnp.int32, (GM_BLOCK, GROUP_WIDTH), 1) // (GROUP_WIDTH // GM_HEADS)
    for r in range(tm // GM_BLOCK):
        rows = slice(r * GM_BLOCK, (r + 1) * GM_BLOCK)
        u = _gelu(z[rows, COL_AU:COL_AU + GROUP_WIDTH])
        v = _gelu(z[rows, COL_AV:COL_AV + GROUP_WIDTH])
        vn = _layer_norm(v, gmg_ref[l:l + 1, :], gmb_ref[l:l + 1, :])
        vn_heads = jnp.concatenate([jnp.where(lane_head == hh, vn, 0.0) for hh in range(GM_HEADS)],
                                   axis=0).astype(BF16)
        ya_ref[0, rows, :] = (u * (_dot(gmw_ref[...], vn_heads) + gmbias_ref[...])).astype(BF16)

    cq = z[:, COL_CQ:COL_CQ + 256]
    cqn = cq * lax.rsqrt(jnp.sum(cq * cq, axis=-1, keepdims=True) * (1.0 / MLA_Q_RANK) + RMS_EPS)
    cqn = (cqn * qn_ref[...]).astype(BF16)
    qq = _dot(cqn, wq_ref[...])
    tq_c = jnp.concatenate([tq_c_ref[...]] * MLA_HEADS, axis=1)
    tq_s = jnp.concatenate([tq_s_ref[...]] * MLA_HEADS, axis=1)
    q_ref[0] = (qq[:, :QK_PAD] * tq_c + qq[:, QK_PAD:] * tq_s).astype(BF16)

    ckv = z[:, COL_CKV:COL_CKV + MLA_KV_RANK]
    ckvn = ckv * lax.rsqrt(jnp.mean(ckv * ckv, axis=-1, keepdims=True) + RMS_EPS)
    ckvn = (ckvn * kvn_ref[...]).astype(BF16)
    krt = (z[:, COL_KR:COL_KR + 128] * tk_ref[...]).astype(BF16)
    kv = _dot(jnp.concatenate([ckvn, krt], axis=1), wkv_ref[...])
    k_ref[0] = kv[:, :QK_PAD].astype(BF16)
    one_col = (lax.broadcasted_iota(jnp.int32, (tm, HEAD_PAD - MLA_V_DIM), 1) == 0).astype(F32)
    for hh in range(MLA_HEADS):
        vh = kv[:, QK_PAD + hh * MLA_V_DIM:QK_PAD + (hh + 1) * MLA_V_DIM]
        v_ref[0, hh] = jnp.concatenate([vh, one_col], axis=1).astype(BF16)


def _inproj(x, l, pre_ln, ln_g, ln_b, mod4, w_ext, gm_g, gm_b, gm_w, gm_bias, qn, kvn, wq_ext, wkv_ext,
            tq_c, tq_s, tk):
    bsz, L, d = x.shape
    tm = INPROJ_ROWS
    per_layer = lambda a: _layer_spec(a, l)
    const = lambda shape: pl.BlockSpec(shape, lambda b, t: (0,) * len(shape))
    out_shape = (
        jax.ShapeDtypeStruct((bsz, L, GROUP_WIDTH), BF16),
        jax.ShapeDtypeStruct((L, bsz * GROUP_WIDTH), F32),
        jax.ShapeDtypeStruct((bsz, L, 4 * GROUP_WIDTH), F32),
        jax.ShapeDtypeStruct((bsz, L, QK_PAD), BF16),
        jax.ShapeDtypeStruct((bsz, L, QK_PAD), BF16),
        jax.ShapeDtypeStruct((bsz, MLA_HEADS, L, HEAD_PAD), BF16),
    )
    row = lambda w: pl.BlockSpec((1, tm, w), lambda b, t: (b, t, 0))
    return pl.pallas_call(
        functools.partial(_inproj_kernel, tm=tm, pre_ln=pre_ln, l=l),
        out_shape=out_shape,
        grid=(bsz, L // tm),
        in_specs=[row(d), const((1, d)), const((1, d)), _mod_spec(mod4, l),
                  per_layer(w_ext), _whole_spec(gm_g), _whole_spec(gm_b), per_layer(gm_w), per_layer(gm_bias),
                  per_layer(qn), per_layer(kvn), per_layer(wq_ext), per_layer(wkv_ext),
                  pl.BlockSpec((tm, HEAD_PAD), lambda b, t: (t, 0)),
                  pl.BlockSpec((tm, HEAD_PAD), lambda b, t: (t, 0)),
                  pl.BlockSpec((tm, 128), lambda b, t: (t, 0))],
        out_specs=(row(GROUP_WIDTH), pl.BlockSpec((tm, GROUP_WIDTH), lambda b, t: (t, b)),
                   row(4 * GROUP_WIDTH), row(QK_PAD), row(QK_PAD),
                   pl.BlockSpec((1, MLA_HEADS, tm, HEAD_PAD), lambda b, t: (b, 0, t, 0))),
        compiler_params=_cparams(("arbitrary", "arbitrary")),
        name="inproj",
    )(x, ln_g, ln_b, mod4, w_ext, gm_g, gm_b, gm_w, gm_bias, qn, kvn, wq_ext, wkv_ext, tq_c, tq_s, tk)


def _attn_kernel(q_ref, k_ref, v_ref, o_ref):
    L = q_ref.shape[1]
    tq = ATT_TQ
    q_chunk = lax.broadcasted_iota(jnp.int32, (tq, tq), 0) // CHUNK
    k_chunk = lax.broadcasted_iota(jnp.int32, (tq, tq), 1) // CHUNK
    allowed = k_chunk <= q_chunk
    nq = L // tq
    order = [t for pair in zip(range(nq // 2), range(nq - 1, nq // 2 - 1, -1)) for t in pair] + (
        [nq // 2] if nq % 2 else [])
    for qi in order:
        lk = (qi + 1) * tq
        outs = []
        for hh in range(ATT_HEADS_PER_STEP):
            lanes = slice(hh * HEAD_PAD, (hh + 1) * HEAD_PAD)
            qh = q_ref[0, qi * tq:(qi + 1) * tq, lanes]
            s = _dot_nt(qh, k_ref[0, 0:lk, lanes])
            s_diag = jnp.where(allowed, s[:, lk - tq:], -1e30)
            s = s_diag if qi == 0 else jnp.concatenate([s[:, :lk - tq], s_diag], axis=1)
            m = jnp.max(s, axis=-1, keepdims=True)
            p = jnp.exp2(s - m)
            pv = _dot(p.astype(BF16), v_ref[0, hh, 0:lk, :])
            outs.append(pv[:, 0:MLA_V_DIM] / pv[:, MLA_V_DIM:MLA_V_DIM + 1])
        o_ref[0, qi * tq:(qi + 1) * tq, :] = jnp.concatenate(outs, axis=1).astype(BF16)


def _attention(q, k, v):
    bsz, L, _ = q.shape
    hp = ATT_HEADS_PER_STEP
    return pl.pallas_call(
        _attn_kernel,
        out_shape=jax.ShapeDtypeStruct((bsz, L, GROUP_WIDTH), BF16),
        grid=(bsz, MLA_HEADS // hp),
        in_specs=[pl.BlockSpec((1, L, hp * HEAD_PAD), lambda b, g: (b, 0, g)),
                  pl.BlockSpec((1, L, hp * HEAD_PAD), lambda b, g: (b, 0, g)),
                  pl.BlockSpec((1, hp, L, HEAD_PAD), lambda b, g: (b, g, 0, 0))],
        out_specs=pl.BlockSpec((1, L, hp * MLA_V_DIM), lambda b, g: (b, 0, g)),
        compiler_params=_cparams(("arbitrary", "arbitrary")),
        name="mla_attention",
    )(q, k, v)


def _s5_kernel(u_ref, bmat_ref, are_ref, aim_ref, cmat_ref, dvec_ref, wglu_ref, bglu_ref,
               o_ref, h_ref, us_ref, bu_ref, hs_ref, os_ref, *, tc, nb, l):
    @pl.when(pl.program_id(0) == 0)
    def _():
        h_ref[...] = jnp.zeros(h_ref.shape, F32)

    nslab = GROUP_WIDTH // LANES
    for bb in range(nb):
        for s in range(nslab):
            lanes = slice(bb * GROUP_WIDTH + s * LANES, bb * GROUP_WIDTH + (s + 1) * LANES)
            us_ref[s, pl.ds(bb, tc, stride=nb), :] = u_ref[:, lanes]
    u = jnp.concatenate([us_ref[s] for s in range(nslab)], axis=1)
    bu_ref[...] = _dot(u.astype(BF16), bmat_ref[...])
    a_re = jnp.broadcast_to(are_ref[...], (nb, S5_NSTATE))
    a_im = jnp.broadcast_to(aim_ref[...], (nb, S5_NSTATE))
    re = slice(0, S5_NSTATE)
    im = slice(S5_NSTATE, 2 * S5_NSTATE)

    def step(i, carry):
        h_re, h_im = carry
        r0 = pl.ds(pl.multiple_of(i * 2 * nb, 2 * nb), nb)
        r1 = pl.ds(pl.multiple_of(i * 2 * nb, 2 * nb) + nb, nb)
        m_re = a_re * h_re - a_im * h_im + bu_ref[r0, re]
        m_im = a_re * h_im + a_im * h_re + bu_ref[r0, im]
        n_re = a_re * m_re - a_im * m_im + bu_ref[r1, re]
        n_im = a_re * m_im + a_im * m_re + bu_ref[r1, im]
        rr = pl.ds(pl.multiple_of(i * 2 * nb, 2 * nb), 2 * nb)
        hs_ref[rr, re] = jnp.concatenate([m_re, n_re], axis=0).astype(BF16)
        hs_ref[rr, im] = jnp.concatenate([m_im, n_im], axis=0).astype(BF16)
        return n_re, n_im

    h_re, h_im = lax.fori_loop(0, tc // 2, step, (h_ref[:, re], h_ref[:, im]))
    h_ref[:, re] = h_re
    h_ref[:, im] = h_im

    half = (tc * nb) // 2
    y = jnp.concatenate([_dot(hs_ref[0:half, :], cmat_ref[...]), _dot(hs_ref[half:2 * half, :], cmat_ref[...])],
                        axis=0) + dvec_ref[l:l + 1, :] * u
    z = _gelu(y)
    z_bf = z.astype(BF16)
    gate = _sigmoid(jnp.concatenate([_dot(z_bf[0:half], wglu_ref[...]), _dot(z_bf[half:2 * half], wglu_ref[...])],
                                    axis=0) + bglu_ref[l:l + 1, :])
    res = z * gate
    for s in range(nslab):
        os_ref[s] = res[:, s * LANES:(s + 1) * LANES]
    for bb in range(nb):
        for s in range(nslab):
            lanes = slice(bb * GROUP_WIDTH + s * LANES, bb * GROUP_WIDTH + (s + 1) * LANES)
            o_ref[:, lanes] = os_ref[s, pl.ds(bb, tc, stride=nb), :].astype(BF16)


def _s5(u_t, nb, l, bmat, a_re, a_im, cmat, dvec, wglu, bglu):
    L = u_t.shape[0]
    gw = GROUP_WIDTH
    tc = S5_CHUNK
    per_layer = lambda a: _layer_spec(a, l)
    return pl.pallas_call(
        functools.partial(_s5_kernel, tc=tc, nb=nb, l=l),
        out_shape=jax.ShapeDtypeStruct((L, nb * gw), BF16),
        grid=(L // tc,),
        in_specs=[pl.BlockSpec((tc, nb * gw), lambda t: (t, 0)),
                  per_layer(bmat), per_layer(a_re), per_layer(a_im), per_layer(cmat), _whole_spec(dvec),
                  per_layer(wglu), _whole_spec(bglu)],
        out_specs=pl.BlockSpec((tc, nb * gw), lambda t: (t, 0)),
        scratch_shapes=[pltpu.VMEM((nb, 2 * S5_NSTATE), F32),
                        pltpu.VMEM((gw // LANES, tc * nb, LANES), F32),
                        pltpu.VMEM((tc * nb, 2 * S5_NSTATE), F32),
                        pltpu.VMEM((tc * nb, 2 * S5_NSTATE), BF16),
                        pltpu.VMEM((gw // LANES, tc * nb, LANES), F32)],
        compiler_params=_cparams(("arbitrary",)),
        name="s5_scan",
    )(u_t, bmat, a_re, a_im, cmat, dvec, wglu, bglu)


def _hgrn_chunk(q, fl, iv, lb, tri, ones_bd, st):
    nsub = CHUNK // HG_SUB
    f = lb + (1.0 - lb) * _sigmoid(fl)
    logf = jnp.log(f)
    kk = 1.0 - f
    hi, lo = _split2(logf)
    b2 = (_dot(tri, hi) + _dot(tri, lo)) * LOG2E
    lane_head = lax.broadcasted_iota(jnp.int32, (HG_SUB, GROUP_WIDTH), 1) // HG_DK
    row_id = lax.broadcasted_iota(jnp.int32, (HG_SUB, GROUP_WIDTH), 0)
    v_bf = iv.astype(BF16)

    terms = []
    for si in range(nsub):
        r0 = si * HG_SUB
        q_i, k_i, b_i = (a[r0:r0 + HG_SUB] for a in (q, kk, b2))
        for j in range(HG_SUB):
            e = jnp.exp2(jnp.where(row_id >= j, b_i - b_i[j:j + 1], -1e30))
            terms.append(q_i * k_i[j:j + 1] * e)
    p = _dot(jnp.concatenate(terms, axis=0).astype(BF16), ones_bd)
    o_parts = []
    for si in range(nsub):
        r0 = si * HG_SUB
        base = si * HG_SUB * HG_SUB
        o_i = p[base:base + HG_SUB] * iv[r0:r0 + 1]
        for j in range(1, HG_SUB):
            o_i = o_i + p[base + j * HG_SUB:base + (j + 1) * HG_SUB] * iv[r0 + j:r0 + j + 1]
        o_parts.append(o_i)

    atts = []
    for si in range(1, nsub):
        r0 = si * HG_SUB
        b_edge = b2[r0 - 1:r0]
        q_t = q[r0:r0 + HG_SUB] * jnp.exp2(b2[r0:r0 + HG_SUB] - b_edge)
        k_t = jnp.concatenate([kk[0:r0] * jnp.exp2(b_edge - b2[0:r0]),
                               jnp.zeros((CHUNK - r0, GROUP_WIDTH), F32)], axis=0).astype(BF16)
        q_exp = jnp.concatenate([jnp.where(lane_head == hh, q_t, 0.0) for hh in range(HG_HEADS)],
                                axis=0).astype(BF16)
        atts.append(_dot_nt(q_exp, k_t))
    o2 = _dot(jnp.concatenate(atts, axis=0).astype(BF16), v_bf)
    for si in range(1, nsub):
        base = (si - 1) * HG_HEADS * HG_SUB
        for hh in range(HG_HEADS):
            o_parts[si] = o_parts[si] + jnp.where(lane_head == hh,
                                                  o2[base + hh * HG_SUB:base + (hh + 1) * HG_SUB], 0.0)
    o_intra = jnp.concatenate(o_parts, axis=0)

    o_inter = _dot_nt((q * jnp.exp2(b2)).astype(BF16), st.astype(BF16))
    b_last = b2[CHUNK - 1:CHUNK]
    k_hat = (kk * jnp.exp2(b_last - b2)).astype(BF16)
    upd = _dot_tn(v_bf, k_hat)
    bd = (lax.broadcasted_iota(jnp.int32, (GROUP_WIDTH, GROUP_WIDTH), 0) // HG_DK
          == lax.broadcasted_iota(jnp.int32, (GROUP_WIDTH, GROUP_WIDTH), 1) // HG_DK)
    st_new = st * jnp.exp2(b_last) + jnp.where(bd, upd, 0.0)
    return o_intra + o_inter, st_new


def _hgrn_kernel(d_ref, lb_ref, gn_ref, tri_ref, ones_ref, o_ref, st_ref, *, th, l):
    @pl.when(pl.program_id(1) == 0)
    def _():
        st_ref[...] = jnp.zeros(st_ref.shape, F32)

    lb = lb_ref[l:l + 1, :]
    tri = tri_ref[...]
    ones_bd = ones_ref[...]
    for c in range(th // CHUNK):
        rows = slice(c * CHUNK, (c + 1) * CHUNK)
        for bb in range(d_ref.shape[0]):
            q = d_ref[bb, rows, 0:GROUP_WIDTH]
            fl = d_ref[bb, rows, GROUP_WIDTH:2 * GROUP_WIDTH]
            iv = d_ref[bb, rows, 2 * GROUP_WIDTH:3 * GROUP_WIDTH]
            g = d_ref[bb, rows, 3 * GROUP_WIDTH:4 * GROUP_WIDTH]
            o, st_new = _hgrn_chunk(q, fl, iv, lb, tri, ones_bd, st_ref[bb])
            st_ref[bb] = st_new
            ms = _dot((o * o).astype(BF16), ones_bd) * (1.0 / HG_DK)
            o = o * lax.rsqrt(ms + RMS_EPS) * gn_ref[l:l + 1, :]
            o_ref[bb, rows, :] = (o * _sigmoid(g)).astype(BF16)


def _hgrn(d, l, lb, gn, tri, ones_bd):
    bsz, L, _ = d.shape
    th = HG_ROWS
    nbb = HG_BATCH if bsz % HG_BATCH == 0 else 1
    const = lambda shape: pl.BlockSpec(shape, lambda b, t: (0,) * len(shape))
    return pl.pallas_call(
        functools.partial(_hgrn_kernel, th=th, l=l),
        out_shape=jax.ShapeDtypeStruct((bsz, L, GROUP_WIDTH), BF16),
        grid=(bsz // nbb, L // th),
        in_specs=[pl.BlockSpec((nbb, th, 4 * GROUP_WIDTH), lambda b, t: (b, t, 0)),
                  _whole_spec(lb), _whole_spec(gn),
                  const((CHUNK, CHUNK)), const((GROUP_WIDTH, GROUP_WIDTH))],
        out_specs=pl.BlockSpec((nbb, th, GROUP_WIDTH), lambda b, t: (b, t, 0)),
        scratch_shapes=[pltpu.VMEM((nbb, GROUP_WIDTH, GROUP_WIDTH), F32)],
        compiler_params=_cparams(("arbitrary", "arbitrary")),
        name="hgrn2",
    )(d, lb, gn, tri, ones_bd)


def _merge_kernel(ya_ref, yb_ref, yc_ref, yd_ref, x_ref, lng_ref, lnb_ref, mod_ref, mg_ref, wo_ref, g_ref, b_ref,
                  o_ref, *, pre_ln, l):
    acc = None
    for gi, y_ref in enumerate((ya_ref, yb_ref, yc_ref, yd_ref)):
        y = (y_ref[...] if gi == 2 else y_ref[0]).astype(F32)
        yn = y * lax.rsqrt(jnp.mean(y * y, axis=-1, keepdims=True) + RMS_EPS)
        yn = (yn * mg_ref[l:l + 1, gi * GROUP_WIDTH:(gi + 1) * GROUP_WIDTH]).astype(BF16)
        part = _dot(yn, wo_ref[gi * GROUP_WIDTH:(gi + 1) * GROUP_WIDTH, :])
        acc = part if acc is None else acc + part
    g1 = mod_ref[0, :, 2 * D_MODEL:3 * D_MODEL]
    x = x_ref[0]
    if pre_ln:
        x = _layer_norm(x, lng_ref[...], lnb_ref[...])
    o_ref[0] = _layer_norm(DN_ALPHA * x + (1.0 + g1) * acc, g_ref[l:l + 1, :], b_ref[l:l + 1, :])


def _merge(ya, yb, yc, yd, x, l, pre_ln, ln_g, ln_b, mod4, mg, wo, g, b):
    bsz, L, d = x.shape
    tm = MERGE_ROWS
    per_layer = lambda a: _layer_spec(a, l)
    const = lambda shape: pl.BlockSpec(shape, lambda bb, t: (0,) * len(shape))
    row = lambda w: pl.BlockSpec((1, tm, w), lambda bb, t: (bb, t, 0))
    return pl.pallas_call(
        functools.partial(_merge_kernel, pre_ln=pre_ln, l=l),
        out_shape=jax.ShapeDtypeStruct((bsz, L, d), F32),
        grid=(bsz, L // tm),
        in_specs=[row(GROUP_WIDTH), row(GROUP_WIDTH), pl.BlockSpec((tm, GROUP_WIDTH), lambda bb, t: (t, bb)),
                  row(GROUP_WIDTH), row(d), const((1, d)), const((1, d)), _mod_spec(mod4, l),
                  _whole_spec(mg), per_layer(wo), _whole_spec(g), _whole_spec(b)],
        out_specs=row(d),
        compiler_params=_cparams(("arbitrary", "arbitrary")),
        name="merge_out",
    )(ya, yb, yc, yd, x, ln_g, ln_b, mod4, mg, wo, g, b)


def _ffn_kernel(x_ref, xh_ref, mod_ref, wup_ref, cw_ref, cb_ref, wdn_ref, g_ref, b_ref, o_ref,
                hp_ref, hx_ref, a_ref, act_ref, ap_ref, *, tm, l):
    ti = pl.program_id(1)
    seg = tm // FF_SEGS
    nslab = D_MODEL // LANES
    sh2 = mod_ref[0, :, 3 * D_MODEL:4 * D_MODEL]
    sc2 = mod_ref[0, :, 4 * D_MODEL:5 * D_MODEL]
    g2 = mod_ref[0, :, 5 * D_MODEL:6 * D_MODEL]
    x = x_ref[0]
    h2 = x * (1.0 + sc2) + sh2
    halo = jnp.where(ti > 0, xh_ref[0] * (1.0 + sc2) + sh2, 0.0)
    for s in range(nslab):
        lanes = slice(s * LANES, (s + 1) * LANES)
        hp_ref[s, 0:FF_HALO, :] = halo[:, lanes]
        for p in range(FF_SEGS):
            hp_ref[s, pl.ds(FF_HALO + p, seg, stride=FF_SEGS), :] = h2[p * seg:(p + 1) * seg, lanes]
    hx_ref[...] = jnp.concatenate([hp_ref[s] for s in range(nslab)], axis=1).astype(BF16)
    nj = D_FF // FF_TILE
    sub0 = lax.broadcasted_iota(jnp.int32, (FF_SEGS, FF_TILE), 0) == 0

    def up(j):
        hx = hx_ref[...]
        for half in range(2):
            a = a_ref.at[j % 2, half]
            cols = slice(half * D_FF + j * FF_TILE, half * D_FF + (j + 1) * FF_TILE)
            a[...] = _dot(hx, wup_ref[:, cols])
            prev1 = jnp.where(sub0, a[FF_HALO - 1:FF_HALO, :], pltpu.roll(a[FF_HALO + tm - 8:FF_HALO + tm, :], 1, 0))
            prev2 = jnp.where(sub0, a[FF_HALO - 2:FF_HALO - 1, :],
                              pltpu.roll(a[FF_HALO + tm - 16:FF_HALO + tm - 8, :], 1, 0))
            a[8:16, :] = prev1
            a[0:8, :] = prev2

    def conv(j, half):
        cols = slice(half * D_FF + j * FF_TILE, half * D_FF + (j + 1) * FF_TILE)
        a = a_ref.at[j % 2, half]
        return (a[0:tm, :] * cw_ref[0:1, cols] + a[8:tm + 8, :] * cw_ref[1:2, cols]
                + a[16:tm + 16, :] * cw_ref[2:3, cols] + cb_ref[l:l + 1, cols])

    up(0)
    acc = None
    for j in range(nj):
        if j + 1 < nj:
            up(j + 1)
        gate = conv(j, 1)
        act_ref[:, j * FF_TILE:(j + 1) * FF_TILE] = (conv(j, 0) * (gate * _sigmoid(gate))).astype(BF16)
        if j + 1 in FF_DOWN_SPLITS:
            k0 = ([0] + [e for e in FF_DOWN_SPLITS if e <= j])[-1] * FF_TILE
            k1 = (j + 1) * FF_TILE
            part = _dot(act_ref[:, k0:k1], wdn_ref[k0:k1, :])
            acc = part if acc is None else acc + part
    for s in range(nslab):
        ap_ref[s] = acc[:, s * LANES:(s + 1) * LANES]
    acc_nat = jnp.concatenate(
        [jnp.concatenate([ap_ref[s, pl.ds(p, seg, stride=FF_SEGS), :] for p in range(FF_SEGS)], axis=0)
         for s in range(nslab)], axis=1)
    o_ref[0] = _layer_norm(DN_ALPHA * x + (1.0 + g2) * acc_nat, g_ref[l:l + 1, :], b_ref[l:l + 1, :])


def _ffn(x, l, mod4, wup, cw, cb, wdn, g, b):
    bsz, L, d = x.shape
    tm = FF_ROWS
    per_layer = lambda a: _layer_spec(a, l, pipeline_mode=pl.Buffered(1))
    hb = tm // FF_HALO
    return pl.pallas_call(
        functools.partial(_ffn_kernel, tm=tm, l=l),
        out_shape=jax.ShapeDtypeStruct((bsz, L, d), F32),
        grid=(bsz, L // tm),
        in_specs=[pl.BlockSpec((1, tm, d), lambda bb, t: (bb, t, 0)),
                  pl.BlockSpec((1, FF_HALO, d), lambda bb, t: (bb, jnp.maximum(t * hb - 1, 0), 0)),
                  _mod_spec(mod4, l),
                  per_layer(wup), per_layer(cw), _whole_spec(cb), per_layer(wdn), _whole_spec(g), _whole_spec(b)],
        out_specs=pl.BlockSpec((1, tm, d), lambda bb, t: (bb, t, 0)),
        scratch_shapes=[pltpu.VMEM((d // LANES, tm + FF_HALO, LANES), F32),
                        pltpu.VMEM((tm + FF_HALO, d), BF16),
                        pltpu.VMEM((2, 2, tm + FF_HALO, FF_TILE), F32),
                        pltpu.VMEM((tm, D_FF), BF16),
                        pltpu.VMEM((d // LANES, tm, LANES), F32)],
        compiler_params=_cparams(("arbitrary", "arbitrary")),
        name="conv_ffn",
    )(x, x, mod4, wup, cw, cb, wdn, g, b)


def _rope_tables(L):
    inv = 1.0 / (ROPE_THETA ** (jnp.arange(0, MLA_ROPE_DIM, 2, dtype=F32) / MLA_ROPE_DIM))
    ang = jnp.arange(L, dtype=F32)[:, None] * inv[None, :]
    cos, sin = jnp.cos(ang), jnp.sin(ang)
    cc = jnp.concatenate([cos, cos], axis=1)
    ss = jnp.concatenate([-sin, sin], axis=1)
    scale = MLA_QK_DIM ** -0.5 * LOG2E
    zeros32 = jnp.zeros((L, HEAD_PAD - MLA_QK_DIM), F32)
    tq_c = jnp.concatenate([jnp.ones((L, MLA_NOPE_DIM), F32), cc, zeros32], axis=1) * scale
    tq_s = jnp.concatenate([jnp.zeros((L, MLA_NOPE_DIM), F32), ss, zeros32], axis=1) * scale
    tk = jnp.concatenate([cc, ss, jnp.zeros((L, 64), F32)], axis=1)
    return tq_c, tq_s, tk


def _swap_halves(w):
    half = MLA_ROPE_DIM // 2
    return jnp.concatenate([w[..., half:], w[..., :half]], axis=-1)


def _prep_inproj_weights(w_in_l, w_uq_l, w_ukv_l, q_norm_l, kv_norm_l):
    d = w_in_l.shape[0]
    a_u, a_v, cq, ckv, kr, c_u, d_q, d_f, d_i, d_g = jnp.split(
        w_in_l, (256, 512, 704, 832, 864, 1120, 1376, 1632, 1888), axis=1)
    z64 = jnp.zeros((d, 64), F32)
    w_ext = jnp.concatenate([a_u, a_v, c_u, d_q, d_f, d_i, d_g, cq, z64, ckv, kr, _swap_halves(kr), z64],
                            axis=1).astype(BF16)
    wq = w_uq_l.reshape(MLA_Q_RANK, MLA_HEADS, MLA_QK_DIM)
    zq32 = jnp.zeros((MLA_Q_RANK, MLA_HEADS, HEAD_PAD - MLA_QK_DIM), F32)
    zq64 = jnp.zeros((MLA_Q_RANK, MLA_HEADS, MLA_NOPE_DIM), F32)
    wq_a = jnp.concatenate([wq, zq32], axis=2).reshape(MLA_Q_RANK, QK_PAD)
    wq_b = jnp.concatenate([zq64, _swap_halves(wq[:, :, MLA_NOPE_DIM:]), zq32], axis=2).reshape(MLA_Q_RANK, QK_PAD)
    wq_ext = jnp.concatenate([wq_a, wq_b], axis=1)
    wq_ext = jnp.concatenate([wq_ext, jnp.zeros((256 - MLA_Q_RANK, 2 * QK_PAD), F32)], axis=0).astype(BF16)
    qn = jnp.concatenate([q_norm_l, jnp.zeros((256 - MLA_Q_RANK,), F32)]).reshape(1, 256)
    wkv = w_ukv_l.reshape(MLA_KV_RANK, MLA_HEADS, MLA_NOPE_DIM + MLA_V_DIM)
    zk64 = jnp.zeros((MLA_KV_RANK, MLA_HEADS, HEAD_PAD - MLA_NOPE_DIM), F32)
    wk = jnp.concatenate([wkv[:, :, :MLA_NOPE_DIM], zk64], axis=2).reshape(MLA_KV_RANK, QK_PAD)
    wv = wkv[:, :, MLA_NOPE_DIM:].reshape(MLA_KV_RANK, MLA_HEADS * MLA_V_DIM)
    eye = jnp.eye(MLA_ROPE_DIM, dtype=F32)
    place_h = jnp.concatenate([jnp.zeros((MLA_ROPE_DIM, MLA_NOPE_DIM), F32), eye,
                               jnp.zeros((MLA_ROPE_DIM, HEAD_PAD - MLA_QK_DIM), F32)], axis=1)
    place = jnp.concatenate([place_h] * MLA_HEADS, axis=1)
    place = jnp.concatenate([place, jnp.zeros((MLA_ROPE_DIM, MLA_HEADS * MLA_V_DIM), F32)], axis=1)
    top = jnp.concatenate([wk, wv], axis=1)
    wkv_ext = jnp.concatenate([top, place, place, jnp.zeros((64, QK_PAD + 256), F32)], axis=0).astype(BF16)
    kvn = kv_norm_l.reshape(1, MLA_KV_RANK)
    return w_ext, wq_ext, wkv_ext, qn, kvn


def _prep_gm(gm_ws_l, gm_bs_l):
    ck = jnp.arange(GM_BLOCK) // CHUNK
    mask = ck[None, :] <= ck[:, None]
    w = jnp.where(mask[None], gm_ws_l, 0)
    w = jnp.transpose(w, (1, 0, 2)).reshape(GM_BLOCK, GM_HEADS * GM_BLOCK).astype(BF16)
    bias = jnp.repeat(gm_bs_l.T, GROUP_WIDTH // GM_HEADS, axis=1)
    return w, bias


def _prep_s5(a_re, a_im, b_re, b_im, c_re, c_im, log_step):
    dt = jnp.exp(log_step)[:, None]
    mag = jnp.exp(dt * a_re)
    abar_re, abar_im = mag * jnp.cos(dt * a_im), mag * jnp.sin(dt * a_im)
    den = a_re * a_re + a_im * a_im
    nr, ni = abar_re - 1.0, abar_im
    zoh_re = (nr * a_re + ni * a_im) / den
    zoh_im = (ni * a_re - nr * a_im) / den
    bbar_re = zoh_re[..., None] * b_re - zoh_im[..., None] * b_im
    bbar_im = zoh_re[..., None] * b_im + zoh_im[..., None] * b_re
    eye_g = jnp.eye(S5_GROUPS, dtype=F32)
    bm_re = jnp.einsum('gpn,gh->gnhp', bbar_re, eye_g).reshape(GROUP_WIDTH, S5_NSTATE)
    bm_im = jnp.einsum('gpn,gh->gnhp', bbar_im, eye_g).reshape(GROUP_WIDTH, S5_NSTATE)
    bmat = jnp.concatenate([bm_re, bm_im], axis=1).astype(BF16)
    cm_re = jnp.einsum('gnp,gh->gphn', c_re, eye_g).reshape(S5_NSTATE, GROUP_WIDTH)
    cm_im = jnp.einsum('gnp,gh->gphn', c_im, eye_g).reshape(S5_NSTATE, GROUP_WIDTH)
    cmat = jnp.concatenate([cm_re, -cm_im], axis=0).astype(BF16)
    return bmat, abar_re.reshape(1, S5_NSTATE), abar_im.reshape(1, S5_NSTATE), cmat


def kernel(x, c, ln_in_g, ln_in_b, hg_lb_logits, w_ada, b_ada, w_in, gm_ln_g, gm_ln_b, gm_ws, gm_bs, mla_q_norm, mla_kv_norm, mla_w_uq, mla_w_ukv, s5_a_re, s5_a_im, s5_b_re, s5_b_im, s5_c_re, s5_c_im, s5_d, s5_log_step, s5_w_glu, s5_b_glu, hg_norm_g, merge_g, w_out, ln1_g, ln1_b, w_up, conv_w, conv_b, w_down, ln2_g, ln2_b):
    bsz, L, d = x.shape
    nl = w_in.shape[0]
    mod = _ada(c, w_ada, b_ada)
    lb_cum = jnp.cumsum(jax.nn.softmax(hg_lb_logits.astype(F32), axis=0), axis=0)
    tq_c, tq_s, tk = _rope_tables(L)
    tri = jnp.tril(jnp.ones((CHUNK, CHUNK), F32)).astype(BF16)
    hid = jnp.arange(GROUP_WIDTH) // HG_DK
    ones_bd = (hid[:, None] == hid[None, :]).astype(BF16)
    w_ext, wq_ext, wkv_ext, qn, kvn = jax.vmap(_prep_inproj_weights)(w_in, mla_w_uq, mla_w_ukv, mla_q_norm,
                                                                     mla_kv_norm)
    gm_w, gm_bias = jax.vmap(_prep_gm)(gm_ws, gm_bs)
    bmat, a_re, a_im, cmat = jax.vmap(_prep_s5)(s5_a_re, s5_a_im, s5_b_re, s5_b_im, s5_c_re, s5_c_im, s5_log_step)
    wglu_bf, wout_bf, wup_bf, wdn_bf = (w.astype(BF16) for w in (s5_w_glu, w_out, w_up, w_down))

    mod4 = mod.reshape(nl, bsz, 1, 6 * d)
    lbs = lb_cum - lb_cum[0:1]

    xs = x
    ln_g, ln_b = ln_in_g.reshape(1, d), ln_in_b.reshape(1, d)
    for l in range(nl):
        pre_ln = l == 0
        y_a, c_u, dd, q, k, v = _inproj(xs, l, pre_ln, ln_g, ln_b, mod4, w_ext, gm_ln_g, gm_ln_b, gm_w, gm_bias,
                                        qn, kvn, wq_ext, wkv_ext, tq_c, tq_s, tk)
        y_b = _attention(q, k, v)
        y_c = _s5(c_u, bsz, l, bmat, a_re, a_im, cmat, s5_d, wglu_bf, s5_b_glu)
        y_d = _hgrn(dd, l, lbs, hg_norm_g, tri, ones_bd)
        xs = _merge(y_a, y_b, y_c, y_d, xs, l, pre_ln, ln_g, ln_b, mod4, merge_g, wout_bf, ln1_g, ln1_b)
        xs = _ffn(xs, l, mod4, wup_bf, conv_w, conv_b, wdn_bf, ln2_g, ln2_b)
    return xs
```
